```python
import jax, jax.numpy as jnp
from jax import lax
import numpy as np

D_MODEL = 2048
BATCH = 2
SEQ = 4096
DEPTH = 2
DEC_BATCH = 128
DEC_SEQ = 8
PAST_LEN = 8192
PAGE_SIZE = 128

ROPE_THETA = 500000.0
EPS = 1e-6
NEG_INF = -1e30
Q_BLOCK = 128
F32 = jnp.float32

H_A = 8
KV_A = 2
DH_A = 64
ROT_A = DH_A // 4
LAMBDA_INIT = 0.2

H_B = 8
Q_LORA = 512
KV_LORA = 256
NOPE_B = 64
ROPE_B = 32
QK_B = NOPE_B + ROPE_B
VH_B = 128

H_C = 8
DH_C = 128
ROT_C = DH_C // 4
DIL_GROUPS = ((128, 1), (512, 4), (2048, 16))
N_DIL = len(DIL_GROUPS)

D_FF = 256 * ((8 * D_MODEL // 3 + 255) // 256)
N_EXPERTS = 8
TOP_K = 2
D_FF_E = 7 * D_MODEL // 2

A_Q = H_A * 2 * DH_A
A_KV = KV_A * 2 * DH_A
SPLIT0 = (A_Q, A_Q + A_KV, A_Q + 2 * A_KV, A_Q + 2 * A_KV + Q_LORA)
IN0 = A_Q + 2 * A_KV + Q_LORA + KV_LORA + ROPE_B
OUT0 = H_A * 2 * DH_A + H_B * VH_B
IN1 = 3 * N_DIL * H_C * DH_C
OUT1 = H_C * DH_C

kernel_name = 'hybrid_diff_mla_dilated_decoder_step'


def rms_norm(x, g):
    xf = x.astype(F32)
    y = xf * lax.rsqrt(jnp.mean(xf * xf, axis=-1, keepdims=True) + EPS)
    return (y * g.astype(F32)).astype(x.dtype)


def rope(x, pos, rot):
    half = rot // 2
    inv_freq = ROPE_THETA ** (-jnp.arange(half, dtype=F32) / half)
    ang = pos.astype(F32)[:, None] * inv_freq[None, :]
    cos = jnp.cos(ang)[:, None, :]
    sin = jnp.sin(ang)[:, None, :]
    xf = x.astype(F32)
    x1, x2 = xf[..., :half], xf[..., half:rot]
    out = jnp.concatenate([x1 * cos - x2 * sin, x1 * sin + x2 * cos, xf[..., rot:]], axis=-1)
    return out.astype(x.dtype)


def adaln(c, w, b):
    m = (jax.nn.silu(c) @ w + b)[:, None, :]
    return jnp.split(m, 6, axis=-1)


def swiglu(h, w_gate, w_up, w_down):
    return (jax.nn.silu(h @ w_gate) * (h @ w_up)) @ w_down


def moe_swiglu(h, router, w_gate, w_up, w_down):
    logits = jnp.einsum('bsd,de->bse', h, router, preferred_element_type=F32)
    top_val, top_idx = lax.top_k(logits, TOP_K)
    gates = jax.nn.softmax(top_val, axis=-1)
    dense_gate = jnp.sum(jax.nn.one_hot(top_idx, N_EXPERTS, dtype=F32) * gates[..., None], axis=-2).astype(h.dtype)
    out = jnp.zeros_like(h)
    for e in range(N_EXPERTS):
        out = out + dense_gate[..., e:e + 1] * swiglu(h, w_gate[e], w_up[e], w_down[e])
    return out


def diff_lambda(lam_params):
    lf = lam_params.astype(F32)
    return jnp.exp(jnp.sum(lf[0] * lf[1])) - jnp.exp(jnp.sum(lf[2] * lf[3])) + LAMBDA_INIT


def to_blocks(a):
    b, s = a.shape[:2]
    return jnp.swapaxes(a.reshape((b, s // Q_BLOCK, Q_BLOCK) + a.shape[2:]), 0, 1)


def from_blocks(a):
    a = jnp.swapaxes(a, 0, 1)
    return a.reshape(a.shape[0], -1, a.shape[-1])


def gather_past(cache, pt, new_rows):
    rows = cache[pt].reshape((-1,) + cache.shape[2:])
    return jnp.concatenate([rows, new_rows], axis=0)[None]


def proj0(h, pos, w):
    b, s = h.shape[:2]
    qa, ka, va, qc, kvc = jnp.split(h @ w['l0_w_in'], SPLIT0, axis=-1)
    qa = rope(rms_norm(qa.reshape(b, s, 2 * H_A, DH_A), w['l0_a_qnorm']), pos, ROT_A)
    ka = rope(rms_norm(ka.reshape(b, s, 2 * KV_A, DH_A), w['l0_a_knorm']), pos, ROT_A)
    qb = (rms_norm(qc, w['l0_b_qa_norm']) @ w['l0_b_w_uq']).reshape(b, s, H_B, QK_B)
    qb = jnp.concatenate([qb[..., :NOPE_B], rope(qb[..., NOPE_B:], pos, ROPE_B)], axis=-1)
    qb = rms_norm(qb, w['l0_b_qnorm'])
    lat = rms_norm(kvc[..., :KV_LORA], w['l0_b_kv_norm'])
    krope = rope(kvc[..., None, KV_LORA:], pos, ROPE_B)[:, :, 0]
    return (qa.reshape(b, s, H_A, 2, DH_A), ka.reshape(b, s, KV_A, 2 * DH_A),
            va.reshape(b, s, KV_A, 2 * DH_A), qb, lat, krope)


def proj1(h, pos, w):
    b, s = h.shape[:2]
    z = (h @ w['l1_w_in']).reshape(b, s, 3, N_DIL * H_C, DH_C)
    q = rope(rms_norm(z[:, :, 0], w['l1_c_qnorm']), pos, ROT_C).reshape(b, s, N_DIL, H_C, DH_C)
    k = rope(rms_norm(z[:, :, 1], w['l1_c_knorm']), pos, ROT_C).reshape(b, s, N_DIL, H_C, DH_C)
    v = z[:, :, 2].reshape(b, s, N_DIL, H_C, DH_C)
    return q, k, v


def diff_attn(q, k, v, q_pos, k_pos, lam, subln):
    b, sq = q.shape[:2]
    qg = q.reshape(b, sq, KV_A, H_A // KV_A, 2, DH_A)
    kc = k.reshape(k.shape[0], k.shape[1], KV_A, 2, DH_A)
    s = jnp.einsum('bqkgcd,bskcd->bckgqs', qg, kc, preferred_element_type=F32) * DH_A ** -0.5
    mask = (k_pos[None, :] <= q_pos[:, None])
    p = jax.nn.softmax(jnp.where(mask, s, NEG_INF), axis=-1)
    attn = p[:, 0] - lam * p[:, 1]
    o = jnp.einsum('bkgqs,bskd->bqkgd', attn, v.astype(F32)).reshape(b, sq, H_A, 2 * DH_A)
    o = rms_norm(o, subln) * (1.0 - LAMBDA_INIT)
    return o.reshape(b, sq, H_A * 2 * DH_A).astype(q.dtype)


def mla_kv(lat, krope, w_ukv, knorm):
    kv = jnp.einsum('bsl,lhd->bshd', lat, w_ukv.reshape(KV_LORA, H_B, NOPE_B + VH_B))
    kr = jnp.broadcast_to(krope[:, :, None, :], kv.shape[:3] + (ROPE_B,))
    k = rms_norm(jnp.concatenate([kv[..., :NOPE_B], kr], axis=-1), knorm)
    return k, kv[..., NOPE_B:]


def softmax_attn(q, k, v, q_pos, k_pos):
    s = jnp.einsum('bqhd,bshd->bhqs', q, k, preferred_element_type=F32) * q.shape[-1] ** -0.5
    s = jnp.where((k_pos[None, :] <= q_pos[:, None])[None, None], s, NEG_INF)
    p = jax.nn.softmax(s, axis=-1)
    o = jnp.einsum('bhqs,bshd->bqhd', p, v.astype(F32))
    b, sq = q.shape[:2]
    return o.reshape(b, sq, -1).astype(q.dtype)


def dilated_group(q, kbuf, vbuf, q_idx, window, dil):
    offs = dil * jnp.arange(window // dil + 1, dtype=jnp.int32)
    idx = q_idx[:, None] - offs[None, :]
    valid = idx >= 0
    idx = jnp.maximum(idx, 0)
    kg = kbuf[:, idx]
    vg = vbuf[:, idx]
    s = jnp.einsum('bqhd,bqnhd->bhqn', q, kg, preferred_element_type=F32) * DH_C ** -0.5
    s = jnp.where(valid[None, None], s, NEG_INF)
    lse = jax.nn.logsumexp(s, axis=-1)
    o = jnp.einsum('bhqn,bqnhd->bqhd', jnp.exp(s - lse[..., None]), vg.astype(F32))
    return o, lse


def dilated_mix(q, kbufs, vbufs, q_idx):
    outs, lses = [], []
    for g, (win, dil) in enumerate(DIL_GROUPS):
        o, lse = dilated_group(q[:, :, g], kbufs[g], vbufs[g], q_idx[g], win, dil)
        outs.append(o)
        lses.append(lse)
    wts = jax.nn.softmax(jnp.stack(lses, axis=-1), axis=-1)
    o = jnp.einsum('bhqg,bqhgd->bqhd', wts, jnp.stack(outs, axis=3))
    b, sq = q.shape[:2]
    return o.reshape(b, sq, OUT1).astype(q.dtype)


def run_group(x, c, pos, w, attn0, attn1):
    new_state = []
    for layer in range(DEPTH):
        sh1, sc1, g1, sh2, sc2, g2 = adaln(c, w['ada_w'][layer], w['ada_b'][layer])
        h = rms_norm(x, w['norm_mix'][layer]) * (1 + sc1) + sh1
        if layer % 2 == 0:
            qa, ka, va, qb, lat, krope = proj0(h, pos, w)
            x = x + g1 * (attn0(qa, ka, va, qb, lat, krope) @ w['l0_w_out'])
            new_state += [ka, va, lat, krope]
            h = rms_norm(x, w['norm_ffn'][layer]) * (1 + sc2) + sh2
            x = x + g2 * swiglu(h, w['l0_ffn_gate'], w['l0_ffn_up'], w['l0_ffn_down'])
        else:
            q, k, v = proj1(h, pos, w)
            o, win_states = attn1(q, k, v)
            x = x + g1 * (o @ w['l1_w_out'])
            new_state += win_states
            h = rms_norm(x, w['norm_ffn'][layer]) * (1 + sc2) + sh2
            x = x + g2 * moe_swiglu(h, w['l1_router'], w['l1_moe_gate'], w['l1_moe_up'], w['l1_moe_down'])
    return x, new_state


def setup_inputs(seed: int = 0) -> dict:
    key = jax.random.key(seed)
    keys = iter(jax.random.split(key, 64))

    def nrm(shape, scale=1.0):
        return jax.random.normal(next(keys), shape, jnp.float32) * scale

    def gain(*shape):
        return 1.0 + nrm(shape, 0.1)

    n_pages = PAST_LEN // PAGE_SIZE
    n_used = DEC_BATCH * n_pages
    n_pool = n_used + n_used // 4
    page_table = jax.random.permutation(next(keys), n_pool)[:n_used].reshape(DEC_BATCH, n_pages).astype(jnp.int32)
    return {
        'x_prompt': nrm((BATCH, SEQ, D_MODEL)),
        'x_sample': nrm((DEC_BATCH, DEC_SEQ, D_MODEL)),
        'c_prompt': nrm((BATCH, D_MODEL)),
        'c_sample': nrm((DEC_BATCH, D_MODEL)),
        'page_table': page_table,
        'cache_a_k': nrm((n_pool, PAGE_SIZE, KV_A, 2 * DH_A)),
        'cache_a_v': nrm((n_pool, PAGE_SIZE, KV_A, 2 * DH_A)),
        'cache_b_lat': nrm((n_pool, PAGE_SIZE, KV_LORA)),
        'cache_b_krope': nrm((n_pool, PAGE_SIZE, ROPE_B)),
        'state_c_win0': nrm((DEC_BATCH, min(DIL_GROUPS[0][0], PAST_LEN), 2, H_C, DH_C)),
        'state_c_win1': nrm((DEC_BATCH, min(DIL_GROUPS[1][0], PAST_LEN), 2, H_C, DH_C)),
        'state_c_win2': nrm((DEC_BATCH, min(DIL_GROUPS[2][0], PAST_LEN), 2, H_C, DH_C)),
        'ada_w': nrm((DEPTH, D_MODEL, 6 * D_MODEL), D_MODEL ** -0.5),
        'ada_b': nrm((DEPTH, 6 * D_MODEL), 0.02),
        'norm_mix': gain(DEPTH, D_MODEL),
        'norm_ffn': gain(DEPTH, D_MODEL),
        'l0_w_in': nrm((D_MODEL, IN0), D_MODEL ** -0.5),
        'l0_a_qnorm': gain(DH_A),
        'l0_a_knorm': gain(DH_A),
        'l0_a_lambda': nrm((4, DH_A), 0.1),
        'l0_a_subln': gain(2 * DH_A),
        'l0_b_qa_norm': gain(Q_LORA),
        'l0_b_w_uq': nrm((Q_LORA, H_B * QK_B), Q_LORA ** -0.5),
        'l0_b_kv_norm': gain(KV_LORA),
        'l0_b_w_ukv': nrm((KV_LORA, H_B * (NOPE_B + VH_B)), KV_LORA ** -0.5),
        'l0_b_qnorm': gain(QK_B),
        'l0_b_knorm': gain(QK_B),
        'l0_w_out': nrm((OUT0, D_MODEL), OUT0 ** -0.5),
        'l0_ffn_gate': nrm((D_MODEL, D_FF), D_MODEL ** -0.5),
        'l0_ffn_up': nrm((D_MODEL, D_FF), D_MODEL ** -0.5),
        'l0_ffn_down': nrm((D_FF, D_MODEL), D_FF ** -0.5),
        'l1_w_in': nrm((D_MODEL, IN1), D_MODEL ** -0.5),
        'l1_c_qnorm': gain(DH_C),
        'l1_c_knorm': gain(DH_C),
        'l1_w_out': nrm((OUT1, D_MODEL), OUT1 ** -0.5),
        'l1_router': nrm((D_MODEL, N_EXPERTS), D_MODEL ** -0.5),
        'l1_moe_gate': nrm((N_EXPERTS, D_MODEL, D_FF_E), D_MODEL ** -0.5),
        'l1_moe_up': nrm((N_EXPERTS, D_MODEL, D_FF_E), D_MODEL ** -0.5),
        'l1_moe_down': nrm((N_EXPERTS, D_FF_E, D_MODEL), D_FF_E ** -0.5),
    }


def reference(x_prompt, x_sample, c_prompt, c_sample, page_table,
              cache_a_k, cache_a_v, cache_b_lat, cache_b_krope,
              state_c_win0, state_c_win1, state_c_win2,
              ada_w, ada_b, norm_mix, norm_ffn,
              l0_w_in, l0_a_qnorm, l0_a_knorm, l0_a_lambda, l0_a_subln,
              l0_b_qa_norm, l0_b_w_uq, l0_b_kv_norm, l0_b_w_ukv, l0_b_qnorm, l0_b_knorm,
              l0_w_out, l0_ffn_gate, l0_ffn_up, l0_ffn_down,
              l1_w_in, l1_c_qnorm, l1_c_knorm, l1_w_out,
              l1_router, l1_moe_gate, l1_moe_up, l1_moe_down):
    w = dict(ada_w=ada_w, ada_b=ada_b, norm_mix=norm_mix, norm_ffn=norm_ffn,
             l0_w_in=l0_w_in, l0_a_qnorm=l0_a_qnorm, l0_a_knorm=l0_a_knorm,
             l0_b_qa_norm=l0_b_qa_norm, l0_b_w_uq=l0_b_w_uq, l0_b_kv_norm=l0_b_kv_norm,
             l0_b_qnorm=l0_b_qnorm, l0_w_out=l0_w_out,
             l0_ffn_gate=l0_ffn_gate, l0_ffn_up=l0_ffn_up, l0_ffn_down=l0_ffn_down,
             l1_w_in=l1_w_in, l1_c_qnorm=l1_c_qnorm, l1_c_knorm=l1_c_knorm, l1_w_out=l1_w_out,
             l1_router=l1_router, l1_moe_gate=l1_moe_gate, l1_moe_up=l1_moe_up,
             l1_moe_down=l1_moe_down)
    lam = diff_lambda(l0_a_lambda)
    seq = x_prompt.shape[1]
    dec_seq = x_sample.shape[1]
    past = page_table.shape[1] * PAGE_SIZE
    pos_p = jnp.arange(seq, dtype=jnp.int32)
    pos_s = past + jnp.arange(dec_seq, dtype=jnp.int32)
    k_pos_s = jnp.arange(past + dec_seq, dtype=jnp.int32)
    win_inputs = (state_c_win0, state_c_win1, state_c_win2)

    def prompt_attn0(qa, ka, va, qb, lat, krope):
        kb, vb = mla_kv(lat, krope, l0_b_w_ukv, l0_b_knorm)

        def block(args):
            qa_blk, qb_blk, qp = args
            oa = diff_attn(qa_blk, ka, va, qp, pos_p, lam, l0_a_subln)
            ob = softmax_attn(qb_blk, kb, vb, qp, pos_p)
            return jnp.concatenate([oa, ob], axis=-1)

        return from_blocks(lax.map(block, (to_blocks(qa), to_blocks(qb), pos_p.reshape(-1, Q_BLOCK))))

    def sample_attn0(qa, ka, va, qb, lat, krope):
        def one(args):
            pt, qa_i, ka_i, va_i, qb_i, lat_i, kr_i = args
            k_a = gather_past(cache_a_k, pt, ka_i)
            v_a = gather_past(cache_a_v, pt, va_i)
            oa = diff_attn(qa_i[None], k_a, v_a, pos_s, k_pos_s, lam, l0_a_subln)
            kb, vb = mla_kv(gather_past(cache_b_lat, pt, lat_i), gather_past(cache_b_krope, pt, kr_i),
                            l0_b_w_ukv, l0_b_knorm)
            ob = softmax_attn(qb_i[None], kb, vb, pos_s, k_pos_s)
            return jnp.concatenate([oa, ob], axis=-1)[0]

        return lax.map(one, (page_table, qa, ka, va, qb, lat, krope))

    def prompt_attn1(q, k, v):
        kbufs = [k[:, :, g] for g in range(N_DIL)]
        vbufs = [v[:, :, g] for g in range(N_DIL)]

        def block(args):
            q_blk, qp = args
            return dilated_mix(q_blk, kbufs, vbufs, (qp,) * N_DIL)

        o = from_blocks(lax.map(block, (to_blocks(q), pos_p.reshape(-1, Q_BLOCK))))
        states = [jnp.stack([k[:, -min(win, seq):, g], v[:, -min(win, seq):, g]], axis=2)
                  for g, (win, _) in enumerate(DIL_GROUPS)]
        return o, states

    def sample_attn1(q, k, v):
        bufs = [jnp.concatenate([st, jnp.stack([k[:, :, g], v[:, :, g]], axis=2)], axis=1)
                for g, st in enumerate(win_inputs)]
        q_idx = tuple(st.shape[1] + jnp.arange(dec_seq, dtype=jnp.int32) for st in win_inputs)

        def one(args):
            q_i, bufs_i = args
            kb = [bb[None, :, 0] for bb in bufs_i]
            vb = [bb[None, :, 1] for bb in bufs_i]
            return dilated_mix(q_i[None], kb, vb, q_idx)[0]

        o = lax.map(one, (q, bufs))
        states = [bb[:, -min(win, past + dec_seq):] for bb, (win, _) in zip(bufs, DIL_GROUPS)]
        return o, states

    y_prompt, st_p = run_group(x_prompt, c_prompt, pos_p, w, prompt_attn0, prompt_attn1)
    y_sample, st_s = run_group(x_sample, c_sample, pos_s, w, sample_attn0, sample_attn1)
    a_k_p, a_v_p, b_lat_p, b_kr_p, c_w0_p, c_w1_p, c_w2_p = st_p
    a_k_s, a_v_s, b_lat_s, b_kr_s, c_w0_s, c_w1_s, c_w2_s = st_s
    return (y_prompt, y_sample, a_k_p, a_k_s, a_v_p, a_v_s, b_lat_p, b_lat_s, b_kr_p, b_kr_s,
            c_w0_p, c_w0_s, c_w1_p, c_w1_s, c_w2_p, c_w2_s)
```

```python
import functools

import jax
import jax.numpy as jnp
from jax import lax
from jax.experimental import pallas as pl
from jax.experimental.pallas import tpu as pltpu

F32 = jnp.float32
BF16 = jnp.bfloat16
I32 = jnp.int32

D = 2048
DEPTH = 2
PAGE = 128
ROPE_THETA = 500000.0
EPS = 1e-6
NEG_INF = -1e30

H_A, KV_A, DH_A = 8, 2, 64
LAMBDA_INIT = 0.2
H_B, Q_LORA, KV_LORA, NOPE_B, ROPE_B, VH_B = 8, 512, 256, 64, 32, 128
QK_B = NOPE_B + ROPE_B
H_C, DH_C = 8, 128
DIL_GROUPS = ((128, 1), (512, 4), (2048, 16))
N_DIL = 3
D_FF = 5632
N_EXPERTS = 8
D_FF_E = 7168
OUT1 = H_C * DH_C

LANE = 128
SUB = 8
VMEM_BIG = 56 * 1024 * 1024

N_QS = 24
N_KS = 10
Z0_W = 27 * LANE
PAGES_PER_STEP = 16
MOE_TM = 256


def _cparams(sem, vmem=None):
    return pltpu.CompilerParams(dimension_semantics=sem, vmem_limit_bytes=vmem)


def _adaln_kernel(c_ref, w_ref, b_ref, o_ref):
    c = c_ref[...]
    a = (c * jax.nn.sigmoid(c)).astype(BF16)
    o_ref[0] = jnp.dot(a, w_ref[0].astype(BF16), preferred_element_type=F32) + b_ref[0]


def _adaln_all(c_all, ada_w, ada_b):
    nb = c_all.shape[0]
    tn = 1024
    return pl.pallas_call(
        _adaln_kernel,
        grid=(DEPTH, 6 * D // tn),
        in_specs=[pl.BlockSpec((nb, D), lambda l, j: (0, 0)),
                  pl.BlockSpec((1, D, tn), lambda l, j: (l, 0, j)),
                  pl.BlockSpec((1, 1, tn), lambda l, j: (l, 0, j))],
        out_specs=pl.BlockSpec((1, nb, tn), lambda l, j: (l, 0, j)),
        out_shape=jax.ShapeDtypeStruct((DEPTH, nb, 6 * D), F32),
        compiler_params=_cparams(("parallel", "parallel")),
        name="adaln",
    )(c_all, ada_w, ada_b.reshape(DEPTH, 1, 6 * D))


def _norm_mod_kernel(x_ref, g_ref, sc_ref, sh_ref, o_ref, *of_ref):
    x = x_ref[...]
    ms = jnp.mean(x * x, axis=-1, keepdims=True)
    y = (x * lax.rsqrt(ms + EPS)) * g_ref[...]
    y = y * (1.0 + sc_ref[...]) + sh_ref[...]
    y2 = y.reshape(o_ref.shape)
    o_ref[...] = y2.astype(BF16)
    if of_ref:
        of_ref[0][...] = y2


def _norm_mod(x, gain, mod, k_sc, k_sh, want_f32=False):
    t = x.shape[0]
    gb = 32
    tm = gb * SUB
    out_shape = [jax.ShapeDtypeStruct((t, D), BF16)]
    out_specs = [pl.BlockSpec((tm, D), lambda i: (i, 0))]
    if want_f32:
        out_shape.append(jax.ShapeDtypeStruct((t, D), F32))
        out_specs.append(pl.BlockSpec((tm, D), lambda i: (i, 0)))
    res = pl.pallas_call(
        _norm_mod_kernel,
        grid=(t // tm,),
        in_specs=[pl.BlockSpec((gb, SUB, D), lambda i: (i, 0, 0)),
                  pl.BlockSpec((1, 1, D), lambda i: (0, 0, 0)),
                  pl.BlockSpec((gb, 1, D), lambda i: (i, 0, k_sc)),
                  pl.BlockSpec((gb, 1, D), lambda i: (i, 0, k_sh))],
        out_specs=out_specs,
        out_shape=out_shape,
        compiler_params=_cparams(("parallel",)),
        name="norm_mod",
    )(x.reshape(t // SUB, SUB, D), gain.reshape(1, 1, D), mod, mod)
    return res if want_f32 else res[0]


def _mm_kernel(x_ref, w_ref, o_ref, wb_ref):
    @pl.when(pl.program_id(1) == 0)
    def _():
        wb_ref[...] = w_ref[...].astype(BF16)
    o_ref[...] = jnp.dot(x_ref[...], wb_ref[...], preferred_element_type=F32).astype(o_ref.dtype)


def _mm(x, w, tm=512, tn=512, out_dtype=F32):
    m, k = x.shape
    n = w.shape[1]
    return pl.pallas_call(
        _mm_kernel,
        grid=(n // tn, m // tm),
        in_specs=[pl.BlockSpec((tm, k), lambda j, i: (i, 0)),
                  pl.BlockSpec((k, tn), lambda j, i: (0, j))],
        out_specs=pl.BlockSpec((tm, tn), lambda j, i: (i, j)),
        out_shape=jax.ShapeDtypeStruct((m, n), out_dtype),
        scratch_shapes=[pltpu.VMEM((k, tn), BF16)],
        compiler_params=_cparams(("arbitrary", "arbitrary"), VMEM_BIG),
        name="mm",
    )(x, w)


def _mm_res_kernel(x_ref, w_ref, r_ref, g_ref, o_ref, wb_ref):
    @pl.when(pl.program_id(1) == 0)
    def _():
        wb_ref[...] = w_ref[...].astype(BF16)
    acc = jnp.dot(x_ref[...], wb_ref[...], preferred_element_type=F32)
    o_ref[...] = r_ref[...] + g_ref[...] * acc.reshape(o_ref.shape)


def _mm_res(x, w, res, mod, k_gate, tm=512, tn=512):
    m, k = x.shape
    n = w.shape[1]
    gb = tm // SUB
    nj = n // tn
    out = pl.pallas_call(
        _mm_res_kernel,
        grid=(nj, m // tm),
        in_specs=[pl.BlockSpec((tm, k), lambda j, i: (i, 0)),
                  pl.BlockSpec((k, tn), lambda j, i: (0, j)),
                  pl.BlockSpec((gb, SUB, tn), lambda j, i: (i, 0, j)),
                  pl.BlockSpec((gb, 1, tn), lambda j, i: (i, 0, k_gate * nj + j))],
        out_specs=pl.BlockSpec((gb, SUB, tn), lambda j, i: (i, 0, j)),
        out_shape=jax.ShapeDtypeStruct((m // SUB, SUB, n), F32),
        scratch_shapes=[pltpu.VMEM((k, tn), BF16)],
        compiler_params=_cparams(("arbitrary", "arbitrary"), VMEM_BIG),
        name="mm_res",
    )(x, w, res.reshape(m // SUB, SUB, n), mod)
    return out.reshape(m, n)


def _mm_swiglu_kernel(x_ref, wg_ref, wu_ref, o_ref, wgb_ref, wub_ref):
    @pl.when(pl.program_id(1) == 0)
    def _():
        wgb_ref[...] = wg_ref[...].astype(BF16)
        wub_ref[...] = wu_ref[...].astype(BF16)
    x = x_ref[...]
    a = jnp.dot(x, wgb_ref[...], preferred_element_type=F32)
    b = jnp.dot(x, wub_ref[...], preferred_element_type=F32)
    o_ref[...] = ((a * jax.nn.sigmoid(a)) * b).astype(BF16)


def _mm_swiglu(x, wg, wu, tm=512, tn=512):
    m, k = x.shape
    n = wg.shape[1]
    return pl.pallas_call(
        _mm_swiglu_kernel,
        grid=(n // tn, m // tm),
        in_specs=[pl.BlockSpec((tm, k), lambda j, i: (i, 0)),
                  pl.BlockSpec((k, tn), lambda j, i: (0, j)),
                  pl.BlockSpec((k, tn), lambda j, i: (0, j))],
        out_specs=pl.BlockSpec((tm, tn), lambda j, i: (i, j)),
        out_shape=jax.ShapeDtypeStruct((m, n), BF16),
        scratch_shapes=[pltpu.VMEM((k, tn), BF16), pltpu.VMEM((k, tn), BF16)],
        compiler_params=_cparams(("arbitrary", "arbitrary"), VMEM_BIG),
        name="mm_swiglu",
    )(x, wg, wu)


def _rope_tables(pos, half, offset, period):
    inv_freq = ROPE_THETA ** (-jnp.arange(half, dtype=F32) / half)
    ang = pos.astype(F32)[:, None] * inv_freq[None, :]
    cos, sin = jnp.cos(ang), jnp.sin(ang)
    n = pos.shape[0]
    seg_c = jnp.ones((n, period), F32)
    seg_c = seg_c.at[:, offset:offset + half].set(cos).at[:, offset + half:offset + 2 * half].set(cos)
    seg_m = jnp.zeros((n, period), F32).at[:, offset:offset + half].set(-sin)
    seg_p = jnp.zeros((n, period), F32).at[:, offset + half:offset + 2 * half].set(sin)
    rep = LANE // period
    return jnp.concatenate([jnp.tile(seg_c, (1, rep)), jnp.tile(seg_m, (1, rep)), jnp.tile(seg_p, (1, rep))], axis=1)


def _rope(x, tab, half):
    c = tab[:, 0:LANE]
    sm = tab[:, LANE:2 * LANE]
    sp = tab[:, 2 * LANE:3 * LANE]
    return x * c + pltpu.roll(x, LANE - half, 1) * sm + pltpu.roll(x, half, 1) * sp


def _rms(x, n_valid):
    return lax.rsqrt(jnp.sum(x * x, axis=-1, keepdims=True) * (1.0 / n_valid) + EPS)


def _proj0_post_kernel(z_ref, ta_ref, tb_ref, tk_ref, gq_ref, gk_ref, gqa_ref, gkv_ref, gqb_ref, gkb_ref,
                       wuq_ref, wuk_ref, wuv_ref, *out_refs, prompt):
    if prompt:
        q_ref, ka_ref, va_ref, lat_ref, kr_ref, ks_ref, vs_ref = out_refs
    else:
        q_ref, ka_ref, va_ref, lat_ref, kr_ref = out_refs
    ta = ta_ref[...]
    tb = tb_ref[...]
    tk = tk_ref[...]
    lane = lax.broadcasted_iota(I32, (1, LANE), 1)
    lo = lane < DH_A

    def put_q(s, val):
        if prompt:
            q_ref[0, s] = val.astype(BF16)
        else:
            q_ref[:, s * LANE:(s + 1) * LANE] = val

    for s in range(2 * H_A):
        x = z_ref[:, s * LANE:(s + 1) * LANE]
        y = (x * _rms(x, DH_A)) * gq_ref[...]
        put_q(s, _rope(y, ta, DH_A // 8) * (DH_A ** -0.5))
    for kv in range(KV_A):
        x = z_ref[:, (16 + kv) * LANE:(17 + kv) * LANE]
        xx = x * x
        s_lo = jnp.sum(jnp.where(lo, xx, 0.0), axis=-1, keepdims=True)
        s_hi = jnp.sum(jnp.where(lo, 0.0, xx), axis=-1, keepdims=True)
        inv = jnp.where(lo, lax.rsqrt(s_lo * (1.0 / DH_A) + EPS), lax.rsqrt(s_hi * (1.0 / DH_A) + EPS))
        k = _rope((x * inv) * gk_ref[...], ta, DH_A // 8)
        v = z_ref[:, (18 + kv) * LANE:(19 + kv) * LANE]
        ka_ref[:, kv * LANE:(kv + 1) * LANE] = k
        va_ref[:, kv * LANE:(kv + 1) * LANE] = v
        if prompt:
            ks_ref[0, kv] = k.astype(BF16)
            vs_ref[0, kv] = v.astype(BF16)
    qc = z_ref[:, 20 * LANE:24 * LANE]
    qcn = (qc * _rms(qc, Q_LORA)) * gqa_ref[...]
    qb = jnp.dot(qcn.astype(BF16), wuq_ref[...], preferred_element_type=F32)
    for h in range(H_B):
        x = _rope(qb[:, h * LANE:(h + 1) * LANE], tb, ROPE_B // 2)
        y = (x * _rms(x, QK_B)) * gqb_ref[...]
        put_q(2 * H_A + h, y * (QK_B ** -0.5))
    kvc = z_ref[:, 24 * LANE:26 * LANE]
    lat = (kvc * _rms(kvc, KV_LORA)) * gkv_ref[...]
    lat_ref[...] = lat
    kr = _rope(z_ref[:, 26 * LANE:27 * LANE], tk, ROPE_B // 2)
    kr_ref[...] = kr[:, 0:ROPE_B]
    if prompt:
        latb = lat.astype(BF16)
        kn = jnp.dot(latb, wuk_ref[...], preferred_element_type=F32)
        vv = jnp.dot(latb, wuv_ref[...], preferred_element_type=F32)
        kr_hi = pltpu.roll(kr, NOPE_B, 1)
        for h in range(H_B):
            x = kn[:, h * LANE:(h + 1) * LANE] + kr_hi
            ks_ref[0, KV_A + h] = ((x * _rms(x, QK_B)) * gkb_ref[...]).astype(BF16)
            vs_ref[0, KV_A + h] = vv[:, h * LANE:(h + 1) * LANE].astype(BF16)


def _proj0_post(z0, tabs, gains, wts, *, prompt, nb, seq, row0):
    ta, tb, tk = tabs
    tm = 256
    n_rows = nb * seq
    nt = n_rows // tm
    r0 = row0 // tm
    if prompt:
        per_b = seq // tm
        rows = lambda i: (i, 0)
        grid = (nt,)
        zmap = lambda i: (r0 + i, 0)
        q_spec = pl.BlockSpec((1, N_QS, tm, LANE), lambda i: (i // per_b, 0, i % per_b, 0))
        kv_spec = pl.BlockSpec((1, N_KS, tm, LANE), lambda i: (i // per_b, 0, i % per_b, 0))
        out_shape = [jax.ShapeDtypeStruct((nb, N_QS, seq, LANE), BF16)]
        out_specs = [q_spec]
    else:
        rows = lambda i: (i, 0)
        grid = (nt,)
        zmap = lambda i: (r0 + i, 0)
        out_shape = [jax.ShapeDtypeStruct((n_rows, N_QS * LANE), F32)]
        out_specs = [pl.BlockSpec((tm, N_QS * LANE), rows)]
    out_shape += [jax.ShapeDtypeStruct((n_rows, 2 * LANE), F32)] * 3 + [jax.ShapeDtypeStruct((n_rows, ROPE_B), F32)]
    out_specs += [pl.BlockSpec((tm, 2 * LANE), rows)] * 3 + [pl.BlockSpec((tm, ROPE_B), rows)]
    if prompt:
        out_shape += [jax.ShapeDtypeStruct((nb, N_KS, seq, LANE), BF16)] * 2
        out_specs += [kv_spec, kv_spec]
    tab_spec = pl.BlockSpec((tm, 3 * LANE), zmap)
    const = lambda a: pl.BlockSpec(a.shape, lambda i: (0,) * a.ndim)
    return pl.pallas_call(
        functools.partial(_proj0_post_kernel, prompt=prompt),
        grid=grid,
        in_specs=[pl.BlockSpec((tm, Z0_W), zmap), tab_spec, tab_spec, tab_spec]
                 + [const(a) for a in gains] + [const(a) for a in wts],
        out_specs=out_specs,
        out_shape=out_shape,
        compiler_params=_cparams(("parallel",)),
        name="proj0_post_p" if prompt else "proj0_post_s",
    )(z0, ta, tb, tk, *gains, *wts)


def _flash_kernel(q_ref, k_ref, v_ref, o_ref, m_ref, l_ref, acc_ref, *, tq):
    qi = pl.program_id(2)
    q = q_ref[0, 0]
    m_ref[...] = jnp.full(m_ref.shape, NEG_INF, F32)
    l_ref[...] = jnp.zeros(l_ref.shape, F32)
    acc_ref[...] = jnp.zeros(acc_ref.shape, F32)

    def step(ki, masked):
        start = pl.multiple_of(ki * tq, tq)
        k = k_ref[0, 0, pl.ds(start, tq), :]
        v = v_ref[0, 0, pl.ds(start, tq), :]
        s = lax.dot_general(q, k, (((1,), (1,)), ((), ())), preferred_element_type=F32)
        if masked:
            r = lax.broadcasted_iota(I32, (tq, tq), 0)
            c = lax.broadcasted_iota(I32, (tq, tq), 1)
            s = jnp.where(c <= r, s, NEG_INF)
        m_prev = m_ref[...]
        m_new = jnp.maximum(m_prev, jnp.max(s, axis=1, keepdims=True))
        alpha = jnp.exp(m_prev - m_new)
        p = jnp.exp(s - m_new)
        l_ref[...] = alpha * l_ref[...] + jnp.sum(p, axis=1, keepdims=True)
        acc_ref[...] = alpha * acc_ref[...] + jnp.dot(p.astype(BF16), v, preferred_element_type=F32)
        m_ref[...] = m_new

    def body(ki, carry):
        step(ki, False)
        return carry

    lax.fori_loop(0, qi, body, 0)
    step(qi, True)
    o_ref[0, 0] = acc_ref[...] / l_ref[...]


def _flash(q, k, v, tq=512):
    nb, _, seq, _ = q.shape
    kv_map = lambda b, s, i: (b, jnp.where(s < 2 * H_A, s // (2 * H_A // KV_A), s - (2 * H_A - KV_A)), 0, 0)
    return pl.pallas_call(
        functools.partial(_flash_kernel, tq=tq),
        grid=(nb, N_QS, seq // tq),
        in_specs=[pl.BlockSpec((1, 1, tq, LANE), lambda b, s, i: (b, s, i, 0)),
                  pl.BlockSpec((1, 1, seq, LANE), kv_map),
                  pl.BlockSpec((1, 1, seq, LANE), kv_map)],
        out_specs=pl.BlockSpec((1, 1, tq, LANE), lambda b, s, i: (b, s, i, 0)),
        out_shape=jax.ShapeDtypeStruct((nb, N_QS, seq, LANE), F32),
        scratch_shapes=[pltpu.VMEM((tq, 1), F32), pltpu.VMEM((tq, 1), F32), pltpu.VMEM((tq, LANE), F32)],
        compiler_params=_cparams(("parallel", "parallel", "arbitrary")),
        name="flash0",
    )(q, k, v)


def _diff_lambda(lam_ref):
    lf = lam_ref[...]
    a = jnp.sum(lf[0:1] * lf[1:2], axis=-1, keepdims=True)
    b = jnp.sum(lf[2:3] * lf[3:4], axis=-1, keepdims=True)
    return jnp.exp(a) - jnp.exp(b) + LAMBDA_INIT


def _attn0_post_kernel(o_ref, lam_ref, sub_ref, a_ref):
    lam = _diff_lambda(lam_ref)
    for h in range(H_A):
        d = o_ref[0, 2 * h] - lam * o_ref[0, 2 * h + 1]
        y = ((d * _rms(d, 2 * DH_A)) * sub_ref[...]) * (1.0 - LAMBDA_INIT)
        a_ref[:, h * LANE:(h + 1) * LANE] = y.astype(BF16)
    for h in range(H_B):
        a_ref[:, (H_A + h) * LANE:(H_A + h + 1) * LANE] = o_ref[0, 2 * H_A + h].astype(BF16)


def _attn0_post(o, lam_p, subln):
    nb, _, seq, _ = o.shape
    tm = 256
    per_b = seq // tm
    return pl.pallas_call(
        _attn0_post_kernel,
        grid=(nb * per_b,),
        in_specs=[pl.BlockSpec((1, N_QS, tm, LANE), lambda i: (i // per_b, 0, i % per_b, 0)),
                  pl.BlockSpec((4, DH_A), lambda i: (0, 0)),
                  pl.BlockSpec((1, LANE), lambda i: (0, 0))],
        out_specs=pl.BlockSpec((tm, D), lambda i: (i, 0)),
        out_shape=jax.ShapeDtypeStruct((nb * seq, D), BF16),
        compiler_params=_cparams(("parallel",)),
        name="attn0_post",
    )(o, lam_p, subln)


def _decode0_kernel(pt_ref, q_ref, kan_ref, van_ref, latn_ref, krn_ref, wukt_ref, wuk_ref, wuv_ref, gkb_ref,
                    lam_ref, sub_ref, *rest, n_chunks):
    pps = PAGES_PER_STEP
    ck = rest[0:pps]
    cv = rest[pps:2 * pps]
    cl = rest[2 * pps:3 * pps]
    cr = rest[3 * pps:4 * pps]
    o_ref = rest[4 * pps]
    ka_s, va_s, lx_s, krs_s, qa_s, qx_s, m_s, l_s, acca_s, accb_s = rest[4 * pps + 1:]
    c = pl.program_id(1)
    n_rows = 2 * H_A * SUB
    n_rows_b = H_B * SUB

    @pl.when(c == 0)
    def _():
        m_s[...] = jnp.full(m_s.shape, NEG_INF, F32)
        l_s[...] = jnp.zeros(l_s.shape, F32)
        acca_s[...] = jnp.zeros(acca_s.shape, F32)
        accb_s[...] = jnp.zeros(accb_s.shape, F32)
        lx_s[:, 2 * LANE:3 * LANE] = jnp.zeros((lx_s.shape[0], LANE), BF16)
        krs_s[...] = jnp.zeros(krs_s.shape, F32)
        lane = lax.broadcasted_iota(I32, (SUB, LANE), 1)
        for s in range(2 * H_A):
            qa_s[s * SUB:(s + 1) * SUB, :] = q_ref[:, s * LANE:(s + 1) * LANE]
        for h in range(H_B):
            qk = q_ref[:, (2 * H_A + h) * LANE:(2 * H_A + h + 1) * LANE] * gkb_ref[...]
            qx_s[h * SUB:(h + 1) * SUB, 0:2 * LANE] = jnp.dot(qk, wukt_ref[h], preferred_element_type=F32)
            qx_s[h * SUB:(h + 1) * SUB, 2 * LANE:3 * LANE] = jnp.where(lane < ROPE_B, pltpu.roll(qk, NOPE_B, 1), 0.0)

    def online(rows, s, pv):
        m_prev = m_s[rows, :]
        m_new = jnp.maximum(m_prev, jnp.max(s, axis=1, keepdims=True))
        alpha = jnp.exp(m_prev - m_new)
        p = jnp.exp(s - m_new)
        l_s[rows, :] = alpha * l_s[rows, :] + jnp.sum(p, axis=1, keepdims=True)
        m_s[rows, :] = m_new
        return alpha, pv(p.astype(BF16))

    def process(n, causal):
        nt = (((1,), (1,)), ((), ()))
        if causal:
            row = lax.broadcasted_iota(I32, (n_rows_b, n), 0)
            col = lax.broadcasted_iota(I32, (n_rows_b, n), 1)
            keep = col <= (row & (SUB - 1))
        qa = qa_s[...].astype(BF16)
        for kv in range(KV_A):
            rows = pl.ds(kv * n_rows_b, n_rows_b)
            kk = ka_s[0:n, kv * LANE:(kv + 1) * LANE]
            vv = va_s[0:n, kv * LANE:(kv + 1) * LANE]
            s = lax.dot_general(qa[kv * n_rows_b:(kv + 1) * n_rows_b], kk, nt, preferred_element_type=F32)
            if causal:
                s = jnp.where(keep, s, NEG_INF)
            alpha, pv = online(rows, s, lambda p: jnp.dot(p, vv, preferred_element_type=F32))
            acca_s[rows, :] = alpha * acca_s[rows, :] + pv
        lx = lx_s[0:n, :]
        latb = lx[:, 0:2 * LANE]
        s_raw = lax.dot_general(qx_s[...].astype(BF16), lx, nt, preferred_element_type=F32)
        kn = jnp.dot(latb, wuk_ref[...], preferred_element_type=F32)
        sq = kn * kn
        sq_hi = sq.astype(BF16)
        sq_lo = (sq - sq_hi.astype(F32)).astype(BF16)
        seg = (lax.broadcasted_iota(I32, (H_B, H_B * NOPE_B), 1) // NOPE_B
               == lax.broadcasted_iota(I32, (H_B, H_B * NOPE_B), 0)).astype(BF16)
        n2 = (lax.dot_general(seg, sq_hi, nt, preferred_element_type=F32)
              + lax.dot_general(seg, sq_lo, nt, preferred_element_type=F32))
        kq = krs_s[0:n, :]
        kq_hi = kq.astype(BF16)
        kq_lo = (kq - kq_hi.astype(F32)).astype(BF16)
        ones = jnp.ones((H_B, LANE), BF16)
        n2 = n2 + (lax.dot_general(ones, kq_hi, nt, preferred_element_type=F32)
                   + lax.dot_general(ones, kq_lo, nt, preferred_element_type=F32))
        rinv = lax.rsqrt(n2 * (1.0 / QK_B) + EPS)
        s = (s_raw.reshape(H_B, SUB, n) * rinv[:, None, :]).reshape(n_rows_b, n)
        if causal:
            s = jnp.where(keep, s, NEG_INF)
        rows = pl.ds(n_rows, n_rows_b)
        alpha, pv = online(rows, s, lambda p: jnp.dot(p, latb, preferred_element_type=F32))
        accb_s[...] = alpha * accb_s[...] + pv

    for i in range(pps):
        r = pl.ds(i * PAGE, PAGE)
        ka_s[r, :] = ck[i][0].astype(BF16)
        va_s[r, :] = cv[i][0].astype(BF16)
        lx_s[r, 0:2 * LANE] = cl[i][0].astype(BF16)
        kr = cr[i][0]
        lx_s[r, 2 * LANE:2 * LANE + ROPE_B] = kr.astype(BF16)
        krs_s[r, 0:ROPE_B] = kr * kr
    process(pps * PAGE, False)

    @pl.when(c == n_chunks - 1)
    def _():
        pad = lambda x: jnp.concatenate([x, jnp.zeros((PAGE - SUB, x.shape[1]), F32)], axis=0)
        r = pl.ds(0, PAGE)
        ka_s[r, :] = pad(kan_ref[...]).astype(BF16)
        va_s[r, :] = pad(van_ref[...]).astype(BF16)
        lx_s[r, 0:2 * LANE] = pad(latn_ref[...]).astype(BF16)
        kr = pad(krn_ref[...])
        lx_s[r, 2 * LANE:2 * LANE + ROPE_B] = kr.astype(BF16)
        krs_s[r, 0:ROPE_B] = kr * kr
        process(PAGE, True)
        lam = _diff_lambda(lam_ref)
        oa = acca_s[...] / l_s[0:n_rows, :]
        for h in range(H_A):
            d = oa[2 * h * SUB:(2 * h + 1) * SUB] - lam * oa[(2 * h + 1) * SUB:(2 * h + 2) * SUB]
            o_ref[:, h * LANE:(h + 1) * LANE] = ((d * _rms(d, 2 * DH_A)) * sub_ref[...]) * (1.0 - LAMBDA_INIT)
        ob = accb_s[...] / l_s[n_rows:n_rows + n_rows_b, :]
        for h in range(H_B):
            o_ref[:, (H_A + h) * LANE:(H_A + h + 1) * LANE] = jnp.dot(
                ob[h * SUB:(h + 1) * SUB], wuv_ref[:, h * LANE:(h + 1) * LANE], preferred_element_type=F32)


def _decode0(page_table, q_s, ka_s, va_s, lat_s, kr_s, caches, wts, gkb, lam_p, subln):
    nb, n_pages = page_table.shape
    pps = PAGES_PER_STEP
    n_chunks = n_pages // pps
    n = pps * PAGE
    ck, cv, cl, cr = caches
    wukt, wuk, wuv = wts
    rowmap = lambda b, c, pt: (b, 0)
    const = lambda a: pl.BlockSpec(a.shape, lambda b, c, pt: (0,) * a.ndim)

    def page_specs(width):
        return [pl.BlockSpec((1, PAGE, width), functools.partial(lambda b, c, pt, i: (pt[b, c * pps + i], 0, 0), i=i))
                for i in range(pps)]

    in_specs = ([pl.BlockSpec((SUB, N_QS * LANE), rowmap), pl.BlockSpec((SUB, 2 * LANE), rowmap),
                 pl.BlockSpec((SUB, 2 * LANE), rowmap), pl.BlockSpec((SUB, 2 * LANE), rowmap),
                 pl.BlockSpec((SUB, ROPE_B), rowmap), const(wukt), const(wuk), const(wuv), const(gkb),
                 const(lam_p), const(subln)]
                + page_specs(2 * LANE) + page_specs(2 * LANE) + page_specs(2 * LANE) + page_specs(ROPE_B))
    grid_spec = pltpu.PrefetchScalarGridSpec(
        num_scalar_prefetch=1,
        grid=(nb, n_chunks),
        in_specs=in_specs,
        out_specs=pl.BlockSpec((SUB, D), rowmap),
        scratch_shapes=[pltpu.VMEM((n, 2 * LANE), BF16), pltpu.VMEM((n, 2 * LANE), BF16),
                        pltpu.VMEM((n, 3 * LANE), BF16), pltpu.VMEM((n, LANE), F32),
                        pltpu.VMEM((2 * H_A * SUB, LANE), F32), pltpu.VMEM((H_B * SUB, 3 * LANE), F32),
                        pltpu.VMEM(((2 * H_A + H_B) * SUB, 1), F32), pltpu.VMEM(((2 * H_A + H_B) * SUB, 1), F32),
                        pltpu.VMEM((2 * H_A * SUB, LANE), F32), pltpu.VMEM((H_B * SUB, 2 * LANE), F32)])
    return pl.pallas_call(
        functools.partial(_decode0_kernel, n_chunks=n_chunks),
        grid_spec=grid_spec,
        out_shape=jax.ShapeDtypeStruct((nb * SUB, D), F32),
        compiler_params=_cparams(("parallel", "arbitrary"), VMEM_BIG),
        name="decode0",
    )(page_table, q_s, ka_s, va_s, lat_s, kr_s, wukt, wuk, wuv, gkb, lam_p, subln,
      *([ck] * pps), *([cv] * pps), *([cl] * pps), *([cr] * pps))


def _proj1_post_kernel(z_ref, tk_ref, gq_ref, gk_ref, q_ref, k_ref, kf_ref, v_ref):
    j = pl.program_id(1)
    tk = tk_ref[...]
    nh = N_DIL * H_C

    @pl.when(j == 0)
    def _():
        for h in range(nh):
            x = z_ref[:, h * LANE:(h + 1) * LANE]
            y = _rope((x * _rms(x, DH_C)) * gq_ref[...], tk, DH_C // 8)
            q_ref[:, h * LANE:(h + 1) * LANE] = (y * (DH_C ** -0.5)).astype(BF16)

    @pl.when(j == 1)
    def _():
        for h in range(nh):
            x = z_ref[:, h * LANE:(h + 1) * LANE]
            y = _rope((x * _rms(x, DH_C)) * gk_ref[...], tk, DH_C // 8)
            kf_ref[:, h * LANE:(h + 1) * LANE] = y
            k_ref[:, h * LANE:(h + 1) * LANE] = y.astype(BF16)

    @pl.when(j == 2)
    def _():
        v_ref[...] = z_ref[...].astype(BF16)


def _proj1_post(z1, tk, gq, gk, *, n_rows, row0):
    tm = 256
    w = N_DIL * H_C * DH_C
    r0 = row0 // tm
    rows = lambda i, j: (i, 0)
    return pl.pallas_call(
        _proj1_post_kernel,
        grid=(n_rows // tm, 3),
        in_specs=[pl.BlockSpec((tm, w), lambda i, j: (r0 + i, j)),
                  pl.BlockSpec((tm, 3 * LANE), lambda i, j: (r0 + i, 0)),
                  pl.BlockSpec((1, LANE), lambda i, j: (0, 0)),
                  pl.BlockSpec((1, LANE), lambda i, j: (0, 0))],
        out_specs=[pl.BlockSpec((tm, w), rows)] * 4,
        out_shape=[jax.ShapeDtypeStruct((n_rows, w), BF16), jax.ShapeDtypeStruct((n_rows, w), BF16),
                   jax.ShapeDtypeStruct((n_rows, w), F32), jax.ShapeDtypeStruct((n_rows, w), BF16)],
        compiler_params=_cparams(("parallel", "arbitrary")),
        name="proj1_post",
    )(z1, tk, gq, gk)


def _dil_prompt_kernel(q_ref, kc_ref, kp_ref, vc_ref, vp_ref, o_ref, lse_ref, *, tq):
    qi = pl.program_id(2)
    nt = (((1,), (1,)), ((), ()))
    r = lax.broadcasted_iota(I32, (tq, tq), 0)
    c = lax.broadcasted_iota(I32, (tq, tq), 1)
    keep_c = c <= r
    keep_p = jnp.logical_and(c >= r, qi > 0)
    lane = lax.broadcasted_iota(I32, (tq, LANE), 1)
    lse_tile = jnp.zeros((tq, LANE), F32)
    for h in range(H_C):
        sl = slice(h * LANE, (h + 1) * LANE)
        q = q_ref[0, :, sl]
        s_c = jnp.where(keep_c, lax.dot_general(q, kc_ref[0, :, sl], nt, preferred_element_type=F32), NEG_INF)
        s_p = jnp.where(keep_p, lax.dot_general(q, kp_ref[0, :, sl], nt, preferred_element_type=F32), NEG_INF)
        m = jnp.maximum(jnp.max(s_c, axis=1, keepdims=True), jnp.max(s_p, axis=1, keepdims=True))
        p_c = jnp.exp(s_c - m)
        p_p = jnp.exp(s_p - m)
        l = jnp.sum(p_c, axis=1, keepdims=True) + jnp.sum(p_p, axis=1, keepdims=True)
        o = (jnp.dot(p_c.astype(BF16), vc_ref[0, :, sl], preferred_element_type=F32)
             + jnp.dot(p_p.astype(BF16), vp_ref[0, :, sl], preferred_element_type=F32))
        o_ref[0, :, sl] = (o / l).astype(o_ref.dtype)
        lse_tile = jnp.where(lane == h, m + jnp.log(l), lse_tile)
    lse_ref[0] = lse_tile


def _dil_prompt(q, k, v, g, dil, nb, seq):
    tq = PAGE
    w = H_C * DH_C
    ns = seq // dil
    view = lambda a: a.reshape(nb, ns, dil * a.shape[1])
    cur = lambda b, r, i: (b, i, r * N_DIL + g)
    prev = lambda b, r, i: (b, jnp.maximum(i - 1, 0), r * N_DIL + g)
    blk = lambda m: pl.BlockSpec((1, tq, w), m)
    o, lse = pl.pallas_call(
        functools.partial(_dil_prompt_kernel, tq=tq),
        grid=(nb, dil, ns // tq),
        in_specs=[blk(cur), blk(cur), blk(prev), blk(cur), blk(prev)],
        out_specs=[pl.BlockSpec((1, tq, w), lambda b, r, i: (b, i, r)),
                   pl.BlockSpec((1, tq, LANE), lambda b, r, i: (b, i, r))],
        out_shape=[jax.ShapeDtypeStruct((nb, ns, dil * w), BF16), jax.ShapeDtypeStruct((nb, ns, dil * LANE), F32)],
        compiler_params=_cparams(("parallel", "parallel", "parallel")),
        name=f"dil_prompt{g}",
    )(view(q), view(k), view(k), view(v), view(v))
    return o.reshape(nb * seq, w), lse.reshape(nb * seq, LANE)


def _dil_sample_kernel(q_ref, kn_ref, vn_ref, st_ref, o_ref, lse_ref, m_s, l_s, acc_s, *, g, dil, ch, n_chunks):
    c = pl.program_id(1)
    nt = (((1,), (1,)), ((), ()))
    w = H_C * DH_C

    @pl.when(c == 0)
    def _():
        m_s[...] = jnp.full(m_s.shape, NEG_INF, F32)
        l_s[...] = jnp.zeros(l_s.shape, F32)
        acc_s[...] = jnp.zeros(acc_s.shape, F32)

    def update(h, s, v):
        rows = pl.ds(h * SUB, SUB)
        m_prev = m_s[rows, :]
        m_new = jnp.maximum(m_prev, jnp.max(s, axis=1, keepdims=True))
        alpha = jnp.exp(m_prev - m_new)
        p = jnp.exp(s - m_new)
        l_s[rows, :] = alpha * l_s[rows, :] + jnp.sum(p, axis=1, keepdims=True)
        acc_s[rows, :] = alpha * acc_s[rows, :] + jnp.dot(p, v, preferred_element_type=F32)
        m_s[rows, :] = m_new

    t = lax.broadcasted_iota(I32, (SUB, ch), 0)
    i = lax.broadcasted_iota(I32, (SUB, ch), 1) + c * ch
    keep = jnp.logical_and(i >= t, ((i - t) & (dil - 1)) == 0)
    for h in range(H_C):
        q = q_ref[:, g * w + h * LANE:g * w + (h + 1) * LANE]
        kk = st_ref[0, :, h * LANE:(h + 1) * LANE]
        vv = st_ref[0, :, w + h * LANE:w + (h + 1) * LANE]
        s = lax.dot_general(q, kk, nt, preferred_element_type=F32)
        update(h, jnp.where(keep, s, NEG_INF), vv)

    @pl.when(c == n_chunks - 1)
    def _():
        tt = lax.broadcasted_iota(I32, (SUB, PAGE), 0)
        tn = lax.broadcasted_iota(I32, (SUB, PAGE), 1)
        keep_n = jnp.logical_and(tn <= tt, ((tt - tn) & (dil - 1)) == 0)
        pad = lambda x: jnp.concatenate([x, jnp.zeros((PAGE - SUB, LANE), F32)], axis=0)
        lane = lax.broadcasted_iota(I32, (SUB, LANE), 1)
        lse_tile = jnp.zeros((SUB, LANE), F32)
        for h in range(H_C):
            sl = slice(g * w + h * LANE, g * w + (h + 1) * LANE)
            q = q_ref[:, sl]
            s = lax.dot_general(q, pad(kn_ref[:, sl]), nt, preferred_element_type=F32)
            update(h, jnp.where(keep_n, s, NEG_INF), pad(vn_ref[:, sl]))
            rows = pl.ds(h * SUB, SUB)
            l = l_s[rows, :]
            o_ref[:, h * LANE:(h + 1) * LANE] = acc_s[rows, :] / l
            lse_tile = jnp.where(lane == h, m_s[rows, :] + jnp.log(l), lse_tile)
        lse_ref[...] = lse_tile


def _dil_sample(q_s, kf_s, z1, state, g, dil, row0):
    nb, win = state.shape[0], state.shape[1]
    w = H_C * DH_C
    ch = min(win, 512)
    n_chunks = win // ch
    wq = N_DIL * w
    r0 = row0 // SUB
    return pl.pallas_call(
        functools.partial(_dil_sample_kernel, g=g, dil=dil, ch=ch, n_chunks=n_chunks),
        grid=(nb, n_chunks),
        in_specs=[pl.BlockSpec((SUB, wq), lambda b, c: (b, 0)),
                  pl.BlockSpec((SUB, wq), lambda b, c: (b, 0)),
                  pl.BlockSpec((SUB, wq), lambda b, c: (r0 + b, 2)),
                  pl.BlockSpec((1, ch, 2 * w), lambda b, c: (b, c, 0))],
        out_specs=[pl.BlockSpec((SUB, w), lambda b, c: (b, 0)), pl.BlockSpec((SUB, LANE), lambda b, c: (b, 0))],
        out_shape=[jax.ShapeDtypeStruct((nb * SUB, w), F32), jax.ShapeDtypeStruct((nb * SUB, LANE), F32)],
        scratch_shapes=[pltpu.VMEM((H_C * SUB, 1), F32), pltpu.VMEM((H_C * SUB, 1), F32),
                        pltpu.VMEM((H_C * SUB, LANE), F32)],
        compiler_params=_cparams(("parallel", "arbitrary")),
        name=f"dil_sample{g}",
    )(q_s, kf_s, z1, state.reshape(nb, win, 2 * w))


def _dil_combine_kernel(o0_ref, o1_ref, o2_ref, l0_ref, l1_ref, l2_ref, a_ref):
    l0, l1, l2 = l0_ref[...], l1_ref[...], l2_ref[...]
    m = jnp.maximum(jnp.maximum(l0, l1), l2)
    w0, w1, w2 = jnp.exp(l0 - m), jnp.exp(l1 - m), jnp.exp(l2 - m)
    den = w0 + w1 + w2
    w0, w1, w2 = w0 / den, w1 / den, w2 / den
    for h in range(H_C):
        sl = slice(h * LANE, (h + 1) * LANE)
        a_ref[:, sl] = (w0[:, h:h + 1] * o0_ref[:, sl].astype(F32) + w1[:, h:h + 1] * o1_ref[:, sl].astype(F32)
                        + w2[:, h:h + 1] * o2_ref[:, sl].astype(F32)).astype(a_ref.dtype)


def _dil_combine(outs, lses, out_dtype, tm):
    n = outs[0].shape[0]
    w = H_C * DH_C
    rows = lambda i: (i, 0)
    return pl.pallas_call(
        _dil_combine_kernel,
        grid=(n // tm,),
        in_specs=[pl.BlockSpec((tm, w), rows)] * 3 + [pl.BlockSpec((tm, LANE), rows)] * 3,
        out_specs=pl.BlockSpec((tm, w), rows),
        out_shape=jax.ShapeDtypeStruct((n, w), out_dtype),
        compiler_params=_cparams(("parallel",)),
        name="dil_combine",
    )(*outs, *lses)


def _win_shift_kernel(kn_ref, vn_ref, s0_ref, s1_ref, s2_ref, o0_ref, o1_ref, o2_ref, new_s, sem_c, sem_n):
    b = pl.program_id(0)
    nb = pl.num_programs(0)
    w = H_C * DH_C
    states = (s0_ref, s1_ref, s2_ref)
    outs = (o0_ref, o1_ref, o2_ref)

    def copies(bb, slot):
        res = []
        for g in range(N_DIL):
            win = states[g].shape[1]
            res.append(pltpu.make_async_copy(states[g].at[bb, pl.ds(SUB, win - SUB)],
                                             outs[g].at[bb, pl.ds(0, win - SUB)], sem_c.at[slot, g]))
            res.append(pltpu.make_async_copy(new_s.at[slot, g], outs[g].at[bb, pl.ds(win - SUB, SUB)],
                                             sem_n.at[slot, g]))
        return res

    slot = b % 2
    for g in range(N_DIL):
        new_s[slot, g, :, 0:w] = kn_ref[:, g * w:(g + 1) * w]
        new_s[slot, g, :, w:2 * w] = vn_ref[:, g * w:(g + 1) * w]
    for cp in copies(b, slot):
        cp.start()

    @pl.when(b > 0)
    def _():
        for cp in copies(b - 1, 1 - slot):
            cp.wait()

    @pl.when(b == nb - 1)
    def _():
        for cp in copies(b, slot):
            cp.wait()


def _win_shift(kf_s, z1, states, row0):
    nb = states[0].shape[0]
    w = H_C * DH_C
    wq = N_DIL * w
    r0 = row0 // SUB
    flat = [s.reshape(nb, s.shape[1], 2 * w) for s in states]
    any_spec = pl.BlockSpec(memory_space=pl.ANY)
    outs = pl.pallas_call(
        _win_shift_kernel,
        grid=(nb,),
        in_specs=[pl.BlockSpec((SUB, wq), lambda b: (b, 0)), pl.BlockSpec((SUB, wq), lambda b: (r0 + b, 2))]
                 + [any_spec] * 3,
        out_specs=[any_spec] * 3,
        out_shape=[jax.ShapeDtypeStruct(f.shape, F32) for f in flat],
        scratch_shapes=[pltpu.VMEM((2, N_DIL, SUB, 2 * w), F32), pltpu.SemaphoreType.DMA((2, N_DIL)),
                        pltpu.SemaphoreType.DMA((2, N_DIL))],
        compiler_params=_cparams(("arbitrary",)),
        name="win_shift",
    )(kf_s, z1, *flat)
    return [o.reshape(s.shape) for o, s in zip(outs, states)]


def _router_kernel(h_ref, r_ref, idx_ref, gate_ref):
    logits = jnp.dot(h_ref[...], r_ref[...].astype(BF16), preferred_element_type=F32)
    lane = lax.broadcasted_iota(I32, logits.shape, 1)
    lanef = lane.astype(F32)
    lg = jnp.where(lane < N_EXPERTS, logits, -jnp.inf)
    m1 = jnp.max(lg, axis=1, keepdims=True)
    i1 = jnp.min(jnp.where(lg == m1, lanef, float(LANE)), axis=1, keepdims=True)
    lg2 = jnp.where(lanef == i1, -jnp.inf, lg)
    m2 = jnp.max(lg2, axis=1, keepdims=True)
    i2 = jnp.min(jnp.where(lg2 == m2, lanef, float(LANE)), axis=1, keepdims=True)
    e = jnp.exp(m2 - m1)
    g1 = 1.0 / (1.0 + e)
    g2 = e / (1.0 + e)
    idx_ref[...] = jnp.where(lane == 0, i1, jnp.where(lane == 1, i2, 0.0)).astype(I32)
    gate_ref[...] = jnp.where(lane == 0, g1, jnp.where(lane == 1, g2, 0.0))


def _router(h, router):
    t = h.shape[0]
    tm = 512
    rp = jnp.pad(router, ((0, 0), (0, LANE - N_EXPERTS)))
    return pl.pallas_call(
        _router_kernel,
        grid=(t // tm,),
        in_specs=[pl.BlockSpec((tm, D), lambda i: (i, 0)), pl.BlockSpec((D, LANE), lambda i: (0, 0))],
        out_specs=[pl.BlockSpec((tm, LANE), lambda i: (i, 0))] * 2,
        out_shape=[jax.ShapeDtypeStruct((t, LANE), I32), jax.ShapeDtypeStruct((t, LANE), F32)],
        compiler_params=_cparams(("parallel",)),
        name="router",
    )(h, rp)


def _moe_gather_kernel(tok_ref, h_hbm, o_ref, buf, sem, *, tm):
    base = pl.program_id(0) * tm

    def issue(r, carry):
        pltpu.make_async_copy(h_hbm.at[pl.ds(tok_ref[base + r], 1)], buf.at[pl.ds(r, 1)], sem).start()
        return carry

    def wait(r, carry):
        pltpu.make_async_copy(h_hbm.at[pl.ds(0, 1)], buf.at[pl.ds(r, 1)], sem).wait()
        return carry

    lax.fori_loop(0, tm, issue, 0)
    lax.fori_loop(0, tm, wait, 0)
    o_ref[...] = buf[...].astype(BF16)


def _moe_gather(tok_of_slot, hf):
    p = tok_of_slot.shape[0]
    tm = MOE_TM
    grid_spec = pltpu.PrefetchScalarGridSpec(
        num_scalar_prefetch=1, grid=(p // tm,),
        in_specs=[pl.BlockSpec(memory_space=pl.ANY)],
        out_specs=pl.BlockSpec((tm, D), lambda i, tok: (i, 0)),
        scratch_shapes=[pltpu.VMEM((tm, D), F32), pltpu.SemaphoreType.DMA(())])
    return pl.pallas_call(
        functools.partial(_moe_gather_kernel, tm=tm),
        grid_spec=grid_spec,
        out_shape=jax.ShapeDtypeStruct((p, D), BF16),
        compiler_params=_cparams(("arbitrary",)),
        name="moe_gather",
    )(tok_of_slot, hf)


def _moe_up_kernel(te_ref, nu_ref, x_ref, wg_ref, wu_ref, o_ref, wgb_ref, wub_ref):
    i = pl.program_id(1)
    changed = jnp.logical_or(i == 0, te_ref[i] != te_ref[jnp.maximum(i - 1, 0)])

    @pl.when(changed)
    def _():
        wgb_ref[...] = wg_ref[0].astype(BF16)
        wub_ref[...] = wu_ref[0].astype(BF16)

    @pl.when(i < nu_ref[0])
    def _():
        x = x_ref[...]
        a = jnp.dot(x, wgb_ref[...], preferred_element_type=F32)
        b = jnp.dot(x, wub_ref[...], preferred_element_type=F32)
        o_ref[...] = ((a * jax.nn.sigmoid(a)) * b).astype(BF16)

    @pl.when(i >= nu_ref[0])
    def _():
        o_ref[...] = jnp.zeros(o_ref.shape, BF16)


def _moe_up(te, nu, xs, wg, wu, tn=512):
    p = xs.shape[0]
    tm = MOE_TM
    n = wg.shape[2]
    grid_spec = pltpu.PrefetchScalarGridSpec(
        num_scalar_prefetch=2, grid=(n // tn, p // tm),
        in_specs=[pl.BlockSpec((tm, D), lambda j, i, te, nu: (i, 0)),
                  pl.BlockSpec((1, D, tn), lambda j, i, te, nu: (te[i], 0, j)),
                  pl.BlockSpec((1, D, tn), lambda j, i, te, nu: (te[i], 0, j))],
        out_specs=pl.BlockSpec((tm, tn), lambda j, i, te, nu: (i, j)),
        scratch_shapes=[pltpu.VMEM((D, tn), BF16), pltpu.VMEM((D, tn), BF16)])
    return pl.pallas_call(
        _moe_up_kernel, grid_spec=grid_spec,
        out_shape=jax.ShapeDtypeStruct((p, n), BF16),
        compiler_params=_cparams(("arbitrary", "arbitrary"), VMEM_BIG),
        name="moe_up",
    )(te, nu, xs, wg, wu)


def _moe_down_kernel(te_ref, nu_ref, x_ref, w_ref, o_ref, wb_ref):
    i = pl.program_id(1)
    changed = jnp.logical_or(i == 0, te_ref[i] != te_ref[jnp.maximum(i - 1, 0)])

    @pl.when(changed)
    def _():
        wb_ref[...] = w_ref[0].astype(BF16)

    @pl.when(i < nu_ref[0])
    def _():
        o_ref[...] = jnp.dot(x_ref[...], wb_ref[...], preferred_element_type=F32)

    @pl.when(i >= nu_ref[0])
    def _():
        o_ref[...] = jnp.zeros(o_ref.shape, F32)


def _moe_down(te, nu, act, wd, tn=512):
    p, k = act.shape
    tm = MOE_TM
    n = wd.shape[2]
    grid_spec = pltpu.PrefetchScalarGridSpec(
        num_scalar_prefetch=2, grid=(n // tn, p // tm),
        in_specs=[pl.BlockSpec((tm, k), lambda j, i, te, nu: (i, 0)),
                  pl.BlockSpec((1, k, tn), lambda j, i, te, nu: (te[i], 0, j))],
        out_specs=pl.BlockSpec((tm, tn), lambda j, i, te, nu: (i, j)),
        scratch_shapes=[pltpu.VMEM((k, tn), BF16)])
    return pl.pallas_call(
        _moe_down_kernel, grid_spec=grid_spec,
        out_shape=jax.ShapeDtypeStruct((p, n), F32),
        compiler_params=_cparams(("arbitrary", "arbitrary"), VMEM_BIG),
        name="moe_down",
    )(te, nu, act, wd)


def _moe_combine_kernel(slot_ref, ys_hbm, x_ref, gt_ref, g_ref, o_ref, buf_a, buf_b, sem, *, tm):
    base = pl.program_id(0) * tm

    def issue(r, carry):
        pltpu.make_async_copy(ys_hbm.at[pl.ds(slot_ref[2 * (base + r)], 1)], buf_a.at[pl.ds(r, 1)], sem).start()
        pltpu.make_async_copy(ys_hbm.at[pl.ds(slot_ref[2 * (base + r) + 1], 1)], buf_b.at[pl.ds(r, 1)], sem).start()
        return carry

    def wait(r, carry):
        pltpu.make_async_copy(ys_hbm.at[pl.ds(0, 1)], buf_a.at[pl.ds(r, 1)], sem).wait()
        pltpu.make_async_copy(ys_hbm.at[pl.ds(0, 1)], buf_b.at[pl.ds(r, 1)], sem).wait()
        return carry

    lax.fori_loop(0, tm, issue, 0)
    lax.fori_loop(0, tm, wait, 0)
    gt = gt_ref[...]
    y = gt[:, 0:1] * buf_a[...] + gt[:, 1:2] * buf_b[...]
    o_ref[...] = x_ref[...] + g_ref[...] * y.reshape(o_ref.shape)


def _moe_combine(slot_of_assign, ys, x, gates, mod, k_gate):
    t = x.shape[0]
    tm = 256
    gb = tm // SUB
    grid_spec = pltpu.PrefetchScalarGridSpec(
        num_scalar_prefetch=1, grid=(t // tm,),
        in_specs=[pl.BlockSpec(memory_space=pl.ANY),
                  pl.BlockSpec((gb, SUB, D), lambda i, sl: (i, 0, 0)),
                  pl.BlockSpec((tm, LANE), lambda i, sl: (i, 0)),
                  pl.BlockSpec((gb, 1, D), lambda i, sl: (i, 0, k_gate))],
        out_specs=pl.BlockSpec((gb, SUB, D), lambda i, sl: (i, 0, 0)),
        scratch_shapes=[pltpu.VMEM((tm, D), F32), pltpu.VMEM((tm, D), F32), pltpu.SemaphoreType.DMA(())])
    out = pl.pallas_call(
        functools.partial(_moe_combine_kernel, tm=tm),
        grid_spec=grid_spec,
        out_shape=jax.ShapeDtypeStruct((t // SUB, SUB, D), F32),
        compiler_params=_cparams(("arbitrary",)),
        name="moe_combine",
    )(slot_of_assign, ys, x.reshape(t // SUB, SUB, D), gates, mod)
    return out.reshape(t, D)


def _moe_plan(idx, t):
    tm = MOE_TM
    n_assign = 2 * t
    p = n_assign + N_EXPERTS * tm
    e_flat = idx[:, 0:2].reshape(n_assign)
    onehot = (e_flat[:, None] == jnp.arange(N_EXPERTS, dtype=I32)[None, :]).astype(I32)
    csum = jnp.cumsum(onehot, axis=0)
    rank = jnp.take_along_axis(csum, e_flat[:, None], axis=1)[:, 0] - 1
    counts = csum[-1]
    padded = ((counts + tm - 1) // tm) * tm
    ends = jnp.cumsum(padded)
    slot = (ends - padded)[e_flat] + rank
    tok_of_slot = jnp.zeros((p,), I32).at[slot].set(jnp.arange(n_assign, dtype=I32) // 2)
    tile_start = jnp.arange(p // tm, dtype=I32) * tm
    te = jnp.minimum(jnp.searchsorted(ends, tile_start, side="right").astype(I32), N_EXPERTS - 1)
    n_used = (ends[-1] // tm).astype(I32).reshape(1)
    return slot.astype(I32), tok_of_slot, te, n_used


def kernel(x_prompt, x_sample, c_prompt, c_sample, page_table, cache_a_k, cache_a_v, cache_b_lat, cache_b_krope,
           state_c_win0, state_c_win1, state_c_win2, ada_w, ada_b, norm_mix, norm_ffn, l0_w_in, l0_a_qnorm,
           l0_a_knorm, l0_a_lambda, l0_a_subln, l0_b_qa_norm, l0_b_w_uq, l0_b_kv_norm, l0_b_w_ukv, l0_b_qnorm,
           l0_b_knorm, l0_w_out, l0_ffn_gate, l0_ffn_up, l0_ffn_down, l1_w_in, l1_c_qnorm, l1_c_knorm, l1_w_out,
           l1_router, l1_moe_gate, l1_moe_up, l1_moe_down):
    nbp, seq, _ = x_prompt.shape
    nbs, dseq, _ = x_sample.shape
    assert dseq == SUB
    n_pages = page_table.shape[1]
    past = n_pages * PAGE
    tp, ts = nbp * seq, nbs * dseq
    t = tp + ts
    n_pool = cache_a_k.shape[0]

    nc = nbp + nbs
    ncp = -(-nc // SUB) * SUB
    c_all = jnp.pad(jnp.concatenate([c_prompt, c_sample], axis=0), ((0, ncp - nc), (0, 0)))
    mods = _adaln_all(c_all, ada_w, ada_b)
    rg_idx = jnp.concatenate([jnp.repeat(jnp.arange(nbp, dtype=I32), seq // SUB), nbp + jnp.arange(nbs, dtype=I32)])
    mod = [mods[l][rg_idx][:, None, :] for l in range(DEPTH)]

    x = jnp.concatenate([x_prompt.reshape(tp, D), x_sample.reshape(ts, D)], axis=0)
    pos = jnp.concatenate([jnp.tile(jnp.arange(seq, dtype=I32), nbp), jnp.tile(past + jnp.arange(dseq, dtype=I32), nbs)])
    tab_a = _rope_tables(pos, DH_A // 8, 0, DH_A)
    tab_k = _rope_tables(pos, ROPE_B // 2, 0, LANE)
    tab_b = _rope_tables(pos, ROPE_B // 2, NOPE_B, LANE)

    eye2 = jnp.eye(2, dtype=F32)
    w_qa = l0_w_in[:, :1024].reshape(D, H_A, 2, 1, DH_A) * eye2[None, None, :, :, None]
    w0 = jnp.concatenate([w_qa.reshape(D, 2 * H_A * LANE), l0_w_in[:, 1024:2336],
                          jnp.zeros((D, LANE - ROPE_B), F32)], axis=1)
    padl = lambda a, n: jnp.pad(a, ((0, 0),) * (a.ndim - 1) + ((0, n),))
    wuq = padl(l0_b_w_uq.reshape(Q_LORA, H_B, QK_B), LANE - QK_B).reshape(Q_LORA, H_B * LANE).astype(BF16)
    wukv = l0_b_w_ukv.reshape(KV_LORA, H_B, NOPE_B + VH_B)
    wuk_pad = padl(wukv[:, :, :NOPE_B], LANE - NOPE_B).reshape(KV_LORA, H_B * LANE).astype(BF16)
    wuk = wukv[:, :, :NOPE_B].reshape(KV_LORA, H_B * NOPE_B).astype(BF16)
    wuv = wukv[:, :, NOPE_B:].reshape(KV_LORA, H_B * VH_B).astype(BF16)
    wukt = jnp.pad(jnp.transpose(wukv[:, :, :NOPE_B], (1, 2, 0)), ((0, 0), (0, LANE - NOPE_B), (0, 0)))
    g_qa = jnp.tile(l0_a_qnorm, 2).reshape(1, LANE)
    g_ka = jnp.tile(l0_a_knorm, 2).reshape(1, LANE)
    g_qb = padl(l0_b_qnorm, LANE - QK_B).reshape(1, LANE)
    g_kb = padl(l0_b_knorm, LANE - QK_B).reshape(1, LANE)
    gains0 = (g_qa, g_ka, l0_b_qa_norm.reshape(1, Q_LORA), l0_b_kv_norm.reshape(1, KV_LORA), g_qb, g_kb)
    subln = l0_a_subln.reshape(1, LANE)

    h = _norm_mod(x, norm_mix[0], mod[0], 1, 0)
    z0 = _mm(h, w0, tn=384)
    tabs = (tab_a, tab_b, tab_k)
    q_p, ka_p, va_p, lat_p, kr_p, ks_p, vs_p = _proj0_post(
        z0, tabs, gains0, (wuq, wuk_pad, wuv), prompt=True, nb=nbp, seq=seq, row0=0)
    q_s, ka_s, va_s, lat_s, kr_s = _proj0_post(
        z0, tabs, gains0, (wuq, wuk_pad, wuv), prompt=False, nb=nbs, seq=dseq, row0=tp)
    o_p = _flash(q_p, ks_p, vs_p)
    a_p = _attn0_post(o_p, l0_a_lambda, subln)
    caches = (cache_a_k.reshape(n_pool, PAGE, 2 * LANE), cache_a_v.reshape(n_pool, PAGE, 2 * LANE),
              cache_b_lat, cache_b_krope)
    wuv_f = wukv[:, :, NOPE_B:].reshape(KV_LORA, H_B * VH_B)
    a_s = _decode0(page_table, q_s, ka_s, va_s, lat_s, kr_s, caches, (wukt, wuk, wuv_f), g_kb, l0_a_lambda, subln)
    a0 = jnp.concatenate([a_p, a_s.astype(BF16)], axis=0)
    x = _mm_res(a0, l0_w_out, x, mod[0], 2)
    h = _norm_mod(x, norm_ffn[0], mod[0], 4, 3)
    act = _mm_swiglu(h, l0_ffn_gate, l0_ffn_up)
    x = _mm_res(act, l0_ffn_down, x, mod[0], 5)

    h = _norm_mod(x, norm_mix[1], mod[1], 1, 0)
    z1 = _mm(h, l1_w_in, tn=512)
    gq1 = l1_c_qnorm.reshape(1, LANE)
    gk1 = l1_c_knorm.reshape(1, LANE)
    q1p, k1p, kf1p, v1p = _proj1_post(z1, tab_k, gq1, gk1, n_rows=tp, row0=0)
    q1s, _, kf1s, _ = _proj1_post(z1, tab_k, gq1, gk1, n_rows=ts, row0=tp)
    q1s_f = q1s.astype(F32)
    states = (state_c_win0, state_c_win1, state_c_win2)
    outs_p, lses_p, outs_s, lses_s = [], [], [], []
    for g, (_, dil) in enumerate(DIL_GROUPS):
        o, lse = _dil_prompt(q1p, k1p, v1p, g, dil, nbp, seq)
        outs_p.append(o)
        lses_p.append(lse)
        o, lse = _dil_sample(q1s_f, kf1s, z1, states[g], g, dil, tp)
        outs_s.append(o)
        lses_s.append(lse)
    a1 = jnp.concatenate([_dil_combine(outs_p, lses_p, BF16, 256),
                          _dil_combine(outs_s, lses_s, F32, 256).astype(BF16)], axis=0)
    x = _mm_res(a1, l1_w_out, x, mod[1], 2)
    h, hf = _norm_mod(x, norm_ffn[1], mod[1], 4, 3, want_f32=True)
    idx, gates = _router(h, l1_router)
    slot, tok_of_slot, te, n_used = _moe_plan(idx, t)
    xs = _moe_gather(tok_of_slot, hf)
    act = _moe_up(te, n_used, xs, l1_moe_gate, l1_moe_up)
    ys = _moe_down(te, n_used, act, l1_moe_down)
    x = _moe_combine(slot, ys, x, gates, mod[1], 5)

    y_p = x[:tp].reshape(nbp, seq, D)
    y_s = x[tp:].reshape(nbs, dseq, D)
    w = H_C * DH_C
    v1f = z1[:, 2 * N_DIL * w:]
    win_p = []
    for g, (win, _) in enumerate(DIL_GROUPS):
        wl = min(win, seq)
        kk = kf1p.reshape(nbp, seq, N_DIL, H_C, DH_C)[:, seq - wl:, g]
        vv = v1f[:tp].reshape(nbp, seq, N_DIL, H_C, DH_C)[:, seq - wl:, g]
        win_p.append(jnp.stack([kk, vv], axis=2))
    win_s = _win_shift(kf1s, z1, states, tp)
    return (y_p, y_s,
            ka_p.reshape(nbp, seq, KV_A, 2 * DH_A), ka_s.reshape(nbs, dseq, KV_A, 2 * DH_A),
            va_p.reshape(nbp, seq, KV_A, 2 * DH_A), va_s.reshape(nbs, dseq, KV_A, 2 * DH_A),
            lat_p.reshape(nbp, seq, KV_LORA), lat_s.reshape(nbs, dseq, KV_LORA),
            kr_p.reshape(nbp, seq, ROPE_B), kr_s.reshape(nbs, dseq, ROPE_B),
            win_p[0], win_s[0], win_p[1], win_s[1], win_p[2], win_s[2])
```

```python
import functools

import jax
import jax.numpy as jnp
from jax import lax
from jax.experimental import pallas as pl
from jax.experimental.pallas import tpu as pltpu

F32 = jnp.float32
BF16 = jnp.bfloat16
I32 = jnp.int32

D = 2048
DEPTH = 2
PAGE = 128
ROPE_THETA = 500000.0
EPS = 1e-6
NEG_INF = -1e30

H_A, KV_A, DH_A = 8, 2, 64
LAMBDA_INIT = 0.2
H_B, Q_LORA, KV_LORA, NOPE_B, ROPE_B, VH_B = 8, 512, 256, 64, 32, 128
QK_B = NOPE_B + ROPE_B
H_C, DH_C = 8, 128
DIL_GROUPS = ((128, 1), (512, 4), (2048, 16))
N_DIL = 3
D_FF = 5632
N_EXPERTS = 8
D_FF_E = 7168
OUT1 = H_C * DH_C

LANE = 128
SUB = 8
VMEM_BIG = 56 * 1024 * 1024

N_QS = 24
N_KS = 10
Z0_W = 27 * LANE
PAGES_PER_STEP = 16
MOE_TM = 512
MOE_TM_DOWN = 256
LOG2E = 1.4426950408889634


def _cparams(sem, vmem=None):
    return pltpu.CompilerParams(dimension_semantics=sem, vmem_limit_bytes=vmem)


def _adaln_kernel(c_ref, w_ref, b_ref, o_ref):
    c = c_ref[...]
    a = (c * jax.nn.sigmoid(c)).astype(BF16)
    o_ref[0] = jnp.dot(a, w_ref[0].astype(BF16), preferred_element_type=F32) + b_ref[0]


def _adaln_all(c_all, ada_w, ada_b):
    nb = c_all.shape[0]
    tn = 1024
    return pl.pallas_call(
        _adaln_kernel,
        grid=(DEPTH, 6 * D // tn),
        in_specs=[pl.BlockSpec((nb, D), lambda l, j: (0, 0)),
                  pl.BlockSpec((1, D, tn), lambda l, j: (l, 0, j)),
                  pl.BlockSpec((1, 1, tn), lambda l, j: (l, 0, j))],
        out_specs=pl.BlockSpec((1, nb, tn), lambda l, j: (l, 0, j)),
        out_shape=jax.ShapeDtypeStruct((DEPTH, nb, 6 * D), F32),
        compiler_params=_cparams(("parallel", "parallel")),
        name="adaln",
    )(c_all, ada_w, ada_b.reshape(DEPTH, 1, 6 * D))


def _norm_mod_kernel(x_ref, g_ref, sc_ref, sh_ref, o_ref, *of_ref):
    x = x_ref[...]
    ms = jnp.mean(x * x, axis=-1, keepdims=True)
    y = (x * lax.rsqrt(ms + EPS)) * g_ref[...]
    y = y * (1.0 + sc_ref[...]) + sh_ref[...]
    y2 = y.reshape(o_ref.shape)
    o_ref[...] = y2.astype(BF16)
    if of_ref:
        of_ref[0][...] = y2


def _norm_mod(x, gain, mod, k_sc, k_sh, want_f32=False):
    t = x.shape[0]
    gb = 32
    tm = gb * SUB
    out_shape = [jax.ShapeDtypeStruct((t, D), BF16)]
    out_specs = [pl.BlockSpec((tm, D), lambda i: (i, 0))]
    if want_f32:
        out_shape.append(jax.ShapeDtypeStruct((t, D), F32))
        out_specs.append(pl.BlockSpec((tm, D), lambda i: (i, 0)))
    res = pl.pallas_call(
        _norm_mod_kernel,
        grid=(t // tm,),
        in_specs=[pl.BlockSpec((gb, SUB, D), lambda i: (i, 0, 0)),
                  pl.BlockSpec((1, 1, D), lambda i: (0, 0, 0)),
                  pl.BlockSpec((gb, 1, D), lambda i: (i, 0, k_sc)),
                  pl.BlockSpec((gb, 1, D), lambda i: (i, 0, k_sh))],
        out_specs=out_specs,
        out_shape=out_shape,
        compiler_params=_cparams(("parallel",)),
        name="norm_mod",
    )(x.reshape(t // SUB, SUB, D), gain.reshape(1, 1, D), mod, mod)
    return res if want_f32 else res[0]


def _mm_kernel(x_ref, w_ref, o_ref, wb_ref):
    @pl.when(pl.program_id(1) == 0)
    def _():
        wb_ref[...] = w_ref[...].astype(BF16)
    o_ref[...] = jnp.dot(x_ref[...], wb_ref[...], preferred_element_type=F32).astype(o_ref.dtype)


def _mm(x, w, tm=512, tn=512, out_dtype=F32):
    m, k = x.shape
    n = w.shape[1]
    return pl.pallas_call(
        _mm_kernel,
        grid=(n // tn, m // tm),
        in_specs=[pl.BlockSpec((tm, k), lambda j, i: (i, 0)),
                  pl.BlockSpec((k, tn), lambda j, i: (0, j))],
        out_specs=pl.BlockSpec((tm, tn), lambda j, i: (i, j)),
        out_shape=jax.ShapeDtypeStruct((m, n), out_dtype),
        scratch_shapes=[pltpu.VMEM((k, tn), BF16)],
        compiler_params=_cparams(("arbitrary", "arbitrary"), VMEM_BIG),
        name="mm",
    )(x, w)


def _mm_res_kernel(x_ref, w_ref, r_ref, g_ref, o_ref, wb_ref):
    @pl.when(pl.program_id(1) == 0)
    def _():
        wb_ref[...] = w_ref[...].astype(BF16)
    acc = jnp.dot(x_ref[...], wb_ref[...], preferred_element_type=F32)
    o_ref[...] = r_ref[...] + g_ref[...] * acc.reshape(o_ref.shape)


def _mm_res(x, w, res, mod, k_gate, tm=512, tn=512):
    m, k = x.shape
    n = w.shape[1]
    gb = tm // SUB
    nj = n // tn
    out = pl.pallas_call(
        _mm_res_kernel,
        grid=(nj, m // tm),
        in_specs=[pl.BlockSpec((tm, k), lambda j, i: (i, 0)),
                  pl.BlockSpec((k, tn), lambda j, i: (0, j)),
                  pl.BlockSpec((gb, SUB, tn), lambda j, i: (i, 0, j)),
                  pl.BlockSpec((gb, 1, tn), lambda j, i: (i, 0, k_gate * nj + j))],
        out_specs=pl.BlockSpec((gb, SUB, tn), lambda j, i: (i, 0, j)),
        out_shape=jax.ShapeDtypeStruct((m // SUB, SUB, n), F32),
        scratch_shapes=[pltpu.VMEM((k, tn), BF16)],
        compiler_params=_cparams(("arbitrary", "arbitrary"), VMEM_BIG),
        name="mm_res",
    )(x, w, res.reshape(m // SUB, SUB, n), mod)
    return out.reshape(m, n)


def _mm_swiglu_kernel(x_ref, wg_ref, wu_ref, o_ref, wgb_ref, wub_ref):
    @pl.when(pl.program_id(1) == 0)
    def _():
        wgb_ref[...] = wg_ref[...].astype(BF16)
        wub_ref[...] = wu_ref[...].astype(BF16)
    x = x_ref[...]
    a = jnp.dot(x, wgb_ref[...], preferred_element_type=F32)
    b = jnp.dot(x, wub_ref[...], preferred_element_type=F32)
    o_ref[...] = ((a * jax.nn.sigmoid(a)) * b).astype(BF16)


def _mm_swiglu(x, wg, wu, tm=512, tn=512):
    m, k = x.shape
    n = wg.shape[1]
    return pl.pallas_call(
        _mm_swiglu_kernel,
        grid=(n // tn, m // tm),
        in_specs=[pl.BlockSpec((tm, k), lambda j, i: (i, 0)),
                  pl.BlockSpec((k, tn), lambda j, i: (0, j)),
                  pl.BlockSpec((k, tn), lambda j, i: (0, j))],
        out_specs=pl.BlockSpec((tm, tn), lambda j, i: (i, j)),
        out_shape=jax.ShapeDtypeStruct((m, n), BF16),
        scratch_shapes=[pltpu.VMEM((k, tn), BF16), pltpu.VMEM((k, tn), BF16)],
        compiler_params=_cparams(("arbitrary", "arbitrary"), VMEM_BIG),
        name="mm_swiglu",
    )(x, wg, wu)


def _rope_tables(pos, half, offset, period):
    inv_freq = ROPE_THETA ** (-jnp.arange(half, dtype=F32) / half)
    ang = pos.astype(F32)[:, None] * inv_freq[None, :]
    cos, sin = jnp.cos(ang), jnp.sin(ang)
    n = pos.shape[0]
    seg_c = jnp.ones((n, period), F32)
    seg_c = seg_c.at[:, offset:offset + half].set(cos).at[:, offset + half:offset + 2 * half].set(cos)
    seg_m = jnp.zeros((n, period), F32).at[:, offset:offset + half].set(-sin)
    seg_p = jnp.zeros((n, period), F32).at[:, offset + half:offset + 2 * half].set(sin)
    rep = LANE // period
    return jnp.concatenate([jnp.tile(seg_c, (1, rep)), jnp.tile(seg_m, (1, rep)), jnp.tile(seg_p, (1, rep))], axis=1)


def _rope(x, tab, half):
    c = tab[:, 0:LANE]
    sm = tab[:, LANE:2 * LANE]
    sp = tab[:, 2 * LANE:3 * LANE]
    return x * c + pltpu.roll(x, LANE - half, 1) * sm + pltpu.roll(x, half, 1) * sp


def _rms(x, n_valid):
    return lax.rsqrt(jnp.sum(x * x, axis=-1, keepdims=True) * (1.0 / n_valid) + EPS)


def _proj0_post_kernel(z_ref, ta_ref, tb_ref, tk_ref, gq_ref, gk_ref, gqa_ref, gkv_ref, gqb_ref, gkb_ref,
                       wuq_ref, wuk_ref, wuv_ref, *out_refs, prompt):
    if prompt:
        q_ref, ka_ref, va_ref, lat_ref, kr_ref, ks_ref, vs_ref = out_refs
    else:
        q_ref, ka_ref, va_ref, lat_ref, kr_ref = out_refs
    ta = ta_ref[...]
    tb = tb_ref[...]
    tk = tk_ref[...]
    lane = lax.broadcasted_iota(I32, (1, LANE), 1)
    lo = lane < DH_A

    def put_q(s, val):
        if prompt:
            q_ref[0, s] = val.astype(BF16)
        else:
            q_ref[:, s * LANE:(s + 1) * LANE] = val

    for s in range(2 * H_A):
        x = z_ref[:, s * LANE:(s + 1) * LANE]
        y = (x * _rms(x, DH_A)) * gq_ref[...]
        put_q(s, _rope(y, ta, DH_A // 8) * (DH_A ** -0.5 * LOG2E))
    for kv in range(KV_A):
        x = z_ref[:, (16 + kv) * LANE:(17 + kv) * LANE]
        xx = x * x
        s_lo = jnp.sum(jnp.where(lo, xx, 0.0), axis=-1, keepdims=True)
        s_hi = jnp.sum(jnp.where(lo, 0.0, xx), axis=-1, keepdims=True)
        inv = jnp.where(lo, lax.rsqrt(s_lo * (1.0 / DH_A) + EPS), lax.rsqrt(s_hi * (1.0 / DH_A) + EPS))
        k = _rope((x * inv) * gk_ref[...], ta, DH_A // 8)
        v = z_ref[:, (18 + kv) * LANE:(19 + kv) * LANE]
        ka_ref[:, kv * LANE:(kv + 1) * LANE] = k
        va_ref[:, kv * LANE:(kv + 1) * LANE] = v
        if prompt:
            ks_ref[0, kv] = k.astype(BF16)
            vs_ref[0, kv] = v.astype(BF16)
    qc = z_ref[:, 20 * LANE:24 * LANE]
    qcn = (qc * _rms(qc, Q_LORA)) * gqa_ref[...]
    qb = jnp.dot(qcn.astype(BF16), wuq_ref[...], preferred_element_type=F32)
    for h in range(H_B):
        x = _rope(qb[:, h * LANE:(h + 1) * LANE], tb, ROPE_B // 2)
        y = (x * _rms(x, QK_B)) * gqb_ref[...]
        put_q(2 * H_A + h, y * (QK_B ** -0.5 * LOG2E))
    kvc = z_ref[:, 24 * LANE:26 * LANE]
    lat = (kvc * _rms(kvc, KV_LORA)) * gkv_ref[...]
    lat_ref[...] = lat
    kr = _rope(z_ref[:, 26 * LANE:27 * LANE], tk, ROPE_B // 2)
    kr_ref[...] = kr[:, 0:ROPE_B]
    if prompt:
        latb = lat.astype(BF16)
        kn = jnp.dot(latb, wuk_ref[...], preferred_element_type=F32)
        vv = jnp.dot(latb, wuv_ref[...], preferred_element_type=F32)
        kr_hi = pltpu.roll(kr, NOPE_B, 1)
        for h in range(H_B):
            x = kn[:, h * LANE:(h + 1) * LANE] + kr_hi
            ks_ref[0, KV_A + h] = ((x * _rms(x, QK_B)) * gkb_ref[...]).astype(BF16)
            vs_ref[0, KV_A + h] = vv[:, h * LANE:(h + 1) * LANE].astype(BF16)


def _proj0_post(z0, tabs, gains, wts, *, prompt, nb, seq, row0):
    ta, tb, tk = tabs
    tm = 256
    n_rows = nb * seq
    nt = n_rows // tm
    r0 = row0 // tm
    if prompt:
        per_b = seq // tm
        rows = lambda i: (i, 0)
        grid = (nt,)
        zmap = lambda i: (r0 + i, 0)
        q_spec = pl.BlockSpec((1, N_QS, tm, LANE), lambda i: (i // per_b, 0, i % per_b, 0))
        kv_spec = pl.BlockSpec((1, N_KS, tm, LANE), lambda i: (i // per_b, 0, i % per_b, 0))
        out_shape = [jax.ShapeDtypeStruct((nb, N_QS, seq, LANE), BF16)]
        out_specs = [q_spec]
    else:
        rows = lambda i: (i, 0)
        grid = (nt,)
        zmap = lambda i: (r0 + i, 0)
        out_shape = [jax.ShapeDtypeStruct((n_rows, N_QS * LANE), F32)]
        out_specs = [pl.BlockSpec((tm, N_QS * LANE), rows)]
    out_shape += [jax.ShapeDtypeStruct((n_rows, 2 * LANE), F32)] * 3 + [jax.ShapeDtypeStruct((n_rows, ROPE_B), F32)]
    out_specs += [pl.BlockSpec((tm, 2 * LANE), rows)] * 3 + [pl.BlockSpec((tm, ROPE_B), rows)]
    if prompt:
        out_shape += [jax.ShapeDtypeStruct((nb, N_KS, seq, LANE), BF16)] * 2
        out_specs += [kv_spec, kv_spec]
    tab_spec = pl.BlockSpec((tm, 3 * LANE), zmap)
    const = lambda a: pl.BlockSpec(a.shape, lambda i: (0,) * a.ndim)
    return pl.pallas_call(
        functools.partial(_proj0_post_kernel, prompt=prompt),
        grid=grid,
        in_specs=[pl.BlockSpec((tm, Z0_W), zmap), tab_spec, tab_spec, tab_spec]
                 + [const(a) for a in gains] + [const(a) for a in wts],
        out_specs=out_specs,
        out_shape=out_shape,
        compiler_params=_cparams(("parallel",)),
        name="proj0_post_p" if prompt else "proj0_post_s",
    )(z0, ta, tb, tk, *gains, *wts)


def _flash_kernel(q_ref, k0_ref, v0_ref, k1_ref, v1_ref, o_ref, m_ref, l_ref, acc_ref, *, tq):
    qi = pl.program_id(2)
    kv_refs = ((k0_ref, v0_ref), (k1_ref, v1_ref))
    m_ref[...] = jnp.full(m_ref.shape, NEG_INF, F32)
    l_ref[...] = jnp.zeros(l_ref.shape, F32)
    acc_ref[...] = jnp.zeros(acc_ref.shape, F32)

    def step(ki, masked):
        start = pl.multiple_of(ki * tq, tq)
        for j, (k_ref, v_ref) in enumerate(kv_refs):
            k = k_ref[0, 0, pl.ds(start, tq), :]
            v = v_ref[0, 0, pl.ds(start, tq), :]
            s = lax.dot_general(q_ref[0, j], k, (((1,), (1,)), ((), ())), preferred_element_type=F32)
            if masked:
                r = lax.broadcasted_iota(I32, (tq, tq), 0)
                c = lax.broadcasted_iota(I32, (tq, tq), 1)
                s = jnp.where(c <= r, s, NEG_INF)
            m_prev = m_ref[j]
            m_new = jnp.maximum(m_prev, jnp.max(s, axis=1, keepdims=True))
            alpha = jnp.exp2(m_prev - m_new)
            p = jnp.exp2(s - m_new)
            l_ref[j] = alpha * l_ref[j] + jnp.sum(p, axis=1, keepdims=True)
            acc_ref[j] = alpha * acc_ref[j] + jnp.dot(p.astype(BF16), v, preferred_element_type=F32)
            m_ref[j] = m_new

    def body(ki, carry):
        step(ki, False)
        return carry

    lax.fori_loop(0, qi, body, 0)
    step(qi, True)
    for j in range(2):
        o_ref[0, j] = acc_ref[j] / l_ref[j]


def _flash(q, k, v, tq=512):
    nb, _, seq, _ = q.shape

    def kv_map(j):
        def m(b, p, i):
            s = 2 * p + j
            return (b, jnp.where(s < 2 * H_A, s // (2 * H_A // KV_A), s - (2 * H_A - KV_A)), 0, 0)
        return m

    kv_spec = lambda j: pl.BlockSpec((1, 1, seq, LANE), kv_map(j))
    return pl.pallas_call(
        functools.partial(_flash_kernel, tq=tq),
        grid=(nb, N_QS // 2, seq // tq),
        in_specs=[pl.BlockSpec((1, 2, tq, LANE), lambda b, p, i: (b, p, i, 0)),
                  kv_spec(0), kv_spec(0), kv_spec(1), kv_spec(1)],
        out_specs=pl.BlockSpec((1, 2, tq, LANE), lambda b, p, i: (b, p, i, 0)),
        out_shape=jax.ShapeDtypeStruct((nb, N_QS, seq, LANE), F32),
        scratch_shapes=[pltpu.VMEM((2, tq, 1), F32), pltpu.VMEM((2, tq, 1), F32), pltpu.VMEM((2, tq, LANE), F32)],
        compiler_params=_cparams(("parallel", "parallel", "arbitrary")),
        name="flash0",
    )(q, k, v, k, v)


def _diff_lambda(lam_ref):
    lf = lam_ref[...]
    a = jnp.sum(lf[0:1] * lf[1:2], axis=-1, keepdims=True)
    b = jnp.sum(lf[2:3] * lf[3:4], axis=-1, keepdims=True)
    return jnp.exp(a) - jnp.exp(b) + LAMBDA_INIT


def _attn0_post_kernel(o_ref, lam_ref, sub_ref, a_ref):
    lam = _diff_lambda(lam_ref)
    for h in range(H_A):
        d = o_ref[0, 2 * h] - lam * o_ref[0, 2 * h + 1]
        y = ((d * _rms(d, 2 * DH_A)) * sub_ref[...]) * (1.0 - LAMBDA_INIT)
        a_ref[:, h * LANE:(h + 1) * LANE] = y.astype(BF16)
    for h in range(H_B):
        a_ref[:, (H_A + h) * LANE:(H_A + h + 1) * LANE] = o_ref[0, 2 * H_A + h].astype(BF16)


def _attn0_post(o, lam_p, subln):
    nb, _, seq, _ = o.shape
    tm = 256
    per_b = seq // tm
    return pl.pallas_call(
        _attn0_post_kernel,
        grid=(nb * per_b,),
        in_specs=[pl.BlockSpec((1, N_QS, tm, LANE), lambda i: (i // per_b, 0, i % per_b, 0)),
                  pl.BlockSpec((4, DH_A), lambda i: (0, 0)),
                  pl.BlockSpec((1, LANE), lambda i: (0, 0))],
        out_specs=pl.BlockSpec((tm, D), lambda i: (i, 0)),
        out_shape=jax.ShapeDtypeStruct((nb * seq, D), BF16),
        compiler_params=_cparams(("parallel",)),
        name="attn0_post",
    )(o, lam_p, subln)


def _decode0_kernel(pt_ref, q_ref, kan_ref, van_ref, latn_ref, krn_ref, wukt_ref, wuk_ref, wuv_ref, gkb_ref,
                    lam_ref, sub_ref, *rest, n_chunks):
    pps = PAGES_PER_STEP
    ck = rest[0:pps]
    cv = rest[pps:2 * pps]
    cl = rest[2 * pps:3 * pps]
    cr = rest[3 * pps:4 * pps]
    o_ref = rest[4 * pps]
    ka_s, va_s, lx_s, krs_s, qa_s, qx_s, m_s, l_s, acca_s, accb_s = rest[4 * pps + 1:]
    c = pl.program_id(1)
    n_rows = 2 * H_A * SUB
    n_rows_b = H_B * SUB

    @pl.when(c == 0)
    def _():
        m_s[...] = jnp.full(m_s.shape, NEG_INF, F32)
        l_s[...] = jnp.zeros(l_s.shape, F32)
        acca_s[...] = jnp.zeros(acca_s.shape, F32)
        accb_s[...] = jnp.zeros(accb_s.shape, F32)
        lx_s[:, 2 * LANE:3 * LANE] = jnp.zeros((lx_s.shape[0], LANE), BF16)
        krs_s[...] = jnp.zeros(krs_s.shape, BF16)
        lane = lax.broadcasted_iota(I32, (SUB, LANE), 1)
        for s in range(2 * H_A):
            qa_s[s * SUB:(s + 1) * SUB, :] = q_ref[:, s * LANE:(s + 1) * LANE]
        for h in range(H_B):
            qk = q_ref[:, (2 * H_A + h) * LANE:(2 * H_A + h + 1) * LANE] * gkb_ref[...]
            qx_s[h * SUB:(h + 1) * SUB, 0:2 * LANE] = jnp.dot(qk, wukt_ref[h], preferred_element_type=F32)
            qx_s[h * SUB:(h + 1) * SUB, 2 * LANE:3 * LANE] = jnp.where(lane < ROPE_B, pltpu.roll(qk, NOPE_B, 1), 0.0)

    def online(rows, s, pv):
        m_prev = m_s[rows, :]
        m_new = jnp.maximum(m_prev, jnp.max(s, axis=1, keepdims=True))
        alpha = jnp.exp2(m_prev - m_new)
        p = jnp.exp2(s - m_new)
        l_s[rows, :] = alpha * l_s[rows, :] + jnp.sum(p, axis=1, keepdims=True)
        m_s[rows, :] = m_new
        return alpha, pv(p.astype(BF16))

    def process(n, causal):
        nt = (((1,), (1,)), ((), ()))
        if causal:
            row = lax.broadcasted_iota(I32, (n_rows_b, n), 0)
            col = lax.broadcasted_iota(I32, (n_rows_b, n), 1)
            keep = col <= (row & (SUB - 1))
        qa = qa_s[...].astype(BF16)
        for kv in range(KV_A):
            rows = pl.ds(kv * n_rows_b, n_rows_b)
            kk = ka_s[0:n, kv * LANE:(kv + 1) * LANE]
            vv = va_s[0:n, kv * LANE:(kv + 1) * LANE]
            s = lax.dot_general(qa[kv * n_rows_b:(kv + 1) * n_rows_b], kk, nt, preferred_element_type=F32)
            if causal:
                s = jnp.where(keep, s, NEG_INF)
            alpha, pv = online(rows, s, lambda p: jnp.dot(p, vv, preferred_element_type=F32))
            acca_s[rows, :] = alpha * acca_s[rows, :] + pv
        lx = lx_s[0:n, :]
        latb = lx[:, 0:2 * LANE]
        s_raw = lax.dot_general(qx_s[...].astype(BF16), lx, nt, preferred_element_type=F32)
        kn = jnp.dot(latb, wuk_ref[...], preferred_element_type=F32)
        seg = (lax.broadcasted_iota(I32, (H_B, H_B * NOPE_B), 1) // NOPE_B
               == lax.broadcasted_iota(I32, (H_B, H_B * NOPE_B), 0)).astype(BF16)
        n2 = lax.dot_general(seg, (kn * kn).astype(BF16), nt, preferred_element_type=F32)
        n2 = n2 + lax.dot_general(jnp.ones((H_B, LANE), BF16), krs_s[0:n, :], nt, preferred_element_type=F32)
        rinv = lax.rsqrt(n2 * (1.0 / QK_B) + EPS)
        s = (s_raw.reshape(H_B, SUB, n) * rinv[:, None, :]).reshape(n_rows_b, n)
        if causal:
            s = jnp.where(keep, s, NEG_INF)
        rows = pl.ds(n_rows, n_rows_b)
        alpha, pv = online(rows, s, lambda p: jnp.dot(p, latb, preferred_element_type=F32))
        accb_s[...] = alpha * accb_s[...] + pv

    for i in range(pps):
        r = pl.ds(i * PAGE, PAGE)
        for kv in range(KV_A):
            ka_s[r, kv * LANE:(kv + 1) * LANE] = ck[i][0, pl.ds(kv, PAGE, stride=KV_A), :].astype(BF16)
            va_s[r, kv * LANE:(kv + 1) * LANE] = cv[i][0, pl.ds(kv, PAGE, stride=KV_A), :].astype(BF16)
        lx_s[r, 0:2 * LANE] = cl[i][0].astype(BF16)
        kr = cr[i][0]
        lx_s[r, 2 * LANE:2 * LANE + ROPE_B] = kr.astype(BF16)
        krs_s[r, 0:ROPE_B] = (kr * kr).astype(BF16)
    process(pps * PAGE, False)

    @pl.when(c == n_chunks - 1)
    def _():
        pad = lambda x: jnp.concatenate([x, jnp.zeros((PAGE - SUB, x.shape[1]), F32)], axis=0)
        r = pl.ds(0, PAGE)
        ka_s[r, :] = pad(kan_ref[...]).astype(BF16)
        va_s[r, :] = pad(van_ref[...]).astype(BF16)
        lx_s[r, 0:2 * LANE] = pad(latn_ref[...]).astype(BF16)
        kr = pad(krn_ref[...])
        lx_s[r, 2 * LANE:2 * LANE + ROPE_B] = kr.astype(BF16)
        krs_s[r, 0:ROPE_B] = (kr * kr).astype(BF16)
        process(PAGE, True)
        lam = _diff_lambda(lam_ref)
        oa = acca_s[...] / l_s[0:n_rows, :]
        for h in range(H_A):
            d = oa[2 * h * SUB:(2 * h + 1) * SUB] - lam * oa[(2 * h + 1) * SUB:(2 * h + 2) * SUB]
            o_ref[:, h * LANE:(h + 1) * LANE] = ((d * _rms(d, 2 * DH_A)) * sub_ref[...]) * (1.0 - LAMBDA_INIT)
        ob = accb_s[...] / l_s[n_rows:n_rows + n_rows_b, :]
        for h in range(H_B):
            o_ref[:, (H_A + h) * LANE:(H_A + h + 1) * LANE] = jnp.dot(
                ob[h * SUB:(h + 1) * SUB], wuv_ref[:, h * LANE:(h + 1) * LANE], preferred_element_type=F32)


def _decode0(page_table, q_s, ka_s, va_s, lat_s, kr_s, caches, wts, gkb, lam_p, subln):
    nb, n_pages = page_table.shape
    pps = PAGES_PER_STEP
    n_chunks = n_pages // pps
    n = pps * PAGE
    ck, cv, cl, cr = caches
    wukt, wuk, wuv = wts
    rowmap = lambda b, c, pt: (b, 0)
    const = lambda a: pl.BlockSpec(a.shape, lambda b, c, pt: (0,) * a.ndim)

    def page_specs(rows, width):
        return [pl.BlockSpec((1, rows, width), functools.partial(lambda b, c, pt, i: (pt[b, c * pps + i], 0, 0), i=i))
                for i in range(pps)]

    in_specs = ([pl.BlockSpec((SUB, N_QS * LANE), rowmap), pl.BlockSpec((SUB, 2 * LANE), rowmap),
                 pl.BlockSpec((SUB, 2 * LANE), rowmap), pl.BlockSpec((SUB, 2 * LANE), rowmap),
                 pl.BlockSpec((SUB, ROPE_B), rowmap), const(wukt), const(wuk), const(wuv), const(gkb),
                 const(lam_p), const(subln)]
                + page_specs(KV_A * PAGE, LANE) + page_specs(KV_A * PAGE, LANE)
                + page_specs(PAGE, 2 * LANE) + page_specs(PAGE, ROPE_B))
    grid_spec = pltpu.PrefetchScalarGridSpec(
        num_scalar_prefetch=1,
        grid=(nb, n_chunks),
        in_specs=in_specs,
        out_specs=pl.BlockSpec((SUB, D), rowmap),
        scratch_shapes=[pltpu.VMEM((n, 2 * LANE), BF16), pltpu.VMEM((n, 2 * LANE), BF16),
                        pltpu.VMEM((n, 3 * LANE), BF16), pltpu.VMEM((n, LANE), BF16),
                        pltpu.VMEM((2 * H_A * SUB, LANE), F32), pltpu.VMEM((H_B * SUB, 3 * LANE), F32),
                        pltpu.VMEM(((2 * H_A + H_B) * SUB, 1), F32), pltpu.VMEM(((2 * H_A + H_B) * SUB, 1), F32),
                        pltpu.VMEM((2 * H_A * SUB, LANE), F32), pltpu.VMEM((H_B * SUB, 2 * LANE), F32)])
    return pl.pallas_call(
        functools.partial(_decode0_kernel, n_chunks=n_chunks),
        grid_spec=grid_spec,
        out_shape=jax.ShapeDtypeStruct((nb * SUB, D), F32),
        compiler_params=_cparams(("parallel", "arbitrary"), VMEM_BIG),
        name="decode0",
    )(page_table, q_s, ka_s, va_s, lat_s, kr_s, wukt, wuk, wuv, gkb, lam_p, subln,
      *([ck] * pps), *([cv] * pps), *([cl] * pps), *([cr] * pps))


def _proj1_post_kernel(z_ref, tk_ref, gq_ref, gk_ref, q_ref, k_ref, kf_ref, v_ref):
    j = pl.program_id(1)
    tk = tk_ref[...]
    nh = N_DIL * H_C

    @pl.when(j == 0)
    def _():
        for h in range(nh):
            x = z_ref[:, h * LANE:(h + 1) * LANE]
            y = _rope((x * _rms(x, DH_C)) * gq_ref[...], tk, DH_C // 8)
            q_ref[:, h * LANE:(h + 1) * LANE] = (y * (DH_C ** -0.5)).astype(BF16)

    @pl.when(j == 1)
    def _():
        for h in range(nh):
            x = z_ref[:, h * LANE:(h + 1) * LANE]
            y = _rope((x * _rms(x, DH_C)) * gk_ref[...], tk, DH_C // 8)
            kf_ref[:, h * LANE:(h + 1) * LANE] = y
            k_ref[:, h * LANE:(h + 1) * LANE] = y.astype(BF16)

    @pl.when(j == 2)
    def _():
        v_ref[...] = z_ref[...].astype(BF16)


def _proj1_post(z1, tk, gq, gk, *, n_rows, row0):
    tm = 256
    w = N_DIL * H_C * DH_C
    r0 = row0 // tm
    rows = lambda i, j: (i, 0)
    return pl.pallas_call(
        _proj1_post_kernel,
        grid=(n_rows // tm, 3),
        in_specs=[pl.BlockSpec((tm, w), lambda i, j: (r0 + i, j)),
                  pl.BlockSpec((tm, 3 * LANE), lambda i, j: (r0 + i, 0)),
                  pl.BlockSpec((1, LANE), lambda i, j: (0, 0)),
                  pl.BlockSpec((1, LANE), lambda i, j: (0, 0))],
        out_specs=[pl.BlockSpec((tm, w), rows)] * 4,
        out_shape=[jax.ShapeDtypeStruct((n_rows, w), BF16), jax.ShapeDtypeStruct((n_rows, w), BF16),
                   jax.ShapeDtypeStruct((n_rows, w), F32), jax.ShapeDtypeStruct((n_rows, w), BF16)],
        compiler_params=_cparams(("parallel", "arbitrary")),
        name="proj1_post",
    )(z1, tk, gq, gk)


def _dil_prompt_kernel(q_ref, kc_ref, kp_ref, vc_ref, vp_ref, o_ref, lse_ref, *, tq):
    qi = pl.program_id(2)
    nt = (((1,), (1,)), ((), ()))
    r = lax.broadcasted_iota(I32, (tq, tq), 0)
    c = lax.broadcasted_iota(I32, (tq, tq), 1)
    keep_c = c <= r
    keep_p = jnp.logical_and(c >= r, qi > 0)
    lane = lax.broadcasted_iota(I32, (tq, LANE), 1)
    lse_tile = jnp.zeros((tq, LANE), F32)
    for h in range(H_C):
        sl = slice(h * LANE, (h + 1) * LANE)
        q = q_ref[0, :, sl]
        s_c = jnp.where(keep_c, lax.dot_general(q, kc_ref[0, :, sl], nt, preferred_element_type=F32), NEG_INF)
        s_p = jnp.where(keep_p, lax.dot_general(q, kp_ref[0, :, sl], nt, preferred_element_type=F32), NEG_INF)
        m = jnp.maximum(jnp.max(s_c, axis=1, keepdims=True), jnp.max(s_p, axis=1, keepdims=True))
        p_c = jnp.exp(s_c - m)
        p_p = jnp.exp(s_p - m)
        l = jnp.sum(p_c, axis=1, keepdims=True) + jnp.sum(p_p, axis=1, keepdims=True)
        o = (jnp.dot(p_c.astype(BF16), vc_ref[0, :, sl], preferred_element_type=F32)
             + jnp.dot(p_p.astype(BF16), vp_ref[0, :, sl], preferred_element_type=F32))
        o_ref[0, :, sl] = (o / l).astype(o_ref.dtype)
        lse_tile = jnp.where(lane == h, m + jnp.log(l), lse_tile)
    lse_ref[0] = lse_tile


def _dil_prompt(q, k, v, g, dil, nb, seq):
    tq = PAGE
    w = H_C * DH_C
    ns = seq // dil
    if dil == 1:
        view = lambda a: a.reshape(nb, ns, a.shape[1])
        col = lambda r: g
    else:
        view = lambda a: a[:, g * w:(g + 1) * w].reshape(nb, ns, dil * w)
        col = lambda r: r
    cur = lambda b, r, i: (b, i, col(r))
    prev = lambda b, r, i: (b, jnp.maximum(i - 1, 0), col(r))
    blk = lambda m: pl.BlockSpec((1, tq, w), m)
    o, lse = pl.pallas_call(
        functools.partial(_dil_prompt_kernel, tq=tq),
        grid=(nb, dil, ns // tq),
        in_specs=[blk(cur), blk(cur), blk(prev), blk(cur), blk(prev)],
        out_specs=[pl.BlockSpec((1, tq, w), lambda b, r, i: (b, i, r)),
                   pl.BlockSpec((1, tq, LANE), lambda b, r, i: (b, i, r))],
        out_shape=[jax.ShapeDtypeStruct((nb, ns, dil * w), BF16), jax.ShapeDtypeStruct((nb, ns, dil * LANE), F32)],
        compiler_params=_cparams(("parallel", "parallel", "parallel")),
        name=f"dil_prompt{g}",
    )(view(q), view(k), view(k), view(v), view(v))
    return o.reshape(nb * seq, w), lse.reshape(nb * seq, LANE)


def _dil_sample_kernel(q_ref, kn_ref, vn_ref, st_ref, tail_ref, new_ref, o_ref, lse_ref, ns_ref, m_s, l_s, acc_s,
                       *, g, dil, ch, n_chunks):
    c = pl.program_id(1)
    nt = (((1,), (1,)), ((), ()))
    w = H_C * DH_C
    rpw = 2 * H_C
    body = (ch - SUB) * rpw

    ns_ref[0, 0:body] = st_ref[0, SUB * rpw:ch * rpw]

    @pl.when(c < n_chunks - 1)
    def _():
        ns_ref[0, body:ch * rpw] = tail_ref[0]

    @pl.when(c == n_chunks - 1)
    def _():
        ns_ref[0, body:ch * rpw] = new_ref[0]

    @pl.when(c == 0)
    def _():
        m_s[...] = jnp.full(m_s.shape, NEG_INF, F32)
        l_s[...] = jnp.zeros(l_s.shape, F32)
        acc_s[...] = jnp.zeros(acc_s.shape, F32)

    def update(h, s, v):
        rows = pl.ds(h * SUB, SUB)
        m_prev = m_s[rows, :]
        m_new = jnp.maximum(m_prev, jnp.max(s, axis=1, keepdims=True))
        alpha = jnp.exp(m_prev - m_new)
        p = jnp.exp(s - m_new)
        l_s[rows, :] = alpha * l_s[rows, :] + jnp.sum(p, axis=1, keepdims=True)
        acc_s[rows, :] = alpha * acc_s[rows, :] + jnp.dot(p, v, preferred_element_type=F32)
        m_s[rows, :] = m_new

    t = lax.broadcasted_iota(I32, (SUB, ch), 0)
    i = lax.broadcasted_iota(I32, (SUB, ch), 1) + c * ch
    keep = jnp.logical_and(i >= t, ((i - t) & (dil - 1)) == 0)
    for h in range(H_C):
        q = q_ref[:, g * w + h * LANE:g * w + (h + 1) * LANE]
        kk = st_ref[0, pl.ds(h, ch, stride=rpw), :]
        vv = st_ref[0, pl.ds(H_C + h, ch, stride=rpw), :]
        s = lax.dot_general(q, kk, nt, preferred_element_type=F32)
        update(h, jnp.where(keep, s, NEG_INF), vv)

    @pl.when(c == n_chunks - 1)
    def _():
        tt = lax.broadcasted_iota(I32, (SUB, PAGE), 0)
        tn = lax.broadcasted_iota(I32, (SUB, PAGE), 1)
        keep_n = jnp.logical_and(tn <= tt, ((tt - tn) & (dil - 1)) == 0)
        pad = lambda x: jnp.concatenate([x, jnp.zeros((PAGE - SUB, LANE), F32)], axis=0)
        lane = lax.broadcasted_iota(I32, (SUB, LANE), 1)
        lse_tile = jnp.zeros((SUB, LANE), F32)
        for h in range(H_C):
            sl = slice(g * w + h * LANE, g * w + (h + 1) * LANE)
            q = q_ref[:, sl]
            s = lax.dot_general(q, pad(kn_ref[:, sl]), nt, preferred_element_type=F32)
            update(h, jnp.where(keep_n, s, NEG_INF), pad(vn_ref[:, sl]))
            rows = pl.ds(h * SUB, SUB)
            l = l_s[rows, :]
            o_ref[:, h * LANE:(h + 1) * LANE] = acc_s[rows, :] / l
            lse_tile = jnp.where(lane == h, m_s[rows, :] + jnp.log(l), lse_tile)
        lse_ref[...] = lse_tile


def _dil_sample(q_s, kf_s, z1, state, new_rows, g, dil, row0):
    nb, win = state.shape[0], state.shape[1]
    w = H_C * DH_C
    rpw = 2 * H_C
    ch = min(win, 512)
    n_chunks = win // ch
    wq = N_DIL * w
    r0 = row0 // SUB
    tail_blocks = ch // SUB
    last_tail = win // SUB - 1
    o, lse, ns = pl.pallas_call(
        functools.partial(_dil_sample_kernel, g=g, dil=dil, ch=ch, n_chunks=n_chunks),
        grid=(nb, n_chunks),
        in_specs=[pl.BlockSpec((SUB, wq), lambda b, c: (b, 0)),
                  pl.BlockSpec((SUB, wq), lambda b, c: (b, 0)),
                  pl.BlockSpec((SUB, wq), lambda b, c: (r0 + b, 2)),
                  pl.BlockSpec((1, ch * rpw, LANE), lambda b, c: (b, c, 0)),
                  pl.BlockSpec((1, SUB * rpw, LANE), lambda b, c: (b, jnp.minimum((c + 1) * tail_blocks, last_tail), 0)),
                  pl.BlockSpec((1, SUB * rpw, LANE), lambda b, c: (b, 0, 0))],
        out_specs=[pl.BlockSpec((SUB, w), lambda b, c: (b, 0)), pl.BlockSpec((SUB, LANE), lambda b, c: (b, 0)),
                   pl.BlockSpec((1, ch * rpw, LANE), lambda b, c: (b, c, 0))],
        out_shape=[jax.ShapeDtypeStruct((nb * SUB, w), F32), jax.ShapeDtypeStruct((nb * SUB, LANE), F32),
                   jax.ShapeDtypeStruct((nb, win * rpw, LANE), F32)],
        scratch_shapes=[pltpu.VMEM((H_C * SUB, 1), F32), pltpu.VMEM((H_C * SUB, 1), F32),
                        pltpu.VMEM((H_C * SUB, LANE), F32)],
        compiler_params=_cparams(("parallel", "arbitrary")),
        name=f"dil_sample{g}",
    )(q_s, kf_s, z1, state.reshape(nb, win * rpw, LANE), state.reshape(nb, win * rpw, LANE),
      new_rows.reshape(nb, SUB * rpw, LANE))
    return o, lse, ns.reshape(state.shape)


def _dil_combine_kernel(o0_ref, o1_ref, o2_ref, l0_ref, l1_ref, l2_ref, a_ref):
    l0, l1, l2 = l0_ref[...], l1_ref[...], l2_ref[...]
    m = jnp.maximum(jnp.maximum(l0, l1), l2)
    w0, w1, w2 = jnp.exp(l0 - m), jnp.exp(l1 - m), jnp.exp(l2 - m)
    den = w0 + w1 + w2
    w0, w1, w2 = w0 / den, w1 / den, w2 / den
    for h in range(H_C):
        sl = slice(h * LANE, (h + 1) * LANE)
        a_ref[:, sl] = (w0[:, h:h + 1] * o0_ref[:, sl].astype(F32) + w1[:, h:h + 1] * o1_ref[:, sl].astype(F32)
                        + w2[:, h:h + 1] * o2_ref[:, sl].astype(F32)).astype(a_ref.dtype)


def _dil_combine(outs, lses, out_dtype, tm):
    n = outs[0].shape[0]
    w = H_C * DH_C
    rows = lambda i: (i, 0)
    return pl.pallas_call(
        _dil_combine_kernel,
        grid=(n // tm,),
        in_specs=[pl.BlockSpec((tm, w), rows)] * 3 + [pl.BlockSpec((tm, LANE), rows)] * 3,
        out_specs=pl.BlockSpec((tm, w), rows),
        out_shape=jax.ShapeDtypeStruct((n, w), out_dtype),
        compiler_params=_cparams(("parallel",)),
        name="dil_combine",
    )(*outs, *lses)


def _router_kernel(h_ref, r_ref, idx_ref, gate_ref):
    logits = jnp.dot(h_ref[...], r_ref[...].astype(BF16), preferred_element_type=F32)
    lane = lax.broadcasted_iota(I32, logits.shape, 1)
    lanef = lane.astype(F32)
    lg = jnp.where(lane < N_EXPERTS, logits, -jnp.inf)
    m1 = jnp.max(lg, axis=1, keepdims=True)
    i1 = jnp.min(jnp.where(lg == m1, lanef, float(LANE)), axis=1, keepdims=True)
    lg2 = jnp.where(lanef == i1, -jnp.inf, lg)
    m2 = jnp.max(lg2, axis=1, keepdims=True)
    i2 = jnp.min(jnp.where(lg2 == m2, lanef, float(LANE)), axis=1, keepdims=True)
    e = jnp.exp(m2 - m1)
    g1 = 1.0 / (1.0 + e)
    g2 = e / (1.0 + e)
    idx_ref[...] = jnp.where(lane == 0, i1, jnp.where(lane == 1, i2, 0.0)).astype(I32)
    gate_ref[...] = jnp.where(lane == 0, g1, jnp.where(lane == 1, g2, 0.0))


def _router(h, router):
    t = h.shape[0]
    tm = 512
    rp = jnp.pad(router, ((0, 0), (0, LANE - N_EXPERTS)))
    return pl.pallas_call(
        _router_kernel,
        grid=(t // tm,),
        in_specs=[pl.BlockSpec((tm, D), lambda i: (i, 0)), pl.BlockSpec((D, LANE), lambda i: (0, 0))],
        out_specs=[pl.BlockSpec((tm, LANE), lambda i: (i, 0))] * 2,
        out_shape=[jax.ShapeDtypeStruct((t, LANE), I32), jax.ShapeDtypeStruct((t, LANE), F32)],
        compiler_params=_cparams(("parallel",)),
        name="router",
    )(h, rp)


def _moe_gather_kernel(tok_ref, h_hbm, o_ref, buf, sem, *, tm):
    base = pl.program_id(0) * tm

    def issue(r, carry):
        pltpu.make_async_copy(h_hbm.at[pl.ds(tok_ref[base + r], 1)], buf.at[pl.ds(r, 1)], sem).start()
        return carry

    def wait(r, carry):
        pltpu.make_async_copy(h_hbm.at[pl.ds(0, 1)], buf.at[pl.ds(r, 1)], sem).wait()
        return carry

    lax.fori_loop(0, tm, issue, 0)
    lax.fori_loop(0, tm, wait, 0)
    o_ref[...] = buf[...].astype(BF16)


def _moe_gather(tok_of_slot, hf):
    p = tok_of_slot.shape[0]
    tm = MOE_TM
    grid_spec = pltpu.PrefetchScalarGridSpec(
        num_scalar_prefetch=1, grid=(p // tm,),
        in_specs=[pl.BlockSpec(memory_space=pl.ANY)],
        out_specs=pl.BlockSpec((tm, D), lambda i, tok: (i, 0)),
        scratch_shapes=[pltpu.VMEM((tm, D), F32), pltpu.SemaphoreType.DMA(())])
    return pl.pallas_call(
        functools.partial(_moe_gather_kernel, tm=tm),
        grid_spec=grid_spec,
        out_shape=jax.ShapeDtypeStruct((p, D), BF16),
        compiler_params=_cparams(("arbitrary",)),
        name="moe_gather",
    )(tok_of_slot, hf)


def _moe_up_kernel(te_ref, nu_ref, x_ref, wg_ref, wu_ref, o_ref, wgb_ref, wub_ref):
    i = pl.program_id(1)
    changed = jnp.logical_or(i == 0, te_ref[i] != te_ref[jnp.maximum(i - 1, 0)])

    @pl.when(changed)
    def _():
        wgb_ref[...] = wg_ref[0].astype(BF16)
        wub_ref[...] = wu_ref[0].astype(BF16)

    @pl.when(i < nu_ref[0])
    def _():
        x = x_ref[...]
        a = jnp.dot(x, wgb_ref[...], preferred_element_type=F32)
        b = jnp.dot(x, wub_ref[...], preferred_element_type=F32)
        o_ref[...] = ((a * jax.nn.sigmoid(a)) * b).astype(BF16)

    @pl.when(i >= nu_ref[0])
    def _():
        o_ref[...] = jnp.zeros(o_ref.shape, BF16)


def _moe_up(te, nu, xs, wg, wu, tn=512):
    p = xs.shape[0]
    tm = MOE_TM
    n = wg.shape[2]
    grid_spec = pltpu.PrefetchScalarGridSpec(
        num_scalar_prefetch=2, grid=(n // tn, p // tm),
        in_specs=[pl.BlockSpec((tm, D), lambda j, i, te, nu: (i, 0)),
                  pl.BlockSpec((1, D, tn), lambda j, i, te, nu: (te[i], 0, j)),
                  pl.BlockSpec((1, D, tn), lambda j, i, te, nu: (te[i], 0, j))],
        out_specs=pl.BlockSpec((tm, tn), lambda j, i, te, nu: (i, j)),
        scratch_shapes=[pltpu.VMEM((D, tn), BF16), pltpu.VMEM((D, tn), BF16)])
    return pl.pallas_call(
        _moe_up_kernel, grid_spec=grid_spec,
        out_shape=jax.ShapeDtypeStruct((p, n), BF16),
        compiler_params=_cparams(("arbitrary", "arbitrary"), VMEM_BIG),
        name="moe_up",
    )(te, nu, xs, wg, wu)


def _moe_down_kernel(te_ref, nu_ref, x_ref, w_ref, o_ref, wb_ref):
    i = pl.program_id(1)
    changed = jnp.logical_or(i == 0, te_ref[i] != te_ref[jnp.maximum(i - 1, 0)])

    @pl.when(changed)
    def _():
        wb_ref[...] = w_ref[0].astype(BF16)

    @pl.when(i < nu_ref[0])
    def _():
        o_ref[...] = jnp.dot(x_ref[...], wb_ref[...], preferred_element_type=F32)

    @pl.when(i >= nu_ref[0])
    def _():
        o_ref[...] = jnp.zeros(o_ref.shape, F32)


def _moe_down(te, nu, act, wd, tn=512):
    p, k = act.shape
    tm = MOE_TM_DOWN
    n = wd.shape[2]
    grid_spec = pltpu.PrefetchScalarGridSpec(
        num_scalar_prefetch=2, grid=(n // tn, p // tm),
        in_specs=[pl.BlockSpec((tm, k), lambda j, i, te, nu: (i, 0)),
                  pl.BlockSpec((1, k, tn), lambda j, i, te, nu: (te[i], 0, j))],
        out_specs=pl.BlockSpec((tm, tn), lambda j, i, te, nu: (i, j)),
        scratch_shapes=[pltpu.VMEM((k, tn), BF16)])
    return pl.pallas_call(
        _moe_down_kernel, grid_spec=grid_spec,
        out_shape=jax.ShapeDtypeStruct((p, n), F32),
        compiler_params=_cparams(("arbitrary", "arbitrary"), VMEM_BIG),
        name="moe_down",
    )(te, nu, act, wd)


def _moe_combine_kernel(slot_ref, ys_hbm, x_ref, gt_ref, g_ref, o_ref, buf_a, buf_b, sem, *, tm):
    base = pl.program_id(0) * tm

    def issue(r, carry):
        pltpu.make_async_copy(ys_hbm.at[pl.ds(slot_ref[2 * (base + r)], 1)], buf_a.at[pl.ds(r, 1)], sem).start()
        pltpu.make_async_copy(ys_hbm.at[pl.ds(slot_ref[2 * (base + r) + 1], 1)], buf_b.at[pl.ds(r, 1)], sem).start()
        return carry

    def wait(r, carry):
        pltpu.make_async_copy(ys_hbm.at[pl.ds(0, 1)], buf_a.at[pl.ds(r, 1)], sem).wait()
        pltpu.make_async_copy(ys_hbm.at[pl.ds(0, 1)], buf_b.at[pl.ds(r, 1)], sem).wait()
        return carry

    lax.fori_loop(0, tm, issue, 0)
    lax.fori_loop(0, tm, wait, 0)
    gt = gt_ref[...]
    y = gt[:, 0:1] * buf_a[...] + gt[:, 1:2] * buf_b[...]
    o_ref[...] = x_ref[...] + g_ref[...] * y.reshape(o_ref.shape)


def _moe_combine(slot_of_assign, ys, x, gates, mod, k_gate):
    t = x.shape[0]
    tm = 256
    gb = tm // SUB
    grid_spec = pltpu.PrefetchScalarGridSpec(
        num_scalar_prefetch=1, grid=(t // tm,),
        in_specs=[pl.BlockSpec(memory_space=pl.ANY),
                  pl.BlockSpec((gb, SUB, D), lambda i, sl: (i, 0, 0)),
                  pl.BlockSpec((tm, LANE), lambda i, sl: (i, 0)),
                  pl.BlockSpec((gb, 1, D), lambda i, sl: (i, 0, k_gate))],
        out_specs=pl.BlockSpec((gb, SUB, D), lambda i, sl: (i, 0, 0)),
        scratch_shapes=[pltpu.VMEM((tm, D), F32), pltpu.VMEM((tm, D), F32), pltpu.SemaphoreType.DMA(())])
    out = pl.pallas_call(
        functools.partial(_moe_combine_kernel, tm=tm),
        grid_spec=grid_spec,
        out_shape=jax.ShapeDtypeStruct((t // SUB, SUB, D), F32),
        compiler_params=_cparams(("arbitrary",)),
        name="moe_combine",
    )(slot_of_assign, ys, x.reshape(t // SUB, SUB, D), gates, mod)
    return out.reshape(t, D)


def _moe_plan(idx, t):
    tm = MOE_TM
    n_assign = 2 * t
    p = n_assign + N_EXPERTS * tm
    e_flat = idx[:, 0:2].reshape(n_assign)
    onehot = (e_flat[:, None] == jnp.arange(N_EXPERTS, dtype=I32)[None, :]).astype(I32)
    csum = jnp.cumsum(onehot, axis=0)
    rank = jnp.take_along_axis(csum, e_flat[:, None], axis=1)[:, 0] - 1
    counts = csum[-1]
    padded = ((counts + tm - 1) // tm) * tm
    ends = jnp.cumsum(padded)
    slot = (ends - padded)[e_flat] + rank
    tok_of_slot = jnp.zeros((p,), I32).at[slot].set(jnp.arange(n_assign, dtype=I32) // 2)
    tile_start = jnp.arange(p // tm, dtype=I32) * tm
    te = jnp.minimum(jnp.searchsorted(ends, tile_start, side="right").astype(I32), N_EXPERTS - 1)
    n_used = (ends[-1] // tm).astype(I32).reshape(1)
    return slot.astype(I32), tok_of_slot, te, n_used


def kernel(x_prompt, x_sample, c_prompt, c_sample, page_table, cache_a_k, cache_a_v, cache_b_lat, cache_b_krope,
           state_c_win0, state_c_win1, state_c_win2, ada_w, ada_b, norm_mix, norm_ffn, l0_w_in, l0_a_qnorm,
           l0_a_knorm, l0_a_lambda, l0_a_subln, l0_b_qa_norm, l0_b_w_uq, l0_b_kv_norm, l0_b_w_ukv, l0_b_qnorm,
           l0_b_knorm, l0_w_out, l0_ffn_gate, l0_ffn_up, l0_ffn_down, l1_w_in, l1_c_qnorm, l1_c_knorm, l1_w_out,
           l1_router, l1_moe_gate, l1_moe_up, l1_moe_down):
    nbp, seq, _ = x_prompt.shape
    nbs, dseq, _ = x_sample.shape
    assert dseq == SUB
    n_pages = page_table.shape[1]
    past = n_pages * PAGE
    tp, ts = nbp * seq, nbs * dseq
    t = tp + ts
    n_pool = cache_a_k.shape[0]

    nc = nbp + nbs
    ncp = -(-nc // SUB) * SUB
    c_all = jnp.pad(jnp.concatenate([c_prompt, c_sample], axis=0), ((0, ncp - nc), (0, 0)))
    mods = _adaln_all(c_all, ada_w, ada_b)
    rg_idx = jnp.concatenate([jnp.repeat(jnp.arange(nbp, dtype=I32), seq // SUB), nbp + jnp.arange(nbs, dtype=I32)])
    mod = [mods[l][rg_idx][:, None, :] for l in range(DEPTH)]

    x = jnp.concatenate([x_prompt.reshape(tp, D), x_sample.reshape(ts, D)], axis=0)
    pos = jnp.concatenate([jnp.tile(jnp.arange(seq, dtype=I32), nbp), jnp.tile(past + jnp.arange(dseq, dtype=I32), nbs)])
    tab_a = _rope_tables(pos, DH_A // 8, 0, DH_A)
    tab_k = _rope_tables(pos, ROPE_B // 2, 0, LANE)
    tab_b = _rope_tables(pos, ROPE_B // 2, NOPE_B, LANE)

    eye2 = jnp.eye(2, dtype=F32)
    w_qa = l0_w_in[:, :1024].reshape(D, H_A, 2, 1, DH_A) * eye2[None, None, :, :, None]
    w0 = jnp.concatenate([w_qa.reshape(D, 2 * H_A * LANE), l0_w_in[:, 1024:2336],
                          jnp.zeros((D, LANE - ROPE_B), F32)], axis=1)
    padl = lambda a, n: jnp.pad(a, ((0, 0),) * (a.ndim - 1) + ((0, n),))
    wuq = padl(l0_b_w_uq.reshape(Q_LORA, H_B, QK_B), LANE - QK_B).reshape(Q_LORA, H_B * LANE).astype(BF16)
    wukv = l0_b_w_ukv.reshape(KV_LORA, H_B, NOPE_B + VH_B)
    wuk_pad = padl(wukv[:, :, :NOPE_B], LANE - NOPE_B).reshape(KV_LORA, H_B * LANE).astype(BF16)
    wuk = wukv[:, :, :NOPE_B].reshape(KV_LORA, H_B * NOPE_B).astype(BF16)
    wuv = wukv[:, :, NOPE_B:].reshape(KV_LORA, H_B * VH_B).astype(BF16)
    wukt = jnp.pad(jnp.transpose(wukv[:, :, :NOPE_B], (1, 2, 0)), ((0, 0), (0, LANE - NOPE_B), (0, 0)))
    g_qa = jnp.tile(l0_a_qnorm, 2).reshape(1, LANE)
    g_ka = jnp.tile(l0_a_knorm, 2).reshape(1, LANE)
    g_qb = padl(l0_b_qnorm, LANE - QK_B).reshape(1, LANE)
    g_kb = padl(l0_b_knorm, LANE - QK_B).reshape(1, LANE)
    gains0 = (g_qa, g_ka, l0_b_qa_norm.reshape(1, Q_LORA), l0_b_kv_norm.reshape(1, KV_LORA), g_qb, g_kb)
    subln = l0_a_subln.reshape(1, LANE)

    h = _norm_mod(x, norm_mix[0], mod[0], 1, 0)
    z0 = _mm(h, w0, tm=1024, tn=384)
    tabs = (tab_a, tab_b, tab_k)
    q_p, ka_p, va_p, lat_p, kr_p, ks_p, vs_p = _proj0_post(
        z0, tabs, gains0, (wuq, wuk_pad, wuv), prompt=True, nb=nbp, seq=seq, row0=0)
    q_s, ka_s, va_s, lat_s, kr_s = _proj0_post(
        z0, tabs, gains0, (wuq, wuk_pad, wuv), prompt=False, nb=nbs, seq=dseq, row0=tp)
    o_p = _flash(q_p, ks_p, vs_p)
    a_p = _attn0_post(o_p, l0_a_lambda, subln)
    caches = (cache_a_k.reshape(n_pool, PAGE * KV_A, LANE), cache_a_v.reshape(n_pool, PAGE * KV_A, LANE),
              cache_b_lat, cache_b_krope)
    wuv_f = wukv[:, :, NOPE_B:].reshape(KV_LORA, H_B * VH_B)
    a_s = _decode0(page_table, q_s, ka_s, va_s, lat_s, kr_s, caches, (wukt, wuk, wuv_f), g_kb, l0_a_lambda, subln)
    a0 = jnp.concatenate([a_p, a_s.astype(BF16)], axis=0)
    x = _mm_res(a0, l0_w_out, x, mod[0], 2, tm=1024)
    h = _norm_mod(x, norm_ffn[0], mod[0], 4, 3)
    act = _mm_swiglu(h, l0_ffn_gate, l0_ffn_up, tm=1024)
    x = _mm_res(act, l0_ffn_down, x, mod[0], 5)

    h = _norm_mod(x, norm_mix[1], mod[1], 1, 0)
    z1 = _mm(h, l1_w_in, tm=1024, tn=512)
    gq1 = l1_c_qnorm.reshape(1, LANE)
    gk1 = l1_c_knorm.reshape(1, LANE)
    q1p, k1p, kf1p, v1p = _proj1_post(z1, tab_k, gq1, gk1, n_rows=tp, row0=0)
    q1s, _, kf1s, _ = _proj1_post(z1, tab_k, gq1, gk1, n_rows=ts, row0=tp)
    q1s_f = q1s.astype(F32)
    states = (state_c_win0, state_c_win1, state_c_win2)
    w = H_C * DH_C
    v1f = z1[:, 2 * N_DIL * w:]
    new_k = kf1s.reshape(nbs, dseq, N_DIL, H_C, DH_C)
    new_v = v1f[tp:].reshape(nbs, dseq, N_DIL, H_C, DH_C)
    outs_p, lses_p, outs_s, lses_s, win_s = [], [], [], [], []
    for g, (_, dil) in enumerate(DIL_GROUPS):
        o, lse = _dil_prompt(q1p, k1p, v1p, g, dil, nbp, seq)
        outs_p.append(o)
        lses_p.append(lse)
        new_rows = jnp.stack([new_k[:, :, g], new_v[:, :, g]], axis=2)
        o, lse, ns = _dil_sample(q1s_f, kf1s, z1, states[g], new_rows, g, dil, tp)
        outs_s.append(o)
        lses_s.append(lse)
        win_s.append(ns)
    a1 = jnp.concatenate([_dil_combine(outs_p, lses_p, BF16, 256),
                          _dil_combine(outs_s, lses_s, F32, 256).astype(BF16)], axis=0)
    x = _mm_res(a1, l1_w_out, x, mod[1], 2, tm=1024)
    h, hf = _norm_mod(x, norm_ffn[1], mod[1], 4, 3, want_f32=True)
    idx, gates = _router(h, l1_router)
    slot, tok_of_slot, te, n_used = _moe_plan(idx, t)
    xs = _moe_gather(tok_of_slot, hf)
    act = _moe_up(te, n_used, xs, l1_moe_gate, l1_moe_up)
    sub_tiles = MOE_TM // MOE_TM_DOWN
    ys = _moe_down(jnp.repeat(te, sub_tiles), n_used * sub_tiles, act, l1_moe_down)
    x = _moe_combine(slot, ys, x, gates, mod[1], 5)

    y_p = x[:tp].reshape(nbp, seq, D)
    y_s = x[tp:].reshape(nbs, dseq, D)
    win_p = []
    for g, (win, _) in enumerate(DIL_GROUPS):
        wl = min(win, seq)
        kk = kf1p.reshape(nbp, seq, N_DIL, H_C, DH_C)[:, seq - wl:, g]
        vv = v1f[:tp].reshape(nbp, seq, N_DIL, H_C, DH_C)[:, seq - wl:, g]
        win_p.append(jnp.stack([kk, vv], axis=2))
    return (y_p, y_s,
            ka_p.reshape(nbp, seq, KV_A, 2 * DH_A), ka_s.reshape(nbs, dseq, KV_A, 2 * DH_A),
            va_p.reshape(nbp, seq, KV_A, 2 * DH_A), va_s.reshape(nbs, dseq, KV_A, 2 * DH_A),
            lat_p.reshape(nbp, seq, KV_LORA), lat_s.reshape(nbs, dseq, KV_LORA),
            kr_p.reshape(nbp, seq, ROPE_B), kr_s.reshape(nbs, dseq, ROPE_B),
            win_p[0], win_s[0], win_p[1], win_s[1], win_p[2], win_s[2])
```

```python
import functools

import jax
import jax.numpy as jnp
from jax import lax
from jax.experimental import pallas as pl
from jax.experimental.pallas import tpu as pltpu

F32 = jnp.float32
BF16 = jnp.bfloat16
I32 = jnp.int32

D = 2048
DEPTH = 2
PAGE = 128
ROPE_THETA = 500000.0
EPS = 1e-6
NEG_INF = -1e30

H_A, KV_A, DH_A = 8, 2, 64
LAMBDA_INIT = 0.2
H_B, Q_LORA, KV_LORA, NOPE_B, ROPE_B, VH_B = 8, 512, 256, 64, 32, 128
QK_B = NOPE_B + ROPE_B
H_C, DH_C = 8, 128
DIL_GROUPS = ((128, 1), (512, 4), (2048, 16))
N_DIL = 3
D_FF = 5632
N_EXPERTS = 8
D_FF_E = 7168
OUT1 = H_C * DH_C

LANE = 128
SUB = 8
VMEM_BIG = 56 * 1024 * 1024

N_QS = 24
N_KS = 10
Z0_W = 27 * LANE
PAGES_PER_STEP = 16
MOE_TM = 512
MOE_TM_DOWN = 256
LOG2E = 1.4426950408889634


def _cparams(sem, vmem=None):
    return pltpu.CompilerParams(dimension_semantics=sem, vmem_limit_bytes=vmem)


def _adaln_kernel(c_ref, w_ref, b_ref, o_ref):
    c = c_ref[...]
    a = (c * jax.nn.sigmoid(c)).astype(BF16)
    o_ref[0] = jnp.dot(a, w_ref[0].astype(BF16), preferred_element_type=F32) + b_ref[0]


def _adaln_all(c_all, ada_w, ada_b):
    nb = c_all.shape[0]
    tn = 1024
    return pl.pallas_call(
        _adaln_kernel,
        grid=(DEPTH, 6 * D // tn),
        in_specs=[pl.BlockSpec((nb, D), lambda l, j: (0, 0)),
                  pl.BlockSpec((1, D, tn), lambda l, j: (l, 0, j)),
                  pl.BlockSpec((1, 1, tn), lambda l, j: (l, 0, j))],
        out_specs=pl.BlockSpec((1, nb, tn), lambda l, j: (l, 0, j)),
        out_shape=jax.ShapeDtypeStruct((DEPTH, nb, 6 * D), F32),
        compiler_params=_cparams(("parallel", "parallel")),
        name="adaln",
    )(c_all, ada_w, ada_b.reshape(DEPTH, 1, 6 * D))


def _norm_mod_kernel(x_ref, g_ref, sc_ref, sh_ref, o_ref, *of_ref):
    x = x_ref[...]
    ms = jnp.mean(x * x, axis=-1, keepdims=True)
    y = (x * lax.rsqrt(ms + EPS)) * g_ref[...]
    y = y * (1.0 + sc_ref[...]) + sh_ref[...]
    y2 = y.reshape(o_ref.shape)
    o_ref[...] = y2.astype(BF16)
    if of_ref:
        of_ref[0][...] = y2


def _norm_mod(x, gain, mod, k_sc, k_sh, want_f32=False):
    t = x.shape[0]
    gb = 32
    tm = gb * SUB
    out_shape = [jax.ShapeDtypeStruct((t, D), BF16)]
    out_specs = [pl.BlockSpec((tm, D), lambda i: (i, 0))]
    if want_f32:
        out_shape.append(jax.ShapeDtypeStruct((t, D), F32))
        out_specs.append(pl.BlockSpec((tm, D), lambda i: (i, 0)))
    res = pl.pallas_call(
        _norm_mod_kernel,
        grid=(t // tm,),
        in_specs=[pl.BlockSpec((gb, SUB, D), lambda i: (i, 0, 0)),
                  pl.BlockSpec((1, 1, D), lambda i: (0, 0, 0)),
                  pl.BlockSpec((gb, 1, D), lambda i: (i, 0, k_sc)),
                  pl.BlockSpec((gb, 1, D), lambda i: (i, 0, k_sh))],
        out_specs=out_specs,
        out_shape=out_shape,
        compiler_params=_cparams(("parallel",)),
        name="norm_mod",
    )(x.reshape(t // SUB, SUB, D), gain.reshape(1, 1, D), mod, mod)
    return res if want_f32 else res[0]


def _mm_kernel(x_ref, w_ref, o_ref, wb_ref):
    @pl.when(pl.program_id(1) == 0)
    def _():
        wb_ref[...] = w_ref[...].astype(BF16)
    o_ref[...] = jnp.dot(x_ref[...], wb_ref[...], preferred_element_type=F32).astype(o_ref.dtype)


def _mm(x, w, tm=512, tn=512, out_dtype=F32):
    m, k = x.shape
    n = w.shape[1]
    return pl.pallas_call(
        _mm_kernel,
        grid=(n // tn, m // tm),
        in_specs=[pl.BlockSpec((tm, k), lambda j, i: (i, 0)),
                  pl.BlockSpec((k, tn), lambda j, i: (0, j))],
        out_specs=pl.BlockSpec((tm, tn), lambda j, i: (i, j)),
        out_shape=jax.ShapeDtypeStruct((m, n), out_dtype),
        scratch_shapes=[pltpu.VMEM((k, tn), BF16)],
        compiler_params=_cparams(("arbitrary", "arbitrary"), VMEM_BIG),
        name="mm",
    )(x, w)


def _mm_res_kernel(x_ref, w_ref, r_ref, g_ref, o_ref, wb_ref):
    @pl.when(pl.program_id(1) == 0)
    def _():
        wb_ref[...] = w_ref[...].astype(BF16)
    acc = jnp.dot(x_ref[...], wb_ref[...], preferred_element_type=F32)
    o_ref[...] = r_ref[...] + g_ref[...] * acc.reshape(o_ref.shape)


def _mm_res(x, w, res, mod, k_gate, tm=512, tn=512):
    m, k = x.shape
    n = w.shape[1]
    gb = tm // SUB
    nj = n // tn
    out = pl.pallas_call(
        _mm_res_kernel,
        grid=(nj, m // tm),
        in_specs=[pl.BlockSpec((tm, k), lambda j, i: (i, 0)),
                  pl.BlockSpec((k, tn), lambda j, i: (0, j)),
                  pl.BlockSpec((gb, SUB, tn), lambda j, i: (i, 0, j)),
                  pl.BlockSpec((gb, 1, tn), lambda j, i: (i, 0, k_gate * nj + j))],
        out_specs=pl.BlockSpec((gb, SUB, tn), lambda j, i: (i, 0, j)),
        out_shape=jax.ShapeDtypeStruct((m // SUB, SUB, n), F32),
        scratch_shapes=[pltpu.VMEM((k, tn), BF16)],
        compiler_params=_cparams(("arbitrary", "arbitrary"), VMEM_BIG),
        name="mm_res",
    )(x, w, res.reshape(m // SUB, SUB, n), mod)
    return out.reshape(m, n)


def _mm_swiglu_kernel(x_ref, wg_ref, wu_ref, o_ref, wgb_ref, wub_ref):
    @pl.when(pl.program_id(1) == 0)
    def _():
        wgb_ref[...] = wg_ref[...].astype(BF16)
        wub_ref[...] = wu_ref[...].astype(BF16)
    x = x_ref[...]
    a = jnp.dot(x, wgb_ref[...], preferred_element_type=F32)
    b = jnp.dot(x, wub_ref[...], preferred_element_type=F32)
    o_ref[...] = ((a * jax.nn.sigmoid(a)) * b).astype(BF16)


def _mm_swiglu(x, wg, wu, tm=512, tn=512):
    m, k = x.shape
    n = wg.shape[1]
    return pl.pallas_call(
        _mm_swiglu_kernel,
        grid=(n // tn, m // tm),
        in_specs=[pl.BlockSpec((tm, k), lambda j, i: (i, 0)),
                  pl.BlockSpec((k, tn), lambda j, i: (0, j)),
                  pl.BlockSpec((k, tn), lambda j, i: (0, j))],
        out_specs=pl.BlockSpec((tm, tn), lambda j, i: (i, j)),
        out_shape=jax.ShapeDtypeStruct((m, n), BF16),
        scratch_shapes=[pltpu.VMEM((k, tn), BF16), pltpu.VMEM((k, tn), BF16)],
        compiler_params=_cparams(("arbitrary", "arbitrary"), VMEM_BIG),
        name="mm_swiglu",
    )(x, wg, wu)


def _rope_tables(pos, half, offset, period):
    inv_freq = ROPE_THETA ** (-jnp.arange(half, dtype=F32) / half)
    ang = pos.astype(F32)[:, None] * inv_freq[None, :]
    cos, sin = jnp.cos(ang), jnp.sin(ang)
    n = pos.shape[0]
    seg_c = jnp.ones((n, period), F32)
    seg_c = seg_c.at[:, offset:offset + half].set(cos).at[:, offset + half:offset + 2 * half].set(cos)
    seg_m = jnp.zeros((n, period), F32).at[:, offset:offset + half].set(-sin)
    seg_p = jnp.zeros((n, period), F32).at[:, offset + half:offset + 2 * half].set(sin)
    rep = LANE // period
    return jnp.concatenate([jnp.tile(seg_c, (1, rep)), jnp.tile(seg_m, (1, rep)), jnp.tile(seg_p, (1, rep))], axis=1)


def _rope(x, tab, half):
    c = tab[:, 0:LANE]
    sm = tab[:, LANE:2 * LANE]
    sp = tab[:, 2 * LANE:3 * LANE]
    return x * c + pltpu.roll(x, LANE - half, 1) * sm + pltpu.roll(x, half, 1) * sp


def _rms(x, n_valid):
    return lax.rsqrt(jnp.sum(x * x, axis=-1, keepdims=True) * (1.0 / n_valid) + EPS)


def _lane_tiles(x):
    return [x[:, j * LANE:(j + 1) * LANE] for j in range(x.shape[1] // LANE)]


def _fold(parts, op):
    while len(parts) > 1:
        parts = [op(parts[i], parts[i + 1]) if i + 1 < len(parts) else parts[i] for i in range(0, len(parts), 2)]
    return parts[0]


def _softmax_step(s, m_prev, exp=jnp.exp2):
    tiles = _lane_tiles(s)
    m_new = jnp.maximum(m_prev, jnp.max(_fold(tiles, jnp.maximum), axis=1, keepdims=True))
    p_tiles = [exp(t - m_new) for t in tiles]
    row_sum = jnp.sum(_fold(p_tiles, jnp.add), axis=1, keepdims=True)
    return m_new, exp(m_prev - m_new), jnp.concatenate(p_tiles, axis=1), row_sum


def _proj0_post_kernel(z_ref, ta_ref, tb_ref, tk_ref, gq_ref, gk_ref, gqa_ref, gkv_ref, gqb_ref, gkb_ref,
                       wuq_ref, wuk_ref, wuv_ref, *out_refs, prompt):
    if prompt:
        q_ref, ka_ref, va_ref, lat_ref, kr_ref, ks_ref, vs_ref = out_refs
    else:
        q_ref, ka_ref, va_ref, lat_ref, kr_ref = out_refs
    ta = ta_ref[...]
    tb = tb_ref[...]
    tk = tk_ref[...]
    lane = lax.broadcasted_iota(I32, (1, LANE), 1)
    lo = lane < DH_A

    def put_q(s, val):
        if prompt:
            q_ref[0, s] = val.astype(BF16)
        else:
            q_ref[:, s * LANE:(s + 1) * LANE] = val

    for s in range(2 * H_A):
        x = z_ref[:, s * LANE:(s + 1) * LANE]
        y = (x * _rms(x, DH_A)) * gq_ref[...]
        put_q(s, _rope(y, ta, DH_A // 8) * (DH_A ** -0.5 * LOG2E))
    for kv in range(KV_A):
        x = z_ref[:, (16 + kv) * LANE:(17 + kv) * LANE]
        xx = x * x
        s_lo = jnp.sum(jnp.where(lo, xx, 0.0), axis=-1, keepdims=True)
        s_hi = jnp.sum(jnp.where(lo, 0.0, xx), axis=-1, keepdims=True)
        inv = jnp.where(lo, lax.rsqrt(s_lo * (1.0 / DH_A) + EPS), lax.rsqrt(s_hi * (1.0 / DH_A) + EPS))
        k = _rope((x * inv) * gk_ref[...], ta, DH_A // 8)
        v = z_ref[:, (18 + kv) * LANE:(19 + kv) * LANE]
        ka_ref[:, kv * LANE:(kv + 1) * LANE] = k
        va_ref[:, kv * LANE:(kv + 1) * LANE] = v
        if prompt:
            ks_ref[0, kv] = k.astype(BF16)
            vs_ref[0, kv] = v.astype(BF16)
    qc = z_ref[:, 20 * LANE:24 * LANE]
    qcn = (qc * _rms(qc, Q_LORA)) * gqa_ref[...]
    qb = jnp.dot(qcn.astype(BF16), wuq_ref[...], preferred_element_type=F32)
    for h in range(H_B):
        x = _rope(qb[:, h * LANE:(h + 1) * LANE], tb, ROPE_B // 2)
        y = (x * _rms(x, QK_B)) * gqb_ref[...]
        put_q(2 * H_A + h, y * (QK_B ** -0.5 * LOG2E))
    kvc = z_ref[:, 24 * LANE:26 * LANE]
    lat = (kvc * _rms(kvc, KV_LORA)) * gkv_ref[...]
    lat_ref[...] = lat
    kr = _rope(z_ref[:, 26 * LANE:27 * LANE], tk, ROPE_B // 2)
    kr_ref[...] = kr[:, 0:ROPE_B]
    if prompt:
        latb = lat.astype(BF16)
        kn = jnp.dot(latb, wuk_ref[...], preferred_element_type=F32)
        vv = jnp.dot(latb, wuv_ref[...], preferred_element_type=F32)
        kr_hi = pltpu.roll(kr, NOPE_B, 1)
        for h in range(H_B):
            x = kn[:, h * LANE:(h + 1) * LANE] + kr_hi
            ks_ref[0, KV_A + h] = ((x * _rms(x, QK_B)) * gkb_ref[...]).astype(BF16)
            vs_ref[0, KV_A + h] = vv[:, h * LANE:(h + 1) * LANE].astype(BF16)


def _proj0_post(z0, tabs, gains, wts, *, prompt, nb, seq, row0):
    ta, tb, tk = tabs
    tm = 256
    n_rows = nb * seq
    nt = n_rows // tm
    r0 = row0 // tm
    if prompt:
        per_b = seq // tm
        rows = lambda i: (i, 0)
        grid = (nt,)
        zmap = lambda i: (r0 + i, 0)
        q_spec = pl.BlockSpec((1, N_QS, tm, LANE), lambda i: (i // per_b, 0, i % per_b, 0))
        kv_spec = pl.BlockSpec((1, N_KS, tm, LANE), lambda i: (i // per_b, 0, i % per_b, 0))
        out_shape = [jax.ShapeDtypeStruct((nb, N_QS, seq, LANE), BF16)]
        out_specs = [q_spec]
    else:
        rows = lambda i: (i, 0)
        grid = (nt,)
        zmap = lambda i: (r0 + i, 0)
        out_shape = [jax.ShapeDtypeStruct((n_rows, N_QS * LANE), F32)]
        out_specs = [pl.BlockSpec((tm, N_QS * LANE), rows)]
    out_shape += [jax.ShapeDtypeStruct((n_rows, 2 * LANE), F32)] * 3 + [jax.ShapeDtypeStruct((n_rows, ROPE_B), F32)]
    out_specs += [pl.BlockSpec((tm, 2 * LANE), rows)] * 3 + [pl.BlockSpec((tm, ROPE_B), rows)]
    if prompt:
        out_shape += [jax.ShapeDtypeStruct((nb, N_KS, seq, LANE), BF16)] * 2
        out_specs += [kv_spec, kv_spec]
    tab_spec = pl.BlockSpec((tm, 3 * LANE), zmap)
    const = lambda a: pl.BlockSpec(a.shape, lambda i: (0,) * a.ndim)
    return pl.pallas_call(
        functools.partial(_proj0_post_kernel, prompt=prompt),
        grid=grid,
        in_specs=[pl.BlockSpec((tm, Z0_W), zmap), tab_spec, tab_spec, tab_spec]
                 + [const(a) for a in gains] + [const(a) for a in wts],
        out_specs=out_specs,
        out_shape=out_shape,
        compiler_params=_cparams(("parallel",)),
        name="proj0_post_p" if prompt else "proj0_post_s",
    )(z0, ta, tb, tk, *gains, *wts)


def _flash_kernel(q_ref, k0_ref, v0_ref, k1_ref, v1_ref, o_ref, m_ref, l_ref, acc_ref, *, tq):
    qi = pl.program_id(2)
    kv_refs = ((k0_ref, v0_ref), (k1_ref, v1_ref))
    m_ref[...] = jnp.full(m_ref.shape, NEG_INF, F32)
    l_ref[...] = jnp.zeros(l_ref.shape, F32)
    acc_ref[...] = jnp.zeros(acc_ref.shape, F32)

    def step(ki, masked):
        start = pl.multiple_of(ki * tq, tq)
        s = jnp.concatenate(
            [lax.dot_general(q_ref[0, j], k_ref[0, 0, pl.ds(start, tq), :], (((1,), (1,)), ((), ())),
                             preferred_element_type=F32) for j, (k_ref, _) in enumerate(kv_refs)], axis=0)
        if masked:
            r = lax.broadcasted_iota(I32, (2 * tq, tq), 0) & (tq - 1)
            c = lax.broadcasted_iota(I32, (2 * tq, tq), 1)
            s = jnp.where(c <= r, s, NEG_INF)
        m_new, alpha, p, row_sum = _softmax_step(s, m_ref[...])
        l_ref[...] = alpha * l_ref[...] + row_sum
        m_ref[...] = m_new
        p = p.astype(BF16)
        pv = [jnp.dot(p[j * tq:(j + 1) * tq], v_ref[0, 0, pl.ds(start, tq), :], preferred_element_type=F32)
              for j, (_, v_ref) in enumerate(kv_refs)]
        acc_ref[...] = alpha * acc_ref[...] + jnp.concatenate(pv, axis=0)

    def body(ki, carry):
        step(ki, False)
        return carry

    lax.fori_loop(0, qi, body, 0)
    step(qi, True)
    o = acc_ref[...] / l_ref[...]
    for j in range(2):
        o_ref[0, j] = o[j * tq:(j + 1) * tq]


def _flash(q, k, v, tq=512):
    nb, _, seq, _ = q.shape

    def kv_map(j):
        def m(b, p, i):
            s = 2 * p + j
            return (b, jnp.where(s < 2 * H_A, s // (2 * H_A // KV_A), s - (2 * H_A - KV_A)), 0, 0)
        return m

    kv_spec = lambda j: pl.BlockSpec((1, 1, seq, LANE), kv_map(j))
    return pl.pallas_call(
        functools.partial(_flash_kernel, tq=tq),
        grid=(nb, N_QS // 2, seq // tq),
        in_specs=[pl.BlockSpec((1, 2, tq, LANE), lambda b, p, i: (b, p, i, 0)),
                  kv_spec(0), kv_spec(0), kv_spec(1), kv_spec(1)],
        out_specs=pl.BlockSpec((1, 2, tq, LANE), lambda b, p, i: (b, p, i, 0)),
        out_shape=jax.ShapeDtypeStruct((nb, N_QS, seq, LANE), F32),
        scratch_shapes=[pltpu.VMEM((2 * tq, LANE), F32), pltpu.VMEM((2 * tq, LANE), F32),
                        pltpu.VMEM((2 * tq, LANE), F32)],
        compiler_params=_cparams(("parallel", "parallel", "arbitrary")),
        name="flash0",
    )(q, k, v, k, v)


def _diff_lambda(lam_ref):
    lf = lam_ref[...]
    a = jnp.sum(lf[0:1] * lf[1:2], axis=-1, keepdims=True)
    b = jnp.sum(lf[2:3] * lf[3:4], axis=-1, keepdims=True)
    return jnp.exp(a) - jnp.exp(b) + LAMBDA_INIT


def _attn0_post_kernel(o_ref, lam_ref, sub_ref, a_ref):
    lam = _diff_lambda(lam_ref)
    for h in range(H_A):
        d = o_ref[0, 2 * h] - lam * o_ref[0, 2 * h + 1]
        y = ((d * _rms(d, 2 * DH_A)) * sub_ref[...]) * (1.0 - LAMBDA_INIT)
        a_ref[:, h * LANE:(h + 1) * LANE] = y.astype(BF16)
    for h in range(H_B):
        a_ref[:, (H_A + h) * LANE:(H_A + h + 1) * LANE] = o_ref[0, 2 * H_A + h].astype(BF16)


def _attn0_post(o, lam_p, subln):
    nb, _, seq, _ = o.shape
    tm = 256
    per_b = seq // tm
    return pl.pallas_call(
        _attn0_post_kernel,
        grid=(nb * per_b,),
        in_specs=[pl.BlockSpec((1, N_QS, tm, LANE), lambda i: (i // per_b, 0, i % per_b, 0)),
                  pl.BlockSpec((4, DH_A), lambda i: (0, 0)),
                  pl.BlockSpec((1, LANE), lambda i: (0, 0))],
        out_specs=pl.BlockSpec((tm, D), lambda i: (i, 0)),
        out_shape=jax.ShapeDtypeStruct((nb * seq, D), BF16),
        compiler_params=_cparams(("parallel",)),
        name="attn0_post",
    )(o, lam_p, subln)


def _decode0_kernel(pt_ref, q_ref, kan_ref, van_ref, latn_ref, krn_ref, wukt_ref, wuk_ref, wuv_ref, gkb_ref,
                    lam_ref, sub_ref, *rest, n_chunks):
    pps = PAGES_PER_STEP
    ck = rest[0:pps]
    cv = rest[pps:2 * pps]
    cl = rest[2 * pps:3 * pps]
    cr = rest[3 * pps:4 * pps]
    o_ref = rest[4 * pps]
    ka_s, va_s, lx_s, krs_s, qa_s, qx_s, m_s, l_s, acca_s, accb_s = rest[4 * pps + 1:]
    c = pl.program_id(1)
    n_rows = 2 * H_A * SUB
    n_rows_b = H_B * SUB

    @pl.when(c == 0)
    def _():
        m_s[...] = jnp.full(m_s.shape, NEG_INF, F32)
        l_s[...] = jnp.zeros(l_s.shape, F32)
        acca_s[...] = jnp.zeros(acca_s.shape, F32)
        accb_s[...] = jnp.zeros(accb_s.shape, F32)
        lx_s[:, 2 * LANE:3 * LANE] = jnp.zeros((lx_s.shape[0], LANE), BF16)
        krs_s[...] = jnp.zeros(krs_s.shape, BF16)
        lane = lax.broadcasted_iota(I32, (SUB, LANE), 1)
        for s in range(2 * H_A):
            qa_s[s * SUB:(s + 1) * SUB, :] = q_ref[:, s * LANE:(s + 1) * LANE]
        for h in range(H_B):
            qk = q_ref[:, (2 * H_A + h) * LANE:(2 * H_A + h + 1) * LANE] * gkb_ref[...]
            qx_s[h * SUB:(h + 1) * SUB, 0:2 * LANE] = jnp.dot(qk, wukt_ref[h], preferred_element_type=F32)
            qx_s[h * SUB:(h + 1) * SUB, 2 * LANE:3 * LANE] = jnp.where(lane < ROPE_B, pltpu.roll(qk, NOPE_B, 1), 0.0)

    def process(n, causal):
        nt = (((1,), (1,)), ((), ()))
        n_all = n_rows + n_rows_b
        qa = qa_s[...].astype(BF16)
        s_a = [lax.dot_general(qa[kv * n_rows_b:(kv + 1) * n_rows_b], ka_s[0:n, kv * LANE:(kv + 1) * LANE], nt,
                               preferred_element_type=F32) for kv in range(KV_A)]
        lx = lx_s[0:n, :]
        latb = lx[:, 0:2 * LANE]
        s_raw = lax.dot_general(qx_s[...].astype(BF16), lx, nt, preferred_element_type=F32)
        kn = jnp.dot(latb, wuk_ref[...], preferred_element_type=F32)
        seg = (lax.broadcasted_iota(I32, (H_B, H_B * NOPE_B), 1) // NOPE_B
               == lax.broadcasted_iota(I32, (H_B, H_B * NOPE_B), 0)).astype(BF16)
        n2 = lax.dot_general(seg, (kn * kn).astype(BF16), nt, preferred_element_type=F32)
        n2 = n2 + lax.dot_general(jnp.ones((H_B, LANE), BF16), krs_s[0:n, :], nt, preferred_element_type=F32)
        rinv = lax.rsqrt(n2 * (1.0 / QK_B) + EPS)
        s_b = (s_raw.reshape(H_B, SUB, n) * rinv[:, None, :]).reshape(n_rows_b, n)
        s = jnp.concatenate(s_a + [s_b], axis=0)
        if causal:
            row = lax.broadcasted_iota(I32, (n_all, n), 0)
            col = lax.broadcasted_iota(I32, (n_all, n), 1)
            s = jnp.where(col <= (row & (SUB - 1)), s, NEG_INF)
        m_new, alpha, p, row_sum = _softmax_step(s, m_s[...])
        l_s[...] = alpha * l_s[...] + row_sum
        m_s[...] = m_new
        p = p.astype(BF16)
        pv_a = [jnp.dot(p[kv * n_rows_b:(kv + 1) * n_rows_b], va_s[0:n, kv * LANE:(kv + 1) * LANE],
                        preferred_element_type=F32) for kv in range(KV_A)]
        acca_s[...] = alpha[0:n_rows] * acca_s[...] + jnp.concatenate(pv_a, axis=0)
        alpha_b = alpha[n_rows:n_all]
        accb_s[...] = (jnp.concatenate([alpha_b, alpha_b], axis=1) * accb_s[...]
                       + jnp.dot(p[n_rows:n_all], latb, preferred_element_type=F32))

    for i in range(pps):
        r = pl.ds(i * PAGE, PAGE)
        for kv in range(KV_A):
            ka_s[r, kv * LANE:(kv + 1) * LANE] = ck[i][0, pl.ds(kv, PAGE, stride=KV_A), :].astype(BF16)
            va_s[r, kv * LANE:(kv + 1) * LANE] = cv[i][0, pl.ds(kv, PAGE, stride=KV_A), :].astype(BF16)
        lx_s[r, 0:2 * LANE] = cl[i][0].astype(BF16)
        kr = cr[i][0]
        lx_s[r, 2 * LANE:2 * LANE + ROPE_B] = kr.astype(BF16)
        krs_s[r, 0:ROPE_B] = (kr * kr).astype(BF16)
    process(pps * PAGE, False)

    @pl.when(c == n_chunks - 1)
    def _():
        pad = lambda x: jnp.concatenate([x, jnp.zeros((PAGE - SUB, x.shape[1]), F32)], axis=0)
        r = pl.ds(0, PAGE)
        ka_s[r, :] = pad(kan_ref[...]).astype(BF16)
        va_s[r, :] = pad(van_ref[...]).astype(BF16)
        lx_s[r, 0:2 * LANE] = pad(latn_ref[...]).astype(BF16)
        kr = pad(krn_ref[...])
        lx_s[r, 2 * LANE:2 * LANE + ROPE_B] = kr.astype(BF16)
        krs_s[r, 0:ROPE_B] = (kr * kr).astype(BF16)
        process(PAGE, True)
        lam = _diff_lambda(lam_ref)
        oa = acca_s[...] / l_s[0:n_rows, :]
        for h in range(H_A):
            d = oa[2 * h * SUB:(2 * h + 1) * SUB] - lam * oa[(2 * h + 1) * SUB:(2 * h + 2) * SUB]
            o_ref[:, h * LANE:(h + 1) * LANE] = ((d * _rms(d, 2 * DH_A)) * sub_ref[...]) * (1.0 - LAMBDA_INIT)
        l_b = l_s[n_rows:n_rows + n_rows_b, :]
        ob = accb_s[...] / jnp.concatenate([l_b, l_b], axis=1)
        for h in range(H_B):
            o_ref[:, (H_A + h) * LANE:(H_A + h + 1) * LANE] = jnp.dot(
                ob[h * SUB:(h + 1) * SUB], wuv_ref[:, h * LANE:(h + 1) * LANE], preferred_element_type=F32)


def _decode0(page_table, q_s, ka_s, va_s, lat_s, kr_s, caches, wts, gkb, lam_p, subln):
    nb, n_pages = page_table.shape
    pps = PAGES_PER_STEP
    n_chunks = n_pages // pps
    n = pps * PAGE
    ck, cv, cl, cr = caches
    wukt, wuk, wuv = wts
    rowmap = lambda b, c, pt: (b, 0)
    const = lambda a: pl.BlockSpec(a.shape, lambda b, c, pt: (0,) * a.ndim)

    def page_specs(rows, width):
        return [pl.BlockSpec((1, rows, width), functools.partial(lambda b, c, pt, i: (pt[b, c * pps + i], 0, 0), i=i))
                for i in range(pps)]

    in_specs = ([pl.BlockSpec((SUB, N_QS * LANE), rowmap), pl.BlockSpec((SUB, 2 * LANE), rowmap),
                 pl.BlockSpec((SUB, 2 * LANE), rowmap), pl.BlockSpec((SUB, 2 * LANE), rowmap),
                 pl.BlockSpec((SUB, ROPE_B), rowmap), const(wukt), const(wuk), const(wuv), const(gkb),
                 const(lam_p), const(subln)]
                + page_specs(KV_A * PAGE, LANE) + page_specs(KV_A * PAGE, LANE)
                + page_specs(PAGE, 2 * LANE) + page_specs(PAGE, ROPE_B))
    grid_spec = pltpu.PrefetchScalarGridSpec(
        num_scalar_prefetch=1,
        grid=(nb, n_chunks),
        in_specs=in_specs,
        out_specs=pl.BlockSpec((SUB, D), rowmap),
        scratch_shapes=[pltpu.VMEM((n, 2 * LANE), BF16), pltpu.VMEM((n, 2 * LANE), BF16),
                        pltpu.VMEM((n, 3 * LANE), BF16), pltpu.VMEM((n, LANE), BF16),
                        pltpu.VMEM((2 * H_A * SUB, LANE), F32), pltpu.VMEM((H_B * SUB, 3 * LANE), F32),
                        pltpu.VMEM(((2 * H_A + H_B) * SUB, LANE), F32), pltpu.VMEM(((2 * H_A + H_B) * SUB, LANE), F32),
                        pltpu.VMEM((2 * H_A * SUB, LANE), F32), pltpu.VMEM((H_B * SUB, 2 * LANE), F32)])
    return pl.pallas_call(
        functools.partial(_decode0_kernel, n_chunks=n_chunks),
        grid_spec=grid_spec,
        out_shape=jax.ShapeDtypeStruct((nb * SUB, D), F32),
        compiler_params=_cparams(("parallel", "arbitrary"), VMEM_BIG),
        name="decode0",
    )(page_table, q_s, ka_s, va_s, lat_s, kr_s, wukt, wuk, wuv, gkb, lam_p, subln,
      *([ck] * pps), *([cv] * pps), *([cl] * pps), *([cr] * pps))


def _proj1_post_kernel(z_ref, tk_ref, gq_ref, gk_ref, q_ref, k_ref, kf_ref, v_ref):
    j = pl.program_id(1)
    tk = tk_ref[...]
    nh = N_DIL * H_C

    @pl.when(j == 0)
    def _():
        for h in range(nh):
            x = z_ref[:, h * LANE:(h + 1) * LANE]
            y = _rope((x * _rms(x, DH_C)) * gq_ref[...], tk, DH_C // 8)
            q_ref[:, h * LANE:(h + 1) * LANE] = (y * (DH_C ** -0.5)).astype(BF16)

    @pl.when(j == 1)
    def _():
        for h in range(nh):
            x = z_ref[:, h * LANE:(h + 1) * LANE]
            y = _rope((x * _rms(x, DH_C)) * gk_ref[...], tk, DH_C // 8)
            kf_ref[:, h * LANE:(h + 1) * LANE] = y
            k_ref[:, h * LANE:(h + 1) * LANE] = y.astype(BF16)

    @pl.when(j == 2)
    def _():
        v_ref[...] = z_ref[...].astype(BF16)


def _proj1_post(z1, tk, gq, gk, *, n_rows, row0):
    tm = 256
    w = N_DIL * H_C * DH_C
    r0 = row0 // tm
    rows = lambda i, j: (i, 0)
    return pl.pallas_call(
        _proj1_post_kernel,
        grid=(n_rows // tm, 3),
        in_specs=[pl.BlockSpec((tm, w), lambda i, j: (r0 + i, j)),
                  pl.BlockSpec((tm, 3 * LANE), lambda i, j: (r0 + i, 0)),
                  pl.BlockSpec((1, LANE), lambda i, j: (0, 0)),
                  pl.BlockSpec((1, LANE), lambda i, j: (0, 0))],
        out_specs=[pl.BlockSpec((tm, w), rows)] * 4,
        out_shape=[jax.ShapeDtypeStruct((n_rows, w), BF16), jax.ShapeDtypeStruct((n_rows, w), BF16),
                   jax.ShapeDtypeStruct((n_rows, w), F32), jax.ShapeDtypeStruct((n_rows, w), BF16)],
        compiler_params=_cparams(("parallel", "arbitrary")),
        name="proj1_post",
    )(z1, tk, gq, gk)


def _dil_prompt_kernel(q_ref, kc_ref, kp_ref, vc_ref, vp_ref, o_ref, lse_ref, *, tq):
    qi = pl.program_id(2)
    nt = (((1,), (1,)), ((), ()))
    r = lax.broadcasted_iota(I32, (tq, tq), 0)
    c = lax.broadcasted_iota(I32, (tq, tq), 1)
    keep_c = c <= r
    keep_p = jnp.logical_and(c >= r, qi > 0)
    lane = lax.broadcasted_iota(I32, (tq, LANE), 1)
    lse_tile = jnp.zeros((tq, LANE), F32)
    for h in range(H_C):
        sl = slice(h * LANE, (h + 1) * LANE)
        q = q_ref[0, :, sl]
        s_c = jnp.where(keep_c, lax.dot_general(q, kc_ref[0, :, sl], nt, preferred_element_type=F32), NEG_INF)
        s_p = jnp.where(keep_p, lax.dot_general(q, kp_ref[0, :, sl], nt, preferred_element_type=F32), NEG_INF)
        m = jnp.maximum(jnp.max(s_c, axis=1, keepdims=True), jnp.max(s_p, axis=1, keepdims=True))
        p_c = jnp.exp(s_c - m)
        p_p = jnp.exp(s_p - m)
        l = jnp.sum(p_c, axis=1, keepdims=True) + jnp.sum(p_p, axis=1, keepdims=True)
        o = (jnp.dot(p_c.astype(BF16), vc_ref[0, :, sl], preferred_element_type=F32)
             + jnp.dot(p_p.astype(BF16), vp_ref[0, :, sl], preferred_element_type=F32))
        o_ref[0, :, sl] = (o / l).astype(o_ref.dtype)
        lse_tile = jnp.where(lane == h, m + jnp.log(l), lse_tile)
    lse_ref[0] = lse_tile


def _dil_prompt(q, k, v, g, dil, nb, seq):
    tq = PAGE
    w = H_C * DH_C
    ns = seq // dil
    if dil == 1:
        view = lambda a: a.reshape(nb, ns, a.shape[1])
        col = lambda r: g
    else:
        view = lambda a: a[:, g * w:(g + 1) * w].reshape(nb, ns, dil * w)
        col = lambda r: r
    cur = lambda b, r, i: (b, i, col(r))
    prev = lambda b, r, i: (b, jnp.maximum(i - 1, 0), col(r))
    blk = lambda m: pl.BlockSpec((1, tq, w), m)
    o, lse = pl.pallas_call(
        functools.partial(_dil_prompt_kernel, tq=tq),
        grid=(nb, dil, ns // tq),
        in_specs=[blk(cur), blk(cur), blk(prev), blk(cur), blk(prev)],
        out_specs=[pl.BlockSpec((1, tq, w), lambda b, r, i: (b, i, r)),
                   pl.BlockSpec((1, tq, LANE), lambda b, r, i: (b, i, r))],
        out_shape=[jax.ShapeDtypeStruct((nb, ns, dil * w), BF16), jax.ShapeDtypeStruct((nb, ns, dil * LANE), F32)],
        compiler_params=_cparams(("parallel", "parallel", "parallel")),
        name=f"dil_prompt{g}",
    )(view(q), view(k), view(k), view(v), view(v))
    return o.reshape(nb * seq, w), lse.reshape(nb * seq, LANE)


def _dil_sample_kernel(q_ref, kn_ref, vn_ref, st_ref, tail_ref, new_ref, o_ref, lse_ref, ns_ref, m_s, l_s, acc_s,
                       *, g, dil, ch, n_chunks):
    c = pl.program_id(1)
    nt = (((1,), (1,)), ((), ()))
    w = H_C * DH_C
    rpw = 2 * H_C
    body = (ch - SUB) * rpw

    ns_ref[0, 0:body] = st_ref[0, SUB * rpw:ch * rpw]

    @pl.when(c < n_chunks - 1)
    def _():
        ns_ref[0, body:ch * rpw] = tail_ref[0]

    @pl.when(c == n_chunks - 1)
    def _():
        ns_ref[0, body:ch * rpw] = new_ref[0]

    @pl.when(c == 0)
    def _():
        m_s[...] = jnp.full(m_s.shape, NEG_INF, F32)
        l_s[...] = jnp.zeros(l_s.shape, F32)
        acc_s[...] = jnp.zeros(acc_s.shape, F32)

    n_rows = H_C * SUB
    heads = [slice(g * w + h * LANE, g * w + (h + 1) * LANE) for h in range(H_C)]

    def update(keys, vals, keep):
        s = jnp.concatenate([lax.dot_general(q_ref[:, heads[h]], keys[h], nt, preferred_element_type=F32)
                             for h in range(H_C)], axis=0)
        m_new, alpha, p, row_sum = _softmax_step(jnp.where(keep, s, NEG_INF), m_s[...], exp=jnp.exp)
        l_s[...] = alpha * l_s[...] + row_sum
        m_s[...] = m_new
        pv = [jnp.dot(p[h * SUB:(h + 1) * SUB], vals[h], preferred_element_type=F32) for h in range(H_C)]
        acc_s[...] = alpha * acc_s[...] + jnp.concatenate(pv, axis=0)

    t = lax.broadcasted_iota(I32, (n_rows, ch), 0) & (SUB - 1)
    i = lax.broadcasted_iota(I32, (n_rows, ch), 1) + c * ch
    update([st_ref[0, pl.ds(h, ch, stride=rpw), :] for h in range(H_C)],
           [st_ref[0, pl.ds(H_C + h, ch, stride=rpw), :] for h in range(H_C)],
           jnp.logical_and(i >= t, ((i - t) & (dil - 1)) == 0))

    @pl.when(c == n_chunks - 1)
    def _():
        tt = lax.broadcasted_iota(I32, (n_rows, PAGE), 0) & (SUB - 1)
        tn = lax.broadcasted_iota(I32, (n_rows, PAGE), 1)
        pad = lambda x: jnp.concatenate([x, jnp.zeros((PAGE - SUB, LANE), F32)], axis=0)
        update([pad(kn_ref[:, heads[h]]) for h in range(H_C)], [pad(vn_ref[:, heads[h]]) for h in range(H_C)],
               jnp.logical_and(tn <= tt, ((tt - tn) & (dil - 1)) == 0))
        lane = lax.broadcasted_iota(I32, (SUB, LANE), 1)
        lse_tile = jnp.zeros((SUB, LANE), F32)
        for h in range(H_C):
            rows = pl.ds(h * SUB, SUB)
            l = l_s[rows, :]
            o_ref[:, h * LANE:(h + 1) * LANE] = acc_s[rows, :] / l
            lse_tile = jnp.where(lane == h, m_s[rows, :] + jnp.log(l), lse_tile)
        lse_ref[...] = lse_tile


def _dil_sample(q_s, kf_s, z1, state, new_rows, g, dil, row0):
    nb, win = state.shape[0], state.shape[1]
    w = H_C * DH_C
    rpw = 2 * H_C
    ch = min(win, 512)
    n_chunks = win // ch
    wq = N_DIL * w
    r0 = row0 // SUB
    tail_blocks = ch // SUB
    last_tail = win // SUB - 1
    o, lse, ns = pl.pallas_call(
        functools.partial(_dil_sample_kernel, g=g, dil=dil, ch=ch, n_chunks=n_chunks),
        grid=(nb, n_chunks),
        in_specs=[pl.BlockSpec((SUB, wq), lambda b, c: (b, 0)),
                  pl.BlockSpec((SUB, wq), lambda b, c: (b, 0)),
                  pl.BlockSpec((SUB, wq), lambda b, c: (r0 + b, 2)),
                  pl.BlockSpec((1, ch * rpw, LANE), lambda b, c: (b, c, 0)),
                  pl.BlockSpec((1, SUB * rpw, LANE), lambda b, c: (b, jnp.minimum((c + 1) * tail_blocks, last_tail), 0)),
                  pl.BlockSpec((1, SUB * rpw, LANE), lambda b, c: (b, 0, 0))],
        out_specs=[pl.BlockSpec((SUB, w), lambda b, c: (b, 0)), pl.BlockSpec((SUB, LANE), lambda b, c: (b, 0)),
                   pl.BlockSpec((1, ch * rpw, LANE), lambda b, c: (b, c, 0))],
        out_shape=[jax.ShapeDtypeStruct((nb * SUB, w), F32), jax.ShapeDtypeStruct((nb * SUB, LANE), F32),
                   jax.ShapeDtypeStruct((nb, win * rpw, LANE), F32)],
        scratch_shapes=[pltpu.VMEM((H_C * SUB, LANE), F32), pltpu.VMEM((H_C * SUB, LANE), F32),
                        pltpu.VMEM((H_C * SUB, LANE), F32)],
        compiler_params=_cparams(("parallel", "arbitrary")),
        name=f"dil_sample{g}",
    )(q_s, kf_s, z1, state.reshape(nb, win * rpw, LANE), state.reshape(nb, win * rpw, LANE),
      new_rows.reshape(nb, SUB * rpw, LANE))
    return o, lse, ns.reshape(state.shape)


def _dil_combine_kernel(o0_ref, o1_ref, o2_ref, l0_ref, l1_ref, l2_ref, a_ref):
    l0, l1, l2 = l0_ref[...], l1_ref[...], l2_ref[...]
    m = jnp.maximum(jnp.maximum(l0, l1), l2)
    w0, w1, w2 = jnp.exp(l0 - m), jnp.exp(l1 - m), jnp.exp(l2 - m)
    den = w0 + w1 + w2
    w0, w1, w2 = w0 / den, w1 / den, w2 / den
    for h in range(H_C):
        sl = slice(h * LANE, (h + 1) * LANE)
        a_ref[:, sl] = (w0[:, h:h + 1] * o0_ref[:, sl].astype(F32) + w1[:, h:h + 1] * o1_ref[:, sl].astype(F32)
                        + w2[:, h:h + 1] * o2_ref[:, sl].astype(F32)).astype(a_ref.dtype)


def _dil_combine(outs, lses, out_dtype, tm):
    n = outs[0].shape[0]
    w = H_C * DH_C
    rows = lambda i: (i, 0)
    return pl.pallas_call(
        _dil_combine_kernel,
        grid=(n // tm,),
        in_specs=[pl.BlockSpec((tm, w), rows)] * 3 + [pl.BlockSpec((tm, LANE), rows)] * 3,
        out_specs=pl.BlockSpec((tm, w), rows),
        out_shape=jax.ShapeDtypeStruct((n, w), out_dtype),
        compiler_params=_cparams(("parallel",)),
        name="dil_combine",
    )(*outs, *lses)


def _router_kernel(h_ref, r_ref, idx_ref, gate_ref):
    logits = jnp.dot(h_ref[...], r_ref[...].astype(BF16), preferred_element_type=F32)
    lane = lax.broadcasted_iota(I32, logits.shape, 1)
    lanef = lane.astype(F32)
    lg = jnp.where(lane < N_EXPERTS, logits, -jnp.inf)
    m1 = jnp.max(lg, axis=1, keepdims=True)
    i1 = jnp.min(jnp.where(lg == m1, lanef, float(LANE)), axis=1, keepdims=True)
    lg2 = jnp.where(lanef == i1, -jnp.inf, lg)
    m2 = jnp.max(lg2, axis=1, keepdims=True)
    i2 = jnp.min(jnp.where(lg2 == m2, lanef, float(LANE)), axis=1, keepdims=True)
    e = jnp.exp(m2 - m1)
    g1 = 1.0 / (1.0 + e)
    g2 = e / (1.0 + e)
    idx_ref[...] = jnp.where(lane == 0, i1, jnp.where(lane == 1, i2, 0.0)).astype(I32)
    gate_ref[...] = jnp.where(lane == 0, g1, jnp.where(lane == 1, g2, 0.0))


def _router(h, router):
    t = h.shape[0]
    tm = 512
    rp = jnp.pad(router, ((0, 0), (0, LANE - N_EXPERTS)))
    return pl.pallas_call(
        _router_kernel,
        grid=(t // tm,),
        in_specs=[pl.BlockSpec((tm, D), lambda i: (i, 0)), pl.BlockSpec((D, LANE), lambda i: (0, 0))],
        out_specs=[pl.BlockSpec((tm, LANE), lambda i: (i, 0))] * 2,
        out_shape=[jax.ShapeDtypeStruct((t, LANE), I32), jax.ShapeDtypeStruct((t, LANE), F32)],
        compiler_params=_cparams(("parallel",)),
        name="router",
    )(h, rp)


def _moe_gather_kernel(tok_ref, h_hbm, o_ref, buf, sem, *, tm):
    base = pl.program_id(0) * tm

    def issue(r, carry):
        pltpu.make_async_copy(h_hbm.at[pl.ds(tok_ref[base + r], 1)], buf.at[pl.ds(r, 1)], sem).start()
        return carry

    def wait(r, carry):
        pltpu.make_async_copy(h_hbm.at[pl.ds(0, 1)], buf.at[pl.ds(r, 1)], sem).wait()
        return carry

    lax.fori_loop(0, tm, issue, 0)
    lax.fori_loop(0, tm, wait, 0)
    o_ref[...] = buf[...].astype(BF16)


def _moe_gather(tok_of_slot, hf):
    p = tok_of_slot.shape[0]
    tm = MOE_TM
    grid_spec = pltpu.PrefetchScalarGridSpec(
        num_scalar_prefetch=1, grid=(p // tm,),
        in_specs=[pl.BlockSpec(memory_space=pl.ANY)],
        out_specs=pl.BlockSpec((tm, D), lambda i, tok: (i, 0)),
        scratch_shapes=[pltpu.VMEM((tm, D), F32), pltpu.SemaphoreType.DMA(())])
    return pl.pallas_call(
        functools.partial(_moe_gather_kernel, tm=tm),
        grid_spec=grid_spec,
        out_shape=jax.ShapeDtypeStruct((p, D), BF16),
        compiler_params=_cparams(("arbitrary",)),
        name="moe_gather",
    )(tok_of_slot, hf)


def _moe_up_kernel(te_ref, nu_ref, x_ref, wg_ref, wu_ref, o_ref, wgb_ref, wub_ref):
    i = pl.program_id(1)
    changed = jnp.logical_or(i == 0, te_ref[i] != te_ref[jnp.maximum(i - 1, 0)])

    @pl.when(changed)
    def _():
        wgb_ref[...] = wg_ref[0].astype(BF16)
        wub_ref[...] = wu_ref[0].astype(BF16)

    @pl.when(i < nu_ref[0])
    def _():
        x = x_ref[...]
        a = jnp.dot(x, wgb_ref[...], preferred_element_type=F32)
        b = jnp.dot(x, wub_ref[...], preferred_element_type=F32)
        o_ref[...] = ((a * jax.nn.sigmoid(a)) * b).astype(BF16)

    @pl.when(i >= nu_ref[0])
    def _():
        o_ref[...] = jnp.zeros(o_ref.shape, BF16)


def _moe_up(te, nu, xs, wg, wu, tn=512):
    p = xs.shape[0]
    tm = MOE_TM
    n = wg.shape[2]
    grid_spec = pltpu.PrefetchScalarGridSpec(
        num_scalar_prefetch=2, grid=(n // tn, p // tm),
        in_specs=[pl.BlockSpec((tm, D), lambda j, i, te, nu: (i, 0)),
                  pl.BlockSpec((1, D, tn), lambda j, i, te, nu: (te[i], 0, j)),
                  pl.BlockSpec((1, D, tn), lambda j, i, te, nu: (te[i], 0, j))],
        out_specs=pl.BlockSpec((tm, tn), lambda j, i, te, nu: (i, j)),
        scratch_shapes=[pltpu.VMEM((D, tn), BF16), pltpu.VMEM((D, tn), BF16)])
    return pl.pallas_call(
        _moe_up_kernel, grid_spec=grid_spec,
        out_shape=jax.ShapeDtypeStruct((p, n), BF16),
        compiler_params=_cparams(("arbitrary", "arbitrary"), VMEM_BIG),
        name="moe_up",
    )(te, nu, xs, wg, wu)


def _moe_down_kernel(te_ref, nu_ref, x_ref, w_ref, o_ref, wb_ref):
    i = pl.program_id(1)
    changed = jnp.logical_or(i == 0, te_ref[i] != te_ref[jnp.maximum(i - 1, 0)])

    @pl.when(changed)
    def _():
        wb_ref[...] = w_ref[0].astype(BF16)

    @pl.when(i < nu_ref[0])
    def _():
        o_ref[...] = jnp.dot(x_ref[...], wb_ref[...], preferred_element_type=F32)

    @pl.when(i >= nu_ref[0])
    def _():
        o_ref[...] = jnp.zeros(o_ref.shape, F32)


def _moe_down(te, nu, act, wd, tn=512):
    p, k = act.shape
    tm = MOE_TM_DOWN
    n = wd.shape[2]
    grid_spec = pltpu.PrefetchScalarGridSpec(
        num_scalar_prefetch=2, grid=(n // tn, p // tm),
        in_specs=[pl.BlockSpec((tm, k), lambda j, i, te, nu: (i, 0)),
                  pl.BlockSpec((1, k, tn), lambda j, i, te, nu: (te[i], 0, j))],
        out_specs=pl.BlockSpec((tm, tn), lambda j, i, te, nu: (i, j)),
        scratch_shapes=[pltpu.VMEM((k, tn), BF16)])
    return pl.pallas_call(
        _moe_down_kernel, grid_spec=grid_spec,
        out_shape=jax.ShapeDtypeStruct((p, n), F32),
        compiler_params=_cparams(("arbitrary", "arbitrary"), VMEM_BIG),
        name="moe_down",
    )(te, nu, act, wd)


def _moe_combine_kernel(slot_ref, ys_hbm, x_ref, gt_ref, g_ref, o_ref, buf_a, buf_b, sem, *, tm):
    base = pl.program_id(0) * tm

    def issue(r, carry):
        pltpu.make_async_copy(ys_hbm.at[pl.ds(slot_ref[2 * (base + r)], 1)], buf_a.at[pl.ds(r, 1)], sem).start()
        pltpu.make_async_copy(ys_hbm.at[pl.ds(slot_ref[2 * (base + r) + 1], 1)], buf_b.at[pl.ds(r, 1)], sem).start()
        return carry

    def wait(r, carry):
        pltpu.make_async_copy(ys_hbm.at[pl.ds(0, 1)], buf_a.at[pl.ds(r, 1)], sem).wait()
        pltpu.make_async_copy(ys_hbm.at[pl.ds(0, 1)], buf_b.at[pl.ds(r, 1)], sem).wait()
        return carry

    lax.fori_loop(0, tm, issue, 0)
    lax.fori_loop(0, tm, wait, 0)
    gt = gt_ref[...]
    y = gt[:, 0:1] * buf_a[...] + gt[:, 1:2] * buf_b[...]
    o_ref[...] = x_ref[...] + g_ref[...] * y.reshape(o_ref.shape)


def _moe_combine(slot_of_assign, ys, x, gates, mod, k_gate):
    t = x.shape[0]
    tm = 256
    gb = tm // SUB
    grid_spec = pltpu.PrefetchScalarGridSpec(
        num_scalar_prefetch=1, grid=(t // tm,),
        in_specs=[pl.BlockSpec(memory_space=pl.ANY),
                  pl.BlockSpec((gb, SUB, D), lambda i, sl: (i, 0, 0)),
                  pl.BlockSpec((tm, LANE), lambda i, sl: (i, 0)),
                  pl.BlockSpec((gb, 1, D), lambda i, sl: (i, 0, k_gate))],
        out_specs=pl.BlockSpec((gb, SUB, D), lambda i, sl: (i, 0, 0)),
        scratch_shapes=[pltpu.VMEM((tm, D), F32), pltpu.VMEM((tm, D), F32), pltpu.SemaphoreType.DMA(())])
    out = pl.pallas_call(
        functools.partial(_moe_combine_kernel, tm=tm),
        grid_spec=grid_spec,
        out_shape=jax.ShapeDtypeStruct((t // SUB, SUB, D), F32),
        compiler_params=_cparams(("arbitrary",)),
        name="moe_combine",
    )(slot_of_assign, ys, x.reshape(t // SUB, SUB, D), gates, mod)
    return out.reshape(t, D)


def _moe_plan(idx, t):
    tm = MOE_TM
    n_assign = 2 * t
    p = n_assign + N_EXPERTS * tm
    e_flat = idx[:, 0:2].reshape(n_assign)
    onehot = (e_flat[:, None] == jnp.arange(N_EXPERTS, dtype=I32)[None, :]).astype(I32)
    csum = jnp.cumsum(onehot, axis=0)
    rank = jnp.take_along_axis(csum, e_flat[:, None], axis=1)[:, 0] - 1
    counts = csum[-1]
    padded = ((counts + tm - 1) // tm) * tm
    ends = jnp.cumsum(padded)
    slot = (ends - padded)[e_flat] + rank
    tok_of_slot = jnp.zeros((p,), I32).at[slot].set(jnp.arange(n_assign, dtype=I32) // 2)
    tile_start = jnp.arange(p // tm, dtype=I32) * tm
    te = jnp.minimum(jnp.searchsorted(ends, tile_start, side="right").astype(I32), N_EXPERTS - 1)
    n_used = (ends[-1] // tm).astype(I32).reshape(1)
    return slot.astype(I32), tok_of_slot, te, n_used


def kernel(x_prompt, x_sample, c_prompt, c_sample, page_table, cache_a_k, cache_a_v, cache_b_lat, cache_b_krope,
           state_c_win0, state_c_win1, state_c_win2, ada_w, ada_b, norm_mix, norm_ffn, l0_w_in, l0_a_qnorm,
           l0_a_knorm, l0_a_lambda, l0_a_subln, l0_b_qa_norm, l0_b_w_uq, l0_b_kv_norm, l0_b_w_ukv, l0_b_qnorm,
           l0_b_knorm, l0_w_out, l0_ffn_gate, l0_ffn_up, l0_ffn_down, l1_w_in, l1_c_qnorm, l1_c_knorm, l1_w_out,
           l1_router, l1_moe_gate, l1_moe_up, l1_moe_down):
    nbp, seq, _ = x_prompt.shape
    nbs, dseq, _ = x_sample.shape
    assert dseq == SUB
    n_pages = page_table.shape[1]
    past = n_pages * PAGE
    tp, ts = nbp * seq, nbs * dseq
    t = tp + ts
    n_pool = cache_a_k.shape[0]

    nc = nbp + nbs
    ncp = -(-nc // SUB) * SUB
    c_all = jnp.pad(jnp.concatenate([c_prompt, c_sample], axis=0), ((0, ncp - nc), (0, 0)))
    mods = _adaln_all(c_all, ada_w, ada_b)
    rg_idx = jnp.concatenate([jnp.repeat(jnp.arange(nbp, dtype=I32), seq // SUB), nbp + jnp.arange(nbs, dtype=I32)])
    mod = [mods[l][rg_idx][:, None, :] for l in range(DEPTH)]

    x = jnp.concatenate([x_prompt.reshape(tp, D), x_sample.reshape(ts, D)], axis=0)
    pos = jnp.concatenate([jnp.tile(jnp.arange(seq, dtype=I32), nbp), jnp.tile(past + jnp.arange(dseq, dtype=I32), nbs)])
    tab_a = _rope_tables(pos, DH_A // 8, 0, DH_A)
    tab_k = _rope_tables(pos, ROPE_B // 2, 0, LANE)
    tab_b = _rope_tables(pos, ROPE_B // 2, NOPE_B, LANE)

    eye2 = jnp.eye(2, dtype=F32)
    w_qa = l0_w_in[:, :1024].reshape(D, H_A, 2, 1, DH_A) * eye2[None, None, :, :, None]
    w0 = jnp.concatenate([w_qa.reshape(D, 2 * H_A * LANE), l0_w_in[:, 1024:2336],
                          jnp.zeros((D, LANE - ROPE_B), F32)], axis=1)
    padl = lambda a, n: jnp.pad(a, ((0, 0),) * (a.ndim - 1) + ((0, n),))
    wuq = padl(l0_b_w_uq.reshape(Q_LORA, H_B, QK_B), LANE - QK_B).reshape(Q_LORA, H_B * LANE).astype(BF16)
    wukv = l0_b_w_ukv.reshape(KV_LORA, H_B, NOPE_B + VH_B)
    wuk_pad = padl(wukv[:, :, :NOPE_B], LANE - NOPE_B).reshape(KV_LORA, H_B * LANE).astype(BF16)
    wuk = wukv[:, :, :NOPE_B].reshape(KV_LORA, H_B * NOPE_B).astype(BF16)
    wuv = wukv[:, :, NOPE_B:].reshape(KV_LORA, H_B * VH_B).astype(BF16)
    wukt = jnp.pad(jnp.transpose(wukv[:, :, :NOPE_B], (1, 2, 0)), ((0, 0), (0, LANE - NOPE_B), (0, 0)))
    g_qa = jnp.tile(l0_a_qnorm, 2).reshape(1, LANE)
    g_ka = jnp.tile(l0_a_knorm, 2).reshape(1, LANE)
    g_qb = padl(l0_b_qnorm, LANE - QK_B).reshape(1, LANE)
    g_kb = padl(l0_b_knorm, LANE - QK_B).reshape(1, LANE)
    gains0 = (g_qa, g_ka, l0_b_qa_norm.reshape(1, Q_LORA), l0_b_kv_norm.reshape(1, KV_LORA), g_qb, g_kb)
    subln = l0_a_subln.reshape(1, LANE)

    h = _norm_mod(x, norm_mix[0], mod[0], 1, 0)
    z0 = _mm(h, w0, tm=1024, tn=384)
    tabs = (tab_a, tab_b, tab_k)
    q_p, ka_p, va_p, lat_p, kr_p, ks_p, vs_p = _proj0_post(
        z0, tabs, gains0, (wuq, wuk_pad, wuv), prompt=True, nb=nbp, seq=seq, row0=0)
    q_s, ka_s, va_s, lat_s, kr_s = _proj0_post(
        z0, tabs, gains0, (wuq, wuk_pad, wuv), prompt=False, nb=nbs, seq=dseq, row0=tp)
    o_p = _flash(q_p, ks_p, vs_p)
    a_p = _attn0_post(o_p, l0_a_lambda, subln)
    caches = (cache_a_k.reshape(n_pool, PAGE * KV_A, LANE), cache_a_v.reshape(n_pool, PAGE * KV_A, LANE),
              cache_b_lat, cache_b_krope)
    wuv_f = wukv[:, :, NOPE_B:].reshape(KV_LORA, H_B * VH_B)
    a_s = _decode0(page_table, q_s, ka_s, va_s, lat_s, kr_s, caches, (wukt, wuk, wuv_f), g_kb, l0_a_lambda, subln)
    a0 = jnp.concatenate([a_p, a_s.astype(BF16)], axis=0)
    x = _mm_res(a0, l0_w_out, x, mod[0], 2, tm=1024)
    h = _norm_mod(x, norm_ffn[0], mod[0], 4, 3)
    act = _mm_swiglu(h, l0_ffn_gate, l0_ffn_up, tm=1024)
    x = _mm_res(act, l0_ffn_down, x, mod[0], 5)

    h = _norm_mod(x, norm_mix[1], mod[1], 1, 0)
    z1 = _mm(h, l1_w_in, tm=1024, tn=512)
    gq1 = l1_c_qnorm.reshape(1, LANE)
    gk1 = l1_c_knorm.reshape(1, LANE)
    q1p, k1p, kf1p, v1p = _proj1_post(z1, tab_k, gq1, gk1, n_rows=tp, row0=0)
    q1s, _, kf1s, _ = _proj1_post(z1, tab_k, gq1, gk1, n_rows=ts, row0=tp)
    q1s_f = q1s.astype(F32)
    states = (state_c_win0, state_c_win1, state_c_win2)
    w = H_C * DH_C
    v1f = z1[:, 2 * N_DIL * w:]
    new_k = kf1s.reshape(nbs, dseq, N_DIL, H_C, DH_C)
    new_v = v1f[tp:].reshape(nbs, dseq, N_DIL, H_C, DH_C)
    outs_p, lses_p, outs_s, lses_s, win_s = [], [], [], [], []
    for g, (_, dil) in enumerate(DIL_GROUPS):
        o, lse = _dil_prompt(q1p, k1p, v1p, g, dil, nbp, seq)
        outs_p.append(o)
        lses_p.append(lse)
        new_rows = jnp.stack([new_k[:, :, g], new_v[:, :, g]], axis=2)
        o, lse, ns = _dil_sample(q1s_f, kf1s, z1, states[g], new_rows, g, dil, tp)
        outs_s.append(o)
        lses_s.append(lse)
        win_s.append(ns)
    a1 = jnp.concatenate([_dil_combine(outs_p, lses_p, BF16, 256),
                          _dil_combine(outs_s, lses_s, F32, 256).astype(BF16)], axis=0)
    x = _mm_res(a1, l1_w_out, x, mod[1], 2, tm=1024)
    h, hf = _norm_mod(x, norm_ffn[1], mod[1], 4, 3, want_f32=True)
    idx, gates = _router(h, l1_router)
    slot, tok_of_slot, te, n_used = _moe_plan(idx, t)
    xs = _moe_gather(tok_of_slot, hf)
    act = _moe_up(te, n_used, xs, l1_moe_gate, l1_moe_up)
    sub_tiles = MOE_TM // MOE_TM_DOWN
    ys = _moe_down(jnp.repeat(te, sub_tiles), n_used * sub_tiles, act, l1_moe_down)
    x = _moe_combine(slot, ys, x, gates, mod[1], 5)

    y_p = x[:tp].reshape(nbp, seq, D)
    y_s = x[tp:].reshape(nbs, dseq, D)
    win_p = []
    for g, (win, _) in enumerate(DIL_GROUPS):
        wl = min(win, seq)
        kk = kf1p.reshape(nbp, seq, N_DIL, H_C, DH_C)[:, seq - wl:, g]
        vv = v1f[:tp].reshape(nbp, seq, N_DIL, H_C, DH_C)[:, seq - wl:, g]
        win_p.append(jnp.stack([kk, vv], axis=2))
    return (y_p, y_s,
            ka_p.reshape(nbp, seq, KV_A, 2 * DH_A), ka_s.reshape(nbs, dseq, KV_A, 2 * DH_A),
            va_p.reshape(nbp, seq, KV_A, 2 * DH_A), va_s.reshape(nbs, dseq, KV_A, 2 * DH_A),
            lat_p.reshape(nbp, seq, KV_LORA), lat_s.reshape(nbs, dseq, KV_LORA),
            kr_p.reshape(nbp, seq, ROPE_B), kr_s.reshape(nbs, dseq, ROPE_B),
            win_p[0], win_s[0], win_p[1], win_s[1], win_p[2], win_s[2])
```

```python
import functools

import jax
import jax.numpy as jnp
from jax import lax
from jax.experimental import pallas as pl
from jax.experimental.pallas import tpu as pltpu

F32 = jnp.float32
BF16 = jnp.bfloat16
I32 = jnp.int32

D = 2048
DEPTH = 2
PAGE = 128
ROPE_THETA = 500000.0
EPS = 1e-6
NEG_INF = -1e30

H_A, KV_A, DH_A = 8, 2, 64
LAMBDA_INIT = 0.2
H_B, Q_LORA, KV_LORA, NOPE_B, ROPE_B, VH_B = 8, 512, 256, 64, 32, 128
QK_B = NOPE_B + ROPE_B
H_C, DH_C = 8, 128
DIL_GROUPS = ((128, 1), (512, 4), (2048, 16))
N_DIL = 3
D_FF = 5632
N_EXPERTS = 8
D_FF_E = 7168
OUT1 = H_C * DH_C

LANE = 128
SUB = 8
VMEM_BIG = 56 * 1024 * 1024

N_QS = 24
N_KS = 10
Z0_W = 27 * LANE
PAGES_PER_STEP = 16
MOE_TM = 512
MOE_TM_DOWN = 256
LOG2E = 1.4426950408889634


def _cparams(sem, vmem=None):
    return pltpu.CompilerParams(dimension_semantics=sem, vmem_limit_bytes=vmem)


def _adaln_kernel(c_ref, w_ref, b_ref, o_ref):
    c = c_ref[...]
    a = (c * jax.nn.sigmoid(c)).astype(BF16)
    o_ref[0] = jnp.dot(a, w_ref[0].astype(BF16), preferred_element_type=F32) + b_ref[0]


def _adaln_all(c_all, ada_w, ada_b):
    nb = c_all.shape[0]
    tn = 1024
    return pl.pallas_call(
        _adaln_kernel,
        grid=(DEPTH, 6 * D // tn),
        in_specs=[pl.BlockSpec((nb, D), lambda l, j: (0, 0)),
                  pl.BlockSpec((1, D, tn), lambda l, j: (l, 0, j)),
                  pl.BlockSpec((1, 1, tn), lambda l, j: (l, 0, j))],
        out_specs=pl.BlockSpec((1, nb, tn), lambda l, j: (l, 0, j)),
        out_shape=jax.ShapeDtypeStruct((DEPTH, nb, 6 * D), F32),
        compiler_params=_cparams(("parallel", "parallel")),
        name="adaln",
    )(c_all, ada_w, ada_b.reshape(DEPTH, 1, 6 * D))


def _norm_mod_kernel(x_ref, g_ref, sc_ref, sh_ref, o_ref, *of_ref):
    x = x_ref[...]
    ms = jnp.mean(x * x, axis=-1, keepdims=True)
    y = (x * lax.rsqrt(ms + EPS)) * g_ref[...]
    y = y * (1.0 + sc_ref[...]) + sh_ref[...]
    y2 = y.reshape(o_ref.shape)
    o_ref[...] = y2.astype(BF16)
    if of_ref:
        of_ref[0][...] = y2


def _norm_mod(x, gain, mod, k_sc, k_sh, want_f32=False):
    t = x.shape[0]
    gb = 32
    tm = gb * SUB
    out_shape = [jax.ShapeDtypeStruct((t, D), BF16)]
    out_specs = [pl.BlockSpec((tm, D), lambda i: (i, 0))]
    if want_f32:
        out_shape.append(jax.ShapeDtypeStruct((t, D), F32))
        out_specs.append(pl.BlockSpec((tm, D), lambda i: (i, 0)))
    res = pl.pallas_call(
        _norm_mod_kernel,
        grid=(t // tm,),
        in_specs=[pl.BlockSpec((gb, SUB, D), lambda i: (i, 0, 0)),
                  pl.BlockSpec((1, 1, D), lambda i: (0, 0, 0)),
                  pl.BlockSpec((gb, 1, D), lambda i: (i, 0, k_sc)),
                  pl.BlockSpec((gb, 1, D), lambda i: (i, 0, k_sh))],
        out_specs=out_specs,
        out_shape=out_shape,
        compiler_params=_cparams(("parallel",)),
        name="norm_mod",
    )(x.reshape(t // SUB, SUB, D), gain.reshape(1, 1, D), mod, mod)
    return res if want_f32 else res[0]


def _mm_kernel(x_ref, w_ref, o_ref, wb_ref):
    @pl.when(pl.program_id(1) == 0)
    def _():
        wb_ref[...] = w_ref[...].astype(BF16)
    o_ref[...] = jnp.dot(x_ref[...], wb_ref[...], preferred_element_type=F32).astype(o_ref.dtype)


def _mm(x, w, tm=512, tn=512, out_dtype=F32):
    m, k = x.shape
    n = w.shape[1]
    return pl.pallas_call(
        _mm_kernel,
        grid=(n // tn, m // tm),
        in_specs=[pl.BlockSpec((tm, k), lambda j, i: (i, 0)),
                  pl.BlockSpec((k, tn), lambda j, i: (0, j))],
        out_specs=pl.BlockSpec((tm, tn), lambda j, i: (i, j)),
        out_shape=jax.ShapeDtypeStruct((m, n), out_dtype),
        scratch_shapes=[pltpu.VMEM((k, tn), BF16)],
        compiler_params=_cparams(("arbitrary", "arbitrary"), VMEM_BIG),
        name="mm",
    )(x, w)


def _mm_res_kernel(x_ref, w_ref, r_ref, g_ref, o_ref, wb_ref):
    @pl.when(pl.program_id(1) == 0)
    def _():
        wb_ref[...] = w_ref[...].astype(BF16)
    acc = jnp.dot(x_ref[...], wb_ref[...], preferred_element_type=F32)
    o_ref[...] = r_ref[...] + g_ref[...] * acc.reshape(o_ref.shape)


def _mm_res(x, w, res, mod, k_gate, tm=512, tn=512):
    m, k = x.shape
    n = w.shape[1]
    gb = tm // SUB
    nj = n // tn
    out = pl.pallas_call(
        _mm_res_kernel,
        grid=(nj, m // tm),
        in_specs=[pl.BlockSpec((tm, k), lambda j, i: (i, 0)),
                  pl.BlockSpec((k, tn), lambda j, i: (0, j)),
                  pl.BlockSpec((gb, SUB, tn), lambda j, i: (i, 0, j)),
                  pl.BlockSpec((gb, 1, tn), lambda j, i: (i, 0, k_gate * nj + j))],
        out_specs=pl.BlockSpec((gb, SUB, tn), lambda j, i: (i, 0, j)),
        out_shape=jax.ShapeDtypeStruct((m // SUB, SUB, n), F32),
        scratch_shapes=[pltpu.VMEM((k, tn), BF16)],
        compiler_params=_cparams(("arbitrary", "arbitrary"), VMEM_BIG),
        name="mm_res",
    )(x, w, res.reshape(m // SUB, SUB, n), mod)
    return out.reshape(m, n)


def _mm_swiglu_kernel(x_ref, wg_ref, wu_ref, o_ref, wgb_ref, wub_ref):
    @pl.when(pl.program_id(1) == 0)
    def _():
        wgb_ref[...] = wg_ref[...].astype(BF16)
        wub_ref[...] = wu_ref[...].astype(BF16)
    x = x_ref[...]
    a = jnp.dot(x, wgb_ref[...], preferred_element_type=F32)
    b = jnp.dot(x, wub_ref[...], preferred_element_type=F32)
    o_ref[...] = ((a * jax.nn.sigmoid(a)) * b).astype(BF16)


def _mm_swiglu(x, wg, wu, tm=512, tn=512):
    m, k = x.shape
    n = wg.shape[1]
    return pl.pallas_call(
        _mm_swiglu_kernel,
        grid=(n // tn, m // tm),
        in_specs=[pl.BlockSpec((tm, k), lambda j, i: (i, 0)),
                  pl.BlockSpec((k, tn), lambda j, i: (0, j)),
                  pl.BlockSpec((k, tn), lambda j, i: (0, j))],
        out_specs=pl.BlockSpec((tm, tn), lambda j, i: (i, j)),
        out_shape=jax.ShapeDtypeStruct((m, n), BF16),
        scratch_shapes=[pltpu.VMEM((k, tn), BF16), pltpu.VMEM((k, tn), BF16)],
        compiler_params=_cparams(("arbitrary", "arbitrary"), VMEM_BIG),
        name="mm_swiglu",
    )(x, wg, wu)


def _rope_tables(pos, half, offset, period):
    inv_freq = ROPE_THETA ** (-jnp.arange(half, dtype=F32) / half)
    ang = pos.astype(F32)[:, None] * inv_freq[None, :]
    cos, sin = jnp.cos(ang), jnp.sin(ang)
    n = pos.shape[0]
    seg_c = jnp.ones((n, period), F32)
    seg_c = seg_c.at[:, offset:offset + half].set(cos).at[:, offset + half:offset + 2 * half].set(cos)
    seg_m = jnp.zeros((n, period), F32).at[:, offset:offset + half].set(-sin)
    seg_p = jnp.zeros((n, period), F32).at[:, offset + half:offset + 2 * half].set(sin)
    rep = LANE // period
    return jnp.concatenate([jnp.tile(seg_c, (1, rep)), jnp.tile(seg_m, (1, rep)), jnp.tile(seg_p, (1, rep))], axis=1)


def _rope(x, tab, half):
    c = tab[:, 0:LANE]
    sm = tab[:, LANE:2 * LANE]
    sp = tab[:, 2 * LANE:3 * LANE]
    return x * c + pltpu.roll(x, LANE - half, 1) * sm + pltpu.roll(x, half, 1) * sp


def _rms(x, n_valid):
    return lax.rsqrt(jnp.sum(x * x, axis=-1, keepdims=True) * (1.0 / n_valid) + EPS)


def _lane_tiles(x):
    return [x[:, j * LANE:(j + 1) * LANE] for j in range(x.shape[1] // LANE)]


def _fold(parts, op):
    while len(parts) > 1:
        parts = [op(parts[i], parts[i + 1]) if i + 1 < len(parts) else parts[i] for i in range(0, len(parts), 2)]
    return parts[0]


def _softmax_step(s, m_prev, exp=jnp.exp2):
    tiles = _lane_tiles(s)
    m_new = jnp.maximum(m_prev, jnp.max(_fold(tiles, jnp.maximum), axis=1, keepdims=True))
    p_tiles = [exp(t - m_new) for t in tiles]
    row_sum = jnp.sum(_fold(p_tiles, jnp.add), axis=1, keepdims=True)
    return m_new, exp(m_prev - m_new), jnp.concatenate(p_tiles, axis=1), row_sum


def _proj0_post_kernel(z_ref, ta_ref, tb_ref, tk_ref, gq_ref, gk_ref, gqa_ref, gkv_ref, gqb_ref, gkb_ref,
                       wuq_ref, wuk_ref, wuv_ref, *out_refs, prompt):
    if prompt:
        q_ref, ka_ref, va_ref, lat_ref, kr_ref, ks_ref, vs_ref = out_refs
    else:
        q_ref, ka_ref, va_ref, lat_ref, kr_ref = out_refs
    ta = ta_ref[...]
    tb = tb_ref[...]
    tk = tk_ref[...]
    lane = lax.broadcasted_iota(I32, (1, LANE), 1)
    lo = lane < DH_A

    def put_q(s, val):
        if prompt:
            q_ref[0, s] = val.astype(BF16)
        else:
            q_ref[:, s * LANE:(s + 1) * LANE] = val

    for s in range(2 * H_A):
        x = z_ref[:, s * LANE:(s + 1) * LANE]
        y = (x * _rms(x, DH_A)) * gq_ref[...]
        put_q(s, _rope(y, ta, DH_A // 8) * (DH_A ** -0.5 * LOG2E))
    for kv in range(KV_A):
        x = z_ref[:, (16 + kv) * LANE:(17 + kv) * LANE]
        xx = x * x
        s_lo = jnp.sum(jnp.where(lo, xx, 0.0), axis=-1, keepdims=True)
        s_hi = jnp.sum(jnp.where(lo, 0.0, xx), axis=-1, keepdims=True)
        inv = jnp.where(lo, lax.rsqrt(s_lo * (1.0 / DH_A) + EPS), lax.rsqrt(s_hi * (1.0 / DH_A) + EPS))
        k = _rope((x * inv) * gk_ref[...], ta, DH_A // 8)
        v = z_ref[:, (18 + kv) * LANE:(19 + kv) * LANE]
        ka_ref[:, kv * LANE:(kv + 1) * LANE] = k
        va_ref[:, kv * LANE:(kv + 1) * LANE] = v
        if prompt:
            ks_ref[0, kv] = k.astype(BF16)
            vs_ref[0, kv] = v.astype(BF16)
    qc = z_ref[:, 20 * LANE:24 * LANE]
    qcn = (qc * _rms(qc, Q_LORA)) * gqa_ref[...]
    qb = jnp.dot(qcn.astype(BF16), wuq_ref[...], preferred_element_type=F32)
    for h in range(H_B):
        x = _rope(qb[:, h * LANE:(h + 1) * LANE], tb, ROPE_B // 2)
        y = (x * _rms(x, QK_B)) * gqb_ref[...]
        put_q(2 * H_A + h, y * (QK_B ** -0.5 * LOG2E))
    kvc = z_ref[:, 24 * LANE:26 * LANE]
    lat = (kvc * _rms(kvc, KV_LORA)) * gkv_ref[...]
    lat_ref[...] = lat
    kr = _rope(z_ref[:, 26 * LANE:27 * LANE], tk, ROPE_B // 2)
    kr_ref[...] = kr[:, 0:ROPE_B]
    if prompt:
        latb = lat.astype(BF16)
        kn = jnp.dot(latb, wuk_ref[...], preferred_element_type=F32)
        vv = jnp.dot(latb, wuv_ref[...], preferred_element_type=F32)
        kr_hi = pltpu.roll(kr, NOPE_B, 1)
        for h in range(H_B):
            x = kn[:, h * LANE:(h + 1) * LANE] + kr_hi
            ks_ref[0, KV_A + h] = ((x * _rms(x, QK_B)) * gkb_ref[...]).astype(BF16)
            vs_ref[0, KV_A + h] = vv[:, h * LANE:(h + 1) * LANE].astype(BF16)


def _proj0_post(z0, tabs, gains, wts, *, prompt, nb, seq, row0):
    ta, tb, tk = tabs
    tm = 256
    n_rows = nb * seq
    nt = n_rows // tm
    r0 = row0 // tm
    if prompt:
        per_b = seq // tm
        rows = lambda i: (i, 0)
        grid = (nt,)
        zmap = lambda i: (r0 + i, 0)
        q_spec = pl.BlockSpec((1, N_QS, tm, LANE), lambda i: (i // per_b, 0, i % per_b, 0))
        kv_spec = pl.BlockSpec((1, N_KS, tm, LANE), lambda i: (i // per_b, 0, i % per_b, 0))
        out_shape = [jax.ShapeDtypeStruct((nb, N_QS, seq, LANE), BF16)]
        out_specs = [q_spec]
    else:
        rows = lambda i: (i, 0)
        grid = (nt,)
        zmap = lambda i: (r0 + i, 0)
        out_shape = [jax.ShapeDtypeStruct((n_rows, N_QS * LANE), F32)]
        out_specs = [pl.BlockSpec((tm, N_QS * LANE), rows)]
    out_shape += [jax.ShapeDtypeStruct((n_rows, 2 * LANE), F32)] * 3 + [jax.ShapeDtypeStruct((n_rows, ROPE_B), F32)]
    out_specs += [pl.BlockSpec((tm, 2 * LANE), rows)] * 3 + [pl.BlockSpec((tm, ROPE_B), rows)]
    if prompt:
        out_shape += [jax.ShapeDtypeStruct((nb, N_KS, seq, LANE), BF16)] * 2
        out_specs += [kv_spec, kv_spec]
    tab_spec = pl.BlockSpec((tm, 3 * LANE), zmap)
    const = lambda a: pl.BlockSpec(a.shape, lambda i: (0,) * a.ndim)
    return pl.pallas_call(
        functools.partial(_proj0_post_kernel, prompt=prompt),
        grid=grid,
        in_specs=[pl.BlockSpec((tm, Z0_W), zmap), tab_spec, tab_spec, tab_spec]
                 + [const(a) for a in gains] + [const(a) for a in wts],
        out_specs=out_specs,
        out_shape=out_shape,
        compiler_params=_cparams(("parallel",)),
        name="proj0_post_p" if prompt else "proj0_post_s",
    )(z0, ta, tb, tk, *gains, *wts)


def _flash_kernel(q_ref, k0_ref, v0_ref, k1_ref, v1_ref, o_ref, m_ref, l_ref, acc_ref, *, tq):
    qi = pl.program_id(2)
    kv_refs = ((k0_ref, v0_ref), (k1_ref, v1_ref))
    m_ref[...] = jnp.full(m_ref.shape, NEG_INF, F32)
    l_ref[...] = jnp.zeros(l_ref.shape, F32)
    acc_ref[...] = jnp.zeros(acc_ref.shape, F32)

    def step(ki, masked):
        start = pl.multiple_of(ki * tq, tq)
        s = jnp.concatenate(
            [lax.dot_general(q_ref[0, j], k_ref[0, 0, pl.ds(start, tq), :], (((1,), (1,)), ((), ())),
                             preferred_element_type=F32) for j, (k_ref, _) in enumerate(kv_refs)], axis=0)
        if masked:
            r = lax.broadcasted_iota(I32, (2 * tq, tq), 0) & (tq - 1)
            c = lax.broadcasted_iota(I32, (2 * tq, tq), 1)
            s = jnp.where(c <= r, s, NEG_INF)
        m_new, alpha, p, row_sum = _softmax_step(s, m_ref[...])
        l_ref[...] = alpha * l_ref[...] + row_sum
        m_ref[...] = m_new
        p = p.astype(BF16)
        pv = [jnp.dot(p[j * tq:(j + 1) * tq], v_ref[0, 0, pl.ds(start, tq), :], preferred_element_type=F32)
              for j, (_, v_ref) in enumerate(kv_refs)]
        acc_ref[...] = alpha * acc_ref[...] + jnp.concatenate(pv, axis=0)

    def body(ki, carry):
        step(ki, False)
        return carry

    lax.fori_loop(0, qi, body, 0)
    step(qi, True)
    o = acc_ref[...] / l_ref[...]
    for j in range(2):
        o_ref[0, j] = o[j * tq:(j + 1) * tq]


def _flash(q, k, v, tq=512):
    nb, _, seq, _ = q.shape

    def kv_map(j):
        def m(b, p, i):
            s = 2 * p + j
            return (b, jnp.where(s < 2 * H_A, s // (2 * H_A // KV_A), s - (2 * H_A - KV_A)), 0, 0)
        return m

    kv_spec = lambda j: pl.BlockSpec((1, 1, seq, LANE), kv_map(j))
    return pl.pallas_call(
        functools.partial(_flash_kernel, tq=tq),
        grid=(nb, N_QS // 2, seq // tq),
        in_specs=[pl.BlockSpec((1, 2, tq, LANE), lambda b, p, i: (b, p, i, 0)),
                  kv_spec(0), kv_spec(0), kv_spec(1), kv_spec(1)],
        out_specs=pl.BlockSpec((1, 2, tq, LANE), lambda b, p, i: (b, p, i, 0)),
        out_shape=jax.ShapeDtypeStruct((nb, N_QS, seq, LANE), F32),
        scratch_shapes=[pltpu.VMEM((2 * tq, LANE), F32), pltpu.VMEM((2 * tq, LANE), F32),
                        pltpu.VMEM((2 * tq, LANE), F32)],
        compiler_params=_cparams(("parallel", "parallel", "arbitrary")),
        name="flash0",
    )(q, k, v, k, v)


def _diff_lambda(lam_ref):
    lf = lam_ref[...]
    a = jnp.sum(lf[0:1] * lf[1:2], axis=-1, keepdims=True)
    b = jnp.sum(lf[2:3] * lf[3:4], axis=-1, keepdims=True)
    return jnp.exp(a) - jnp.exp(b) + LAMBDA_INIT


def _attn0_post_kernel(o_ref, lam_ref, sub_ref, a_ref):
    lam = _diff_lambda(lam_ref)
    for h in range(H_A):
        d = o_ref[0, 2 * h] - lam * o_ref[0, 2 * h + 1]
        y = ((d * _rms(d, 2 * DH_A)) * sub_ref[...]) * (1.0 - LAMBDA_INIT)
        a_ref[:, h * LANE:(h + 1) * LANE] = y.astype(BF16)
    for h in range(H_B):
        a_ref[:, (H_A + h) * LANE:(H_A + h + 1) * LANE] = o_ref[0, 2 * H_A + h].astype(BF16)


def _attn0_post(o, lam_p, subln):
    nb, _, seq, _ = o.shape
    tm = 256
    per_b = seq // tm
    return pl.pallas_call(
        _attn0_post_kernel,
        grid=(nb * per_b,),
        in_specs=[pl.BlockSpec((1, N_QS, tm, LANE), lambda i: (i // per_b, 0, i % per_b, 0)),
                  pl.BlockSpec((4, DH_A), lambda i: (0, 0)),
                  pl.BlockSpec((1, LANE), lambda i: (0, 0))],
        out_specs=pl.BlockSpec((tm, D), lambda i: (i, 0)),
        out_shape=jax.ShapeDtypeStruct((nb * seq, D), BF16),
        compiler_params=_cparams(("parallel",)),
        name="attn0_post",
    )(o, lam_p, subln)


def _decode0_kernel(pt_ref, q_ref, kan_ref, van_ref, latn_ref, krn_ref, wukt_ref, wuk_ref, wuv_ref, gkb_ref,
                    lam_ref, sub_ref, *rest, n_chunks):
    pps = PAGES_PER_STEP
    ck = rest[0:pps]
    cv = rest[pps:2 * pps]
    cl = rest[2 * pps:3 * pps]
    cr = rest[3 * pps:4 * pps]
    o_ref = rest[4 * pps]
    ka_s, va_s, lat_s, krt_s, krq_s, qa_s, qx_s, m_s, l_s, acca_s, accb_s = rest[4 * pps + 1:]
    c = pl.program_id(1)
    n_rows = 2 * H_A * SUB
    n_rows_b = H_B * SUB

    @pl.when(c == 0)
    def _():
        m_s[...] = jnp.full(m_s.shape, NEG_INF, F32)
        l_s[...] = jnp.zeros(l_s.shape, F32)
        acca_s[...] = jnp.zeros(acca_s.shape, F32)
        accb_s[...] = jnp.zeros(accb_s.shape, F32)
        lane = lax.broadcasted_iota(I32, (SUB, LANE), 1)
        for s in range(2 * H_A):
            qa_s[s * SUB:(s + 1) * SUB, :] = q_ref[:, s * LANE:(s + 1) * LANE]
        for h in range(H_B):
            qk = q_ref[:, (2 * H_A + h) * LANE:(2 * H_A + h + 1) * LANE] * gkb_ref[...]
            qx_s[h * SUB:(h + 1) * SUB, 0:2 * LANE] = jnp.dot(qk, wukt_ref[h], preferred_element_type=F32)
            qx_s[h * SUB:(h + 1) * SUB, 2 * LANE:3 * LANE] = jnp.where(lane < ROPE_B, pltpu.roll(qk, NOPE_B, 1), 0.0)

    def process(n, causal):
        nt = (((1,), (1,)), ((), ()))
        n_all = n_rows + n_rows_b
        qa = qa_s[...].astype(BF16)
        s_a = [lax.dot_general(qa[kv * n_rows_b:(kv + 1) * n_rows_b], ka_s[0:n, kv * LANE:(kv + 1) * LANE], nt,
                               preferred_element_type=F32) for kv in range(KV_A)]
        latb = lat_s[0:n, :]
        qx = qx_s[...].astype(BF16)
        s_raw = (lax.dot_general(qx[:, 0:2 * LANE], latb, nt, preferred_element_type=F32)
                 + jnp.dot(qx[:, 2 * LANE:2 * LANE + ROPE_B], krt_s[:, 0:n], preferred_element_type=F32))
        kn = jnp.dot(latb, wuk_ref[...], preferred_element_type=F32)
        seg = (lax.broadcasted_iota(I32, (H_B, H_B * NOPE_B), 1) // NOPE_B
               == lax.broadcasted_iota(I32, (H_B, H_B * NOPE_B), 0)).astype(BF16)
        n2 = lax.dot_general(seg, (kn * kn).astype(BF16), nt, preferred_element_type=F32)
        n2 = n2 + jnp.dot(jnp.ones((H_B, ROPE_B), BF16), krq_s[:, 0:n], preferred_element_type=F32)
        rinv = lax.rsqrt(n2 * (1.0 / QK_B) + EPS)
        s_b = (s_raw.reshape(H_B, SUB, n) * rinv[:, None, :]).reshape(n_rows_b, n)
        s = jnp.concatenate(s_a + [s_b], axis=0)
        if causal:
            row = lax.broadcasted_iota(I32, (n_all, n), 0)
            col = lax.broadcasted_iota(I32, (n_all, n), 1)
            s = jnp.where(col <= (row & (SUB - 1)), s, NEG_INF)
        m_new, alpha, p, row_sum = _softmax_step(s, m_s[...])
        l_s[...] = alpha * l_s[...] + row_sum
        m_s[...] = m_new
        p = p.astype(BF16)
        pv_a = [jnp.dot(p[kv * n_rows_b:(kv + 1) * n_rows_b], va_s[0:n, kv * LANE:(kv + 1) * LANE],
                        preferred_element_type=F32) for kv in range(KV_A)]
        acca_s[...] = alpha[0:n_rows] * acca_s[...] + jnp.concatenate(pv_a, axis=0)
        alpha_b = alpha[n_rows:n_all]
        accb_s[...] = (jnp.concatenate([alpha_b, alpha_b], axis=1) * accb_s[...]
                       + jnp.dot(p[n_rows:n_all], latb, preferred_element_type=F32))

    for i in range(pps):
        r = pl.ds(i * PAGE, PAGE)
        for kv in range(KV_A):
            ka_s[r, kv * LANE:(kv + 1) * LANE] = ck[i][0, pl.ds(kv, PAGE, stride=KV_A), :].astype(BF16)
            va_s[r, kv * LANE:(kv + 1) * LANE] = cv[i][0, pl.ds(kv, PAGE, stride=KV_A), :].astype(BF16)
        lat_s[r, :] = cl[i][0].astype(BF16)
        kr = cr[i][0]
        krt_s[:, i * PAGE:(i + 1) * PAGE] = kr.astype(BF16)
        krq_s[:, i * PAGE:(i + 1) * PAGE] = (kr * kr).astype(BF16)
    process(pps * PAGE, False)

    @pl.when(c == n_chunks - 1)
    def _():
        pad = lambda x: jnp.concatenate([x, jnp.zeros((PAGE - SUB, x.shape[1]), F32)], axis=0)
        r = pl.ds(0, PAGE)
        ka_s[r, :] = pad(kan_ref[...]).astype(BF16)
        va_s[r, :] = pad(van_ref[...]).astype(BF16)
        lat_s[r, :] = pad(latn_ref[...]).astype(BF16)
        kr = krn_ref[0]
        krt_s[:, 0:PAGE] = kr.astype(BF16)
        krq_s[:, 0:PAGE] = (kr * kr).astype(BF16)
        process(PAGE, True)
        lam = _diff_lambda(lam_ref)
        oa = acca_s[...] / l_s[0:n_rows, :]
        for h in range(H_A):
            d = oa[2 * h * SUB:(2 * h + 1) * SUB] - lam * oa[(2 * h + 1) * SUB:(2 * h + 2) * SUB]
            o_ref[:, h * LANE:(h + 1) * LANE] = ((d * _rms(d, 2 * DH_A)) * sub_ref[...]) * (1.0 - LAMBDA_INIT)
        l_b = l_s[n_rows:n_rows + n_rows_b, :]
        ob = accb_s[...] / jnp.concatenate([l_b, l_b], axis=1)
        for h in range(H_B):
            o_ref[:, (H_A + h) * LANE:(H_A + h + 1) * LANE] = jnp.dot(
                ob[h * SUB:(h + 1) * SUB], wuv_ref[:, h * LANE:(h + 1) * LANE], preferred_element_type=F32)


def _decode0(page_table, q_s, ka_s, va_s, lat_s, kr_s, caches, wts, gkb, lam_p, subln):
    nb, n_pages = page_table.shape
    pps = PAGES_PER_STEP
    n_chunks = n_pages // pps
    n = pps * PAGE
    ck, cv, cl, cr = caches
    wukt, wuk, wuv = wts
    rowmap = lambda b, c, pt: (b, 0)
    const = lambda a: pl.BlockSpec(a.shape, lambda b, c, pt: (0,) * a.ndim)

    def page_specs(rows, width):
        return [pl.BlockSpec((1, rows, width), functools.partial(lambda b, c, pt, i: (pt[b, c * pps + i], 0, 0), i=i))
                for i in range(pps)]

    in_specs = ([pl.BlockSpec((SUB, N_QS * LANE), rowmap), pl.BlockSpec((SUB, 2 * LANE), rowmap),
                 pl.BlockSpec((SUB, 2 * LANE), rowmap), pl.BlockSpec((SUB, 2 * LANE), rowmap),
                 pl.BlockSpec((1, ROPE_B, PAGE), lambda b, c, pt: (b, 0, 0)), const(wukt), const(wuk), const(wuv),
                 const(gkb), const(lam_p), const(subln)]
                + page_specs(KV_A * PAGE, LANE) + page_specs(KV_A * PAGE, LANE)
                + page_specs(PAGE, 2 * LANE) + page_specs(ROPE_B, PAGE))
    grid_spec = pltpu.PrefetchScalarGridSpec(
        num_scalar_prefetch=1,
        grid=(nb, n_chunks),
        in_specs=in_specs,
        out_specs=pl.BlockSpec((SUB, D), rowmap),
        scratch_shapes=[pltpu.VMEM((n, 2 * LANE), BF16), pltpu.VMEM((n, 2 * LANE), BF16),
                        pltpu.VMEM((n, 2 * LANE), BF16), pltpu.VMEM((ROPE_B, n), BF16), pltpu.VMEM((ROPE_B, n), BF16),
                        pltpu.VMEM((2 * H_A * SUB, LANE), F32), pltpu.VMEM((H_B * SUB, 3 * LANE), F32),
                        pltpu.VMEM(((2 * H_A + H_B) * SUB, LANE), F32), pltpu.VMEM(((2 * H_A + H_B) * SUB, LANE), F32),
                        pltpu.VMEM((2 * H_A * SUB, LANE), F32), pltpu.VMEM((H_B * SUB, 2 * LANE), F32)])
    return pl.pallas_call(
        functools.partial(_decode0_kernel, n_chunks=n_chunks),
        grid_spec=grid_spec,
        out_shape=jax.ShapeDtypeStruct((nb * SUB, D), F32),
        compiler_params=_cparams(("parallel", "arbitrary"), VMEM_BIG),
        name="decode0",
    )(page_table, q_s, ka_s, va_s, lat_s, kr_s, wukt, wuk, wuv, gkb, lam_p, subln,
      *([ck] * pps), *([cv] * pps), *([cl] * pps), *([cr] * pps))


def _proj1_post_kernel(z_ref, tk_ref, gq_ref, gk_ref, q_ref, k_ref, kf_ref, v_ref):
    j = pl.program_id(1)
    tk = tk_ref[...]
    nh = N_DIL * H_C

    @pl.when(j == 0)
    def _():
        for h in range(nh):
            x = z_ref[:, h * LANE:(h + 1) * LANE]
            y = _rope((x * _rms(x, DH_C)) * gq_ref[...], tk, DH_C // 8)
            q_ref[:, h * LANE:(h + 1) * LANE] = (y * (DH_C ** -0.5)).astype(BF16)

    @pl.when(j == 1)
    def _():
        for h in range(nh):
            x = z_ref[:, h * LANE:(h + 1) * LANE]
            y = _rope((x * _rms(x, DH_C)) * gk_ref[...], tk, DH_C // 8)
            kf_ref[:, h * LANE:(h + 1) * LANE] = y
            k_ref[:, h * LANE:(h + 1) * LANE] = y.astype(BF16)

    @pl.when(j == 2)
    def _():
        v_ref[...] = z_ref[...].astype(BF16)


def _proj1_post(z1, tk, gq, gk, *, n_rows, row0):
    tm = 256
    w = N_DIL * H_C * DH_C
    r0 = row0 // tm
    rows = lambda i, j: (i, 0)
    return pl.pallas_call(
        _proj1_post_kernel,
        grid=(n_rows // tm, 3),
        in_specs=[pl.BlockSpec((tm, w), lambda i, j: (r0 + i, j)),
                  pl.BlockSpec((tm, 3 * LANE), lambda i, j: (r0 + i, 0)),
                  pl.BlockSpec((1, LANE), lambda i, j: (0, 0)),
                  pl.BlockSpec((1, LANE), lambda i, j: (0, 0))],
        out_specs=[pl.BlockSpec((tm, w), rows)] * 4,
        out_shape=[jax.ShapeDtypeStruct((n_rows, w), BF16), jax.ShapeDtypeStruct((n_rows, w), BF16),
                   jax.ShapeDtypeStruct((n_rows, w), F32), jax.ShapeDtypeStruct((n_rows, w), BF16)],
        compiler_params=_cparams(("parallel", "arbitrary")),
        name="proj1_post",
    )(z1, tk, gq, gk)


def _dil_prompt_kernel(q_ref, kc_ref, kp_ref, vc_ref, vp_ref, o_ref, lse_ref, *, tq):
    qi = pl.program_id(2)
    nt = (((1,), (1,)), ((), ()))
    r = lax.broadcasted_iota(I32, (tq, tq), 0)
    c = lax.broadcasted_iota(I32, (tq, tq), 1)
    keep_c = c <= r
    keep_p = jnp.logical_and(c >= r, qi > 0)
    lane = lax.broadcasted_iota(I32, (tq, LANE), 1)
    lse_tile = jnp.zeros((tq, LANE), F32)
    for h in range(H_C):
        sl = slice(h * LANE, (h + 1) * LANE)
        q = q_ref[0, :, sl]
        s_c = jnp.where(keep_c, lax.dot_general(q, kc_ref[0, :, sl], nt, preferred_element_type=F32), NEG_INF)
        s_p = jnp.where(keep_p, lax.dot_general(q, kp_ref[0, :, sl], nt, preferred_element_type=F32), NEG_INF)
        m = jnp.maximum(jnp.max(s_c, axis=1, keepdims=True), jnp.max(s_p, axis=1, keepdims=True))
        p_c = jnp.exp(s_c - m)
        p_p = jnp.exp(s_p - m)
        l = jnp.sum(p_c, axis=1, keepdims=True) + jnp.sum(p_p, axis=1, keepdims=True)
        o = (jnp.dot(p_c.astype(BF16), vc_ref[0, :, sl], preferred_element_type=F32)
             + jnp.dot(p_p.astype(BF16), vp_ref[0, :, sl], preferred_element_type=F32))
        o_ref[0, :, sl] = (o / l).astype(o_ref.dtype)
        lse_tile = jnp.where(lane == h, m + jnp.log(l), lse_tile)
    lse_ref[0] = lse_tile


def _dil_prompt(q, k, v, g, dil, nb, seq):
    tq = PAGE
    w = H_C * DH_C
    ns = seq // dil
    if dil == 1:
        view = lambda a: a.reshape(nb, ns, a.shape[1])
        col = lambda r: g
    else:
        view = lambda a: a[:, g * w:(g + 1) * w].reshape(nb, ns, dil * w)
        col = lambda r: r
    cur = lambda b, r, i: (b, i, col(r))
    prev = lambda b, r, i: (b, jnp.maximum(i - 1, 0), col(r))
    blk = lambda m: pl.BlockSpec((1, tq, w), m)
    o, lse = pl.pallas_call(
        functools.partial(_dil_prompt_kernel, tq=tq),
        grid=(nb, dil, ns // tq),
        in_specs=[blk(cur), blk(cur), blk(prev), blk(cur), blk(prev)],
        out_specs=[pl.BlockSpec((1, tq, w), lambda b, r, i: (b, i, r)),
                   pl.BlockSpec((1, tq, LANE), lambda b, r, i: (b, i, r))],
        out_shape=[jax.ShapeDtypeStruct((nb, ns, dil * w), BF16), jax.ShapeDtypeStruct((nb, ns, dil * LANE), F32)],
        compiler_params=_cparams(("parallel", "parallel", "parallel")),
        name=f"dil_prompt{g}",
    )(view(q), view(k), view(k), view(v), view(v))
    return o.reshape(nb * seq, w), lse.reshape(nb * seq, LANE)


def _dil_sample_kernel(q_ref, kn_ref, vn_ref, st_ref, tail_ref, new_ref, o_ref, lse_ref, ns_ref, m_s, l_s, acc_s,
                       *, g, dil, ch, n_chunks):
    c = pl.program_id(1)
    nt = (((1,), (1,)), ((), ()))
    w = H_C * DH_C
    rpw = 2 * H_C
    body = (ch - SUB) * rpw
    half = dil == 2 * SUB
    n_slab = ch // dil

    if half:
        ns_ref[0, :, 0:SUB * rpw, :] = st_ref[0, :, SUB * rpw:dil * rpw, :]
        if n_slab > 1:
            ns_ref[0, 0:n_slab - 1, SUB * rpw:dil * rpw, :] = st_ref[0, 1:n_slab, 0:SUB * rpw, :]
    else:
        ns_ref[0, 0:body] = st_ref[0, SUB * rpw:ch * rpw]

    @pl.when(c < n_chunks - 1)
    def _():
        if half:
            ns_ref[0, n_slab - 1, SUB * rpw:dil * rpw, :] = tail_ref[0, 0]
        else:
            ns_ref[0, body:ch * rpw] = tail_ref[0]

    @pl.when(c == n_chunks - 1)
    def _():
        if half:
            ns_ref[0, n_slab - 1, SUB * rpw:dil * rpw, :] = new_ref[0, 0]
        else:
            ns_ref[0, body:ch * rpw] = new_ref[0]

    @pl.when(c == 0)
    def _():
        m_s[...] = jnp.full(m_s.shape, NEG_INF, F32)
        l_s[...] = jnp.zeros(l_s.shape, F32)
        acc_s[...] = jnp.zeros(acc_s.shape, F32)

    n_rows = H_C * SUB
    heads = [slice(g * w + h * LANE, g * w + (h + 1) * LANE) for h in range(H_C)]

    def update(keys, vals, keep):
        s = jnp.concatenate([lax.dot_general(q_ref[:, heads[h]], keys[h], nt, preferred_element_type=F32)
                             for h in range(H_C)], axis=0)
        m_new, alpha, p, row_sum = _softmax_step(jnp.where(keep, s, NEG_INF), m_s[...], exp=jnp.exp)
        l_s[...] = alpha * l_s[...] + row_sum
        m_s[...] = m_new
        pv = [jnp.dot(p[h * SUB:(h + 1) * SUB], vals[h], preferred_element_type=F32) for h in range(H_C)]
        acc_s[...] = alpha * acc_s[...] + jnp.concatenate(pv, axis=0)

    if half:
        nk = n_slab * SUB
        t = lax.broadcasted_iota(I32, (n_rows, nk), 0) & (SUB - 1)
        j = lax.broadcasted_iota(I32, (n_rows, nk), 1)
        update([st_ref[0, :, pl.ds(h, SUB, stride=rpw), :].reshape(nk, LANE) for h in range(H_C)],
               [st_ref[0, :, pl.ds(H_C + h, SUB, stride=rpw), :].reshape(nk, LANE) for h in range(H_C)],
               (j & (SUB - 1)) == t)
    else:
        t = lax.broadcasted_iota(I32, (n_rows, ch), 0) & (SUB - 1)
        i = lax.broadcasted_iota(I32, (n_rows, ch), 1) + c * ch
        update([st_ref[0, pl.ds(h, ch, stride=rpw), :] for h in range(H_C)],
               [st_ref[0, pl.ds(H_C + h, ch, stride=rpw), :] for h in range(H_C)],
               jnp.logical_and(i >= t, ((i - t) & (dil - 1)) == 0))

    @pl.when(c == n_chunks - 1)
    def _():
        tt = lax.broadcasted_iota(I32, (n_rows, PAGE), 0) & (SUB - 1)
        tn = lax.broadcasted_iota(I32, (n_rows, PAGE), 1)
        pad = lambda x: jnp.concatenate([x, jnp.zeros((PAGE - SUB, LANE), F32)], axis=0)
        update([pad(kn_ref[:, heads[h]]) for h in range(H_C)], [pad(vn_ref[:, heads[h]]) for h in range(H_C)],
               jnp.logical_and(tn <= tt, ((tt - tn) & (dil - 1)) == 0))
        lane = lax.broadcasted_iota(I32, (SUB, LANE), 1)
        lse_tile = jnp.zeros((SUB, LANE), F32)
        for h in range(H_C):
            rows = pl.ds(h * SUB, SUB)
            l = l_s[rows, :]
            o_ref[:, h * LANE:(h + 1) * LANE] = acc_s[rows, :] / l
            lse_tile = jnp.where(lane == h, m_s[rows, :] + jnp.log(l), lse_tile)
        lse_ref[...] = lse_tile


def _dil_sample(q_s, kf_s, z1, state, new_rows, g, dil, row0):
    nb, win = state.shape[0], state.shape[1]
    w = H_C * DH_C
    rpw = 2 * H_C
    ch = min(win, 512)
    n_chunks = win // ch
    wq = N_DIL * w
    r0 = row0 // SUB
    if dil == 2 * SUB:
        n_slab = ch // dil
        st_view = state.reshape(nb, win // dil, dil * rpw, LANE)
        st_spec = pl.BlockSpec((1, n_slab, dil * rpw, LANE), lambda b, c: (b, c, 0, 0))
        tail_spec = pl.BlockSpec((1, 1, SUB * rpw, LANE),
                                 lambda b, c: (b, jnp.minimum((c + 1) * n_slab, win // dil - 1), 0, 0))
        new_view = new_rows.reshape(nb, 1, SUB * rpw, LANE)
        new_spec = pl.BlockSpec((1, 1, SUB * rpw, LANE), lambda b, c: (b, 0, 0, 0))
    else:
        tail_blocks = ch // SUB
        last_tail = win // SUB - 1
        st_view = state.reshape(nb, win * rpw, LANE)
        st_spec = pl.BlockSpec((1, ch * rpw, LANE), lambda b, c: (b, c, 0))
        tail_spec = pl.BlockSpec((1, SUB * rpw, LANE),
                                 lambda b, c: (b, jnp.minimum((c + 1) * tail_blocks, last_tail), 0))
        new_view = new_rows.reshape(nb, SUB * rpw, LANE)
        new_spec = pl.BlockSpec((1, SUB * rpw, LANE), lambda b, c: (b, 0, 0))
    o, lse, ns = pl.pallas_call(
        functools.partial(_dil_sample_kernel, g=g, dil=dil, ch=ch, n_chunks=n_chunks),
        grid=(nb, n_chunks),
        in_specs=[pl.BlockSpec((SUB, wq), lambda b, c: (b, 0)),
                  pl.BlockSpec((SUB, wq), lambda b, c: (b, 0)),
                  pl.BlockSpec((SUB, wq), lambda b, c: (r0 + b, 2)),
                  st_spec, tail_spec, new_spec],
        out_specs=[pl.BlockSpec((SUB, w), lambda b, c: (b, 0)), pl.BlockSpec((SUB, LANE), lambda b, c: (b, 0)),
                   st_spec],
        out_shape=[jax.ShapeDtypeStruct((nb * SUB, w), F32), jax.ShapeDtypeStruct((nb * SUB, LANE), F32),
                   jax.ShapeDtypeStruct(st_view.shape, F32)],
        scratch_shapes=[pltpu.VMEM((H_C * SUB, LANE), F32), pltpu.VMEM((H_C * SUB, LANE), F32),
                        pltpu.VMEM((H_C * SUB, LANE), F32)],
        compiler_params=_cparams(("parallel", "arbitrary")),
        name=f"dil_sample{g}",
    )(q_s, kf_s, z1, st_view, st_view, new_view)
    return o, lse, ns.reshape(state.shape)


def _dil_combine_kernel(o0_ref, o1_ref, o2_ref, l0_ref, l1_ref, l2_ref, a_ref):
    l0, l1, l2 = l0_ref[...], l1_ref[...], l2_ref[...]
    m = jnp.maximum(jnp.maximum(l0, l1), l2)
    w0, w1, w2 = jnp.exp(l0 - m), jnp.exp(l1 - m), jnp.exp(l2 - m)
    den = w0 + w1 + w2
    w0, w1, w2 = w0 / den, w1 / den, w2 / den
    for h in range(H_C):
        sl = slice(h * LANE, (h + 1) * LANE)
        a_ref[:, sl] = (w0[:, h:h + 1] * o0_ref[:, sl].astype(F32) + w1[:, h:h + 1] * o1_ref[:, sl].astype(F32)
                        + w2[:, h:h + 1] * o2_ref[:, sl].astype(F32)).astype(a_ref.dtype)


def _dil_combine(outs, lses, out_dtype, tm):
    n = outs[0].shape[0]
    w = H_C * DH_C
    rows = lambda i: (i, 0)
    return pl.pallas_call(
        _dil_combine_kernel,
        grid=(n // tm,),
        in_specs=[pl.BlockSpec((tm, w), rows)] * 3 + [pl.BlockSpec((tm, LANE), rows)] * 3,
        out_specs=pl.BlockSpec((tm, w), rows),
        out_shape=jax.ShapeDtypeStruct((n, w), out_dtype),
        compiler_params=_cparams(("parallel",)),
        name="dil_combine",
    )(*outs, *lses)


def _router_kernel(h_ref, r_ref, idx_ref, gate_ref):
    logits = jnp.dot(h_ref[...], r_ref[...].astype(BF16), preferred_element_type=F32)
    lane = lax.broadcasted_iota(I32, logits.shape, 1)
    lanef = lane.astype(F32)
    lg = jnp.where(lane < N_EXPERTS, logits, -jnp.inf)
    m1 = jnp.max(lg, axis=1, keepdims=True)
    i1 = jnp.min(jnp.where(lg == m1, lanef, float(LANE)), axis=1, keepdims=True)
    lg2 = jnp.where(lanef == i1, -jnp.inf, lg)
    m2 = jnp.max(lg2, axis=1, keepdims=True)
    i2 = jnp.min(jnp.where(lg2 == m2, lanef, float(LANE)), axis=1, keepdims=True)
    e = jnp.exp(m2 - m1)
    g1 = 1.0 / (1.0 + e)
    g2 = e / (1.0 + e)
    idx_ref[...] = jnp.where(lane == 0, i1, jnp.where(lane == 1, i2, 0.0)).astype(I32)
    gate_ref[...] = jnp.where(lane == 0, g1, jnp.where(lane == 1, g2, 0.0))


def _router(h, router):
    t = h.shape[0]
    tm = 512
    rp = jnp.pad(router, ((0, 0), (0, LANE - N_EXPERTS)))
    return pl.pallas_call(
        _router_kernel,
        grid=(t // tm,),
        in_specs=[pl.BlockSpec((tm, D), lambda i: (i, 0)), pl.BlockSpec((D, LANE), lambda i: (0, 0))],
        out_specs=[pl.BlockSpec((tm, LANE), lambda i: (i, 0))] * 2,
        out_shape=[jax.ShapeDtypeStruct((t, LANE), I32), jax.ShapeDtypeStruct((t, LANE), F32)],
        compiler_params=_cparams(("parallel",)),
        name="router",
    )(h, rp)


def _moe_gather_kernel(tok_ref, h_hbm, o_ref, buf, sem, *, tm):
    i = pl.program_id(0)
    n = pl.num_programs(0)

    def row_copy(tile, slot, r):
        return pltpu.make_async_copy(h_hbm.at[pl.ds(tok_ref[tile * tm + r], 1)], buf.at[slot, pl.ds(r, 1)],
                                     sem.at[slot])

    def issue(tile, slot):
        def body(r, carry):
            row_copy(tile, slot, r).start()
            return carry
        lax.fori_loop(0, tm, body, 0)

    @pl.when(i == 0)
    def _():
        issue(0, 0)

    @pl.when(i + 1 < n)
    def _():
        issue(i + 1, (i + 1) % 2)

    slot = i % 2

    def wait(r, carry):
        row_copy(i, slot, r).wait()
        return carry

    lax.fori_loop(0, tm, wait, 0)
    o_ref[...] = buf[slot].astype(BF16)


def _moe_gather(tok_of_slot, hf):
    p = tok_of_slot.shape[0]
    tm = MOE_TM
    grid_spec = pltpu.PrefetchScalarGridSpec(
        num_scalar_prefetch=1, grid=(p // tm,),
        in_specs=[pl.BlockSpec(memory_space=pl.ANY)],
        out_specs=pl.BlockSpec((tm, D), lambda i, tok: (i, 0)),
        scratch_shapes=[pltpu.VMEM((2, tm, D), F32), pltpu.SemaphoreType.DMA((2,))])
    return pl.pallas_call(
        functools.partial(_moe_gather_kernel, tm=tm),
        grid_spec=grid_spec,
        out_shape=jax.ShapeDtypeStruct((p, D), BF16),
        compiler_params=_cparams(("arbitrary",)),
        name="moe_gather",
    )(tok_of_slot, hf)


def _moe_up_kernel(te_ref, nu_ref, x_ref, wg_ref, wu_ref, o_ref, wgb_ref, wub_ref):
    i = pl.program_id(1)
    changed = jnp.logical_or(i == 0, te_ref[i] != te_ref[jnp.maximum(i - 1, 0)])

    @pl.when(changed)
    def _():
        wgb_ref[...] = wg_ref[0].astype(BF16)
        wub_ref[...] = wu_ref[0].astype(BF16)

    @pl.when(i < nu_ref[0])
    def _():
        x = x_ref[...]
        a = jnp.dot(x, wgb_ref[...], preferred_element_type=F32)
        b = jnp.dot(x, wub_ref[...], preferred_element_type=F32)
        o_ref[...] = ((a * jax.nn.sigmoid(a)) * b).astype(BF16)

    @pl.when(i >= nu_ref[0])
    def _():
        o_ref[...] = jnp.zeros(o_ref.shape, BF16)


def _moe_up(te, nu, xs, wg, wu, tn=512):
    p = xs.shape[0]
    tm = MOE_TM
    n = wg.shape[2]
    grid_spec = pltpu.PrefetchScalarGridSpec(
        num_scalar_prefetch=2, grid=(n // tn, p // tm),
        in_specs=[pl.BlockSpec((tm, D), lambda j, i, te, nu: (i, 0)),
                  pl.BlockSpec((1, D, tn), lambda j, i, te, nu: (te[i], 0, j)),
                  pl.BlockSpec((1, D, tn), lambda j, i, te, nu: (te[i], 0, j))],
        out_specs=pl.BlockSpec((tm, tn), lambda j, i, te, nu: (i, j)),
        scratch_shapes=[pltpu.VMEM((D, tn), BF16), pltpu.VMEM((D, tn), BF16)])
    return pl.pallas_call(
        _moe_up_kernel, grid_spec=grid_spec,
        out_shape=jax.ShapeDtypeStruct((p, n), BF16),
        compiler_params=_cparams(("arbitrary", "arbitrary"), VMEM_BIG),
        name="moe_up",
    )(te, nu, xs, wg, wu)


def _moe_down_kernel(te_ref, nu_ref, x_ref, w_ref, o_ref, wb_ref):
    i = pl.program_id(1)
    changed = jnp.logical_or(i == 0, te_ref[i] != te_ref[jnp.maximum(i - 1, 0)])

    @pl.when(changed)
    def _():
        wb_ref[...] = w_ref[0].astype(BF16)

    @pl.when(i < nu_ref[0])
    def _():
        o_ref[...] = jnp.dot(x_ref[...], wb_ref[...], preferred_element_type=F32)

    @pl.when(i >= nu_ref[0])
    def _():
        o_ref[...] = jnp.zeros(o_ref.shape, F32)


def _moe_down(te, nu, act, wd, tn=512):
    p, k = act.shape
    tm = MOE_TM_DOWN
    n = wd.shape[2]
    grid_spec = pltpu.PrefetchScalarGridSpec(
        num_scalar_prefetch=2, grid=(n // tn, p // tm),
        in_specs=[pl.BlockSpec((tm, k), lambda j, i, te, nu: (i, 0)),
                  pl.BlockSpec((1, k, tn), lambda j, i, te, nu: (te[i], 0, j))],
        out_specs=pl.BlockSpec((tm, tn), lambda j, i, te, nu: (i, j)),
        scratch_shapes=[pltpu.VMEM((k, tn), BF16)])
    return pl.pallas_call(
        _moe_down_kernel, grid_spec=grid_spec,
        out_shape=jax.ShapeDtypeStruct((p, n), F32),
        compiler_params=_cparams(("arbitrary", "arbitrary"), VMEM_BIG),
        name="moe_down",
    )(te, nu, act, wd)


def _moe_combine_kernel(slot_ref, ys_hbm, x_ref, gt_ref, g_ref, o_ref, buf_a, buf_b, sem, *, tm):
    base = pl.program_id(0) * tm

    def issue(r, carry):
        pltpu.make_async_copy(ys_hbm.at[pl.ds(slot_ref[2 * (base + r)], 1)], buf_a.at[pl.ds(r, 1)], sem).start()
        pltpu.make_async_copy(ys_hbm.at[pl.ds(slot_ref[2 * (base + r) + 1], 1)], buf_b.at[pl.ds(r, 1)], sem).start()
        return carry

    def wait(r, carry):
        pltpu.make_async_copy(ys_hbm.at[pl.ds(0, 1)], buf_a.at[pl.ds(r, 1)], sem).wait()
        pltpu.make_async_copy(ys_hbm.at[pl.ds(0, 1)], buf_b.at[pl.ds(r, 1)], sem).wait()
        return carry

    lax.fori_loop(0, tm, issue, 0)
    lax.fori_loop(0, tm, wait, 0)
    gt = gt_ref[...]
    y = gt[:, 0:1] * buf_a[...] + gt[:, 1:2] * buf_b[...]
    o_ref[...] = x_ref[...] + g_ref[...] * y.reshape(o_ref.shape)


def _moe_combine(slot_of_assign, ys, x, gates, mod, k_gate):
    t = x.shape[0]
    tm = 256
    gb = tm // SUB
    grid_spec = pltpu.PrefetchScalarGridSpec(
        num_scalar_prefetch=1, grid=(t // tm,),
        in_specs=[pl.BlockSpec(memory_space=pl.ANY),
                  pl.BlockSpec((gb, SUB, D), lambda i, sl: (i, 0, 0)),
                  pl.BlockSpec((tm, LANE), lambda i, sl: (i, 0)),
                  pl.BlockSpec((gb, 1, D), lambda i, sl: (i, 0, k_gate))],
        out_specs=pl.BlockSpec((gb, SUB, D), lambda i, sl: (i, 0, 0)),
        scratch_shapes=[pltpu.VMEM((tm, D), F32), pltpu.VMEM((tm, D), F32), pltpu.SemaphoreType.DMA(())])
    out = pl.pallas_call(
        functools.partial(_moe_combine_kernel, tm=tm),
        grid_spec=grid_spec,
        out_shape=jax.ShapeDtypeStruct((t // SUB, SUB, D), F32),
        compiler_params=_cparams(("arbitrary",)),
        name="moe_combine",
    )(slot_of_assign, ys, x.reshape(t // SUB, SUB, D), gates, mod)
    return out.reshape(t, D)


def _moe_plan(idx, t):
    tm = MOE_TM
    n_assign = 2 * t
    p = n_assign + N_EXPERTS * tm
    e_flat = idx[:, 0:2].reshape(n_assign)
    onehot = (e_flat[:, None] == jnp.arange(N_EXPERTS, dtype=I32)[None, :]).astype(I32)
    csum = jnp.cumsum(onehot, axis=0)
    rank = jnp.take_along_axis(csum, e_flat[:, None], axis=1)[:, 0] - 1
    counts = csum[-1]
    padded = ((counts + tm - 1) // tm) * tm
    ends = jnp.cumsum(padded)
    slot = (ends - padded)[e_flat] + rank
    tok_of_slot = jnp.zeros((p,), I32).at[slot].set(jnp.arange(n_assign, dtype=I32) // 2)
    tile_start = jnp.arange(p // tm, dtype=I32) * tm
    te = jnp.minimum(jnp.searchsorted(ends, tile_start, side="right").astype(I32), N_EXPERTS - 1)
    n_used = (ends[-1] // tm).astype(I32).reshape(1)
    return slot.astype(I32), tok_of_slot, te, n_used


def kernel(x_prompt, x_sample, c_prompt, c_sample, page_table, cache_a_k, cache_a_v, cache_b_lat, cache_b_krope,
           state_c_win0, state_c_win1, state_c_win2, ada_w, ada_b, norm_mix, norm_ffn, l0_w_in, l0_a_qnorm,
           l0_a_knorm, l0_a_lambda, l0_a_subln, l0_b_qa_norm, l0_b_w_uq, l0_b_kv_norm, l0_b_w_ukv, l0_b_qnorm,
           l0_b_knorm, l0_w_out, l0_ffn_gate, l0_ffn_up, l0_ffn_down, l1_w_in, l1_c_qnorm, l1_c_knorm, l1_w_out,
           l1_router, l1_moe_gate, l1_moe_up, l1_moe_down):
    nbp, seq, _ = x_prompt.shape
    nbs, dseq, _ = x_sample.shape
    assert dseq == SUB
    n_pages = page_table.shape[1]
    past = n_pages * PAGE
    tp, ts = nbp * seq, nbs * dseq
    t = tp + ts
    n_pool = cache_a_k.shape[0]

    nc = nbp + nbs
    ncp = -(-nc // SUB) * SUB
    c_all = jnp.pad(jnp.concatenate([c_prompt, c_sample], axis=0), ((0, ncp - nc), (0, 0)))
    mods = _adaln_all(c_all, ada_w, ada_b)
    rg_idx = jnp.concatenate([jnp.repeat(jnp.arange(nbp, dtype=I32), seq // SUB), nbp + jnp.arange(nbs, dtype=I32)])
    mod = [mods[l][rg_idx][:, None, :] for l in range(DEPTH)]

    x = jnp.concatenate([x_prompt.reshape(tp, D), x_sample.reshape(ts, D)], axis=0)
    pos = jnp.concatenate([jnp.tile(jnp.arange(seq, dtype=I32), nbp), jnp.tile(past + jnp.arange(dseq, dtype=I32), nbs)])
    tab_a = _rope_tables(pos, DH_A // 8, 0, DH_A)
    tab_k = _rope_tables(pos, ROPE_B // 2, 0, LANE)
    tab_b = _rope_tables(pos, ROPE_B // 2, NOPE_B, LANE)

    eye2 = jnp.eye(2, dtype=F32)
    w_qa = l0_w_in[:, :1024].reshape(D, H_A, 2, 1, DH_A) * eye2[None, None, :, :, None]
    w0 = jnp.concatenate([w_qa.reshape(D, 2 * H_A * LANE), l0_w_in[:, 1024:2336],
                          jnp.zeros((D, LANE - ROPE_B), F32)], axis=1)
    padl = lambda a, n: jnp.pad(a, ((0, 0),) * (a.ndim - 1) + ((0, n),))
    wuq = padl(l0_b_w_uq.reshape(Q_LORA, H_B, QK_B), LANE - QK_B).reshape(Q_LORA, H_B * LANE).astype(BF16)
    wukv = l0_b_w_ukv.reshape(KV_LORA, H_B, NOPE_B + VH_B)
    wuk_pad = padl(wukv[:, :, :NOPE_B], LANE - NOPE_B).reshape(KV_LORA, H_B * LANE).astype(BF16)
    wuk = wukv[:, :, :NOPE_B].reshape(KV_LORA, H_B * NOPE_B).astype(BF16)
    wuv = wukv[:, :, NOPE_B:].reshape(KV_LORA, H_B * VH_B).astype(BF16)
    wukt = jnp.pad(jnp.transpose(wukv[:, :, :NOPE_B], (1, 2, 0)), ((0, 0), (0, LANE - NOPE_B), (0, 0)))
    g_qa = jnp.tile(l0_a_qnorm, 2).reshape(1, LANE)
    g_ka = jnp.tile(l0_a_knorm, 2).reshape(1, LANE)
    g_qb = padl(l0_b_qnorm, LANE - QK_B).reshape(1, LANE)
    g_kb = padl(l0_b_knorm, LANE - QK_B).reshape(1, LANE)
    gains0 = (g_qa, g_ka, l0_b_qa_norm.reshape(1, Q_LORA), l0_b_kv_norm.reshape(1, KV_LORA), g_qb, g_kb)
    subln = l0_a_subln.reshape(1, LANE)

    h = _norm_mod(x, norm_mix[0], mod[0], 1, 0)
    z0 = _mm(h, w0, tm=1024, tn=384)
    tabs = (tab_a, tab_b, tab_k)
    q_p, ka_p, va_p, lat_p, kr_p, ks_p, vs_p = _proj0_post(
        z0, tabs, gains0, (wuq, wuk_pad, wuv), prompt=True, nb=nbp, seq=seq, row0=0)
    q_s, ka_s, va_s, lat_s, kr_s = _proj0_post(
        z0, tabs, gains0, (wuq, wuk_pad, wuv), prompt=False, nb=nbs, seq=dseq, row0=tp)
    o_p = _flash(q_p, ks_p, vs_p)
    a_p = _attn0_post(o_p, l0_a_lambda, subln)
    caches = (cache_a_k.reshape(n_pool, PAGE * KV_A, LANE), cache_a_v.reshape(n_pool, PAGE * KV_A, LANE),
              cache_b_lat, jnp.swapaxes(cache_b_krope, 1, 2))
    krt_new = jnp.pad(jnp.swapaxes(kr_s.reshape(nbs, dseq, ROPE_B), 1, 2), ((0, 0), (0, 0), (0, PAGE - dseq)))
    wuv_f = wukv[:, :, NOPE_B:].reshape(KV_LORA, H_B * VH_B)
    a_s = _decode0(page_table, q_s, ka_s, va_s, lat_s, krt_new, caches, (wukt, wuk, wuv_f), g_kb, l0_a_lambda, subln)
    a0 = jnp.concatenate([a_p, a_s.astype(BF16)], axis=0)
    x = _mm_res(a0, l0_w_out, x, mod[0], 2, tm=1024)
    h = _norm_mod(x, norm_ffn[0], mod[0], 4, 3)
    act = _mm_swiglu(h, l0_ffn_gate, l0_ffn_up, tm=1024)
    x = _mm_res(act, l0_ffn_down, x, mod[0], 5)

    h = _norm_mod(x, norm_mix[1], mod[1], 1, 0)
    z1 = _mm(h, l1_w_in, tm=1024, tn=512)
    gq1 = l1_c_qnorm.reshape(1, LANE)
    gk1 = l1_c_knorm.reshape(1, LANE)
    q1p, k1p, kf1p, v1p = _proj1_post(z1, tab_k, gq1, gk1, n_rows=tp, row0=0)
    q1s, _, kf1s, _ = _proj1_post(z1, tab_k, gq1, gk1, n_rows=ts, row0=tp)
    q1s_f = q1s.astype(F32)
    states = (state_c_win0, state_c_win1, state_c_win2)
    w = H_C * DH_C
    v1f = z1[:, 2 * N_DIL * w:]
    new_k = kf1s.reshape(nbs, dseq, N_DIL, H_C, DH_C)
    new_v = v1f[tp:].reshape(nbs, dseq, N_DIL, H_C, DH_C)
    outs_p, lses_p, outs_s, lses_s, win_s = [], [], [], [], []
    for g, (_, dil) in enumerate(DIL_GROUPS):
        o, lse = _dil_prompt(q1p, k1p, v1p, g, dil, nbp, seq)
        outs_p.append(o)
        lses_p.append(lse)
        new_rows = jnp.stack([new_k[:, :, g], new_v[:, :, g]], axis=2)
        o, lse, ns = _dil_sample(q1s_f, kf1s, z1, states[g], new_rows, g, dil, tp)
        outs_s.append(o)
        lses_s.append(lse)
        win_s.append(ns)
    a1 = jnp.concatenate([_dil_combine(outs_p, lses_p, BF16, 256),
                          _dil_combine(outs_s, lses_s, F32, 256).astype(BF16)], axis=0)
    x = _mm_res(a1, l1_w_out, x, mod[1], 2, tm=1024)
    h, hf = _norm_mod(x, norm_ffn[1], mod[1], 4, 3, want_f32=True)
    idx, gates = _router(h, l1_router)
    slot, tok_of_slot, te, n_used = _moe_plan(idx, t)
    xs = _moe_gather(tok_of_slot, hf)
    act = _moe_up(te, n_used, xs, l1_moe_gate, l1_moe_up)
    sub_tiles = MOE_TM // MOE_TM_DOWN
    ys = _moe_down(jnp.repeat(te, sub_tiles), n_used * sub_tiles, act, l1_moe_down)
    x = _moe_combine(slot, ys, x, gates, mod[1], 5)

    y_p = x[:tp].reshape(nbp, seq, D)
    y_s = x[tp:].reshape(nbs, dseq, D)
    win_p = []
    for g, (win, _) in enumerate(DIL_GROUPS):
        wl = min(win, seq)
        kk = kf1p.reshape(nbp, seq, N_DIL, H_C, DH_C)[:, seq - wl:, g]
        vv = v1f[:tp].reshape(nbp, seq, N_DIL, H_C, DH_C)[:, seq - wl:, g]
        win_p.append(jnp.stack([kk, vv], axis=2))
    return (y_p, y_s,
            ka_p.reshape(nbp, seq, KV_A, 2 * DH_A), ka_s.reshape(nbs, dseq, KV_A, 2 * DH_A),
            va_p.reshape(nbp, seq, KV_A, 2 * DH_A), va_s.reshape(nbs, dseq, KV_A, 2 * DH_A),
            lat_p.reshape(nbp, seq, KV_LORA), lat_s.reshape(nbs, dseq, KV_LORA),
            kr_p.reshape(nbp, seq, ROPE_B), kr_s.reshape(nbs, dseq, ROPE_B),
            win_p[0], win_s[0], win_p[1], win_s[1], win_p[2], win_s[2])
```

```python
import functools

import jax
import jax.numpy as jnp
from jax import lax
from jax.experimental import pallas as pl
from jax.experimental.pallas import tpu as pltpu

F32 = jnp.float32
BF16 = jnp.bfloat16
I32 = jnp.int32

D = 2048
DEPTH = 2
PAGE = 128
ROPE_THETA = 500000.0
EPS = 1e-6
NEG_INF = -1e30

H_A, KV_A, DH_A = 8, 2, 64
LAMBDA_INIT = 0.2
H_B, Q_LORA, KV_LORA, NOPE_B, ROPE_B, VH_B = 8, 512, 256, 64, 32, 128
QK_B = NOPE_B + ROPE_B
H_C, DH_C = 8, 128
DIL_GROUPS = ((128, 1), (512, 4), (2048, 16))
N_DIL = 3
D_FF = 5632
N_EXPERTS = 8
D_FF_E = 7168
OUT1 = H_C * DH_C

LANE = 128
SUB = 8
VMEM_BIG = 56 * 1024 * 1024

N_QS = 24
N_KS = 10
Z0_W = 27 * LANE
PAGES_PER_STEP = 16
MOE_TM = 512
MOE_TM_DOWN = 256
LOG2E = 1.4426950408889634


def _cparams(sem, vmem=None):
    return pltpu.CompilerParams(dimension_semantics=sem, vmem_limit_bytes=vmem)


def _adaln_kernel(c_ref, w_ref, b_ref, o_ref):
    c = c_ref[...]
    a = (c * jax.nn.sigmoid(c)).astype(BF16)
    o_ref[0] = jnp.dot(a, w_ref[0].astype(BF16), preferred_element_type=F32) + b_ref[0]


def _adaln_all(c_all, ada_w, ada_b):
    nb = c_all.shape[0]
    tn = 1024
    return pl.pallas_call(
        _adaln_kernel,
        grid=(DEPTH, 6 * D // tn),
        in_specs=[pl.BlockSpec((nb, D), lambda l, j: (0, 0)),
                  pl.BlockSpec((1, D, tn), lambda l, j: (l, 0, j)),
                  pl.BlockSpec((1, 1, tn), lambda l, j: (l, 0, j))],
        out_specs=pl.BlockSpec((1, nb, tn), lambda l, j: (l, 0, j)),
        out_shape=jax.ShapeDtypeStruct((DEPTH, nb, 6 * D), F32),
        compiler_params=_cparams(("parallel", "parallel")),
        name="adaln",
    )(c_all, ada_w, ada_b.reshape(DEPTH, 1, 6 * D))


def _norm_mod_kernel(x_ref, g_ref, sc_ref, sh_ref, o_ref, *of_ref):
    x = x_ref[...]
    ms = jnp.mean(x * x, axis=-1, keepdims=True)
    y = (x * lax.rsqrt(ms + EPS)) * g_ref[...]
    y = y * (1.0 + sc_ref[...]) + sh_ref[...]
    y2 = y.reshape(o_ref.shape)
    o_ref[...] = y2.astype(BF16)
    if of_ref:
        of_ref[0][...] = y2


def _norm_mod(x, gain, mod, k_sc, k_sh, want_f32=False):
    t = x.shape[0]
    gb = 32
    tm = gb * SUB
    out_shape = [jax.ShapeDtypeStruct((t, D), BF16)]
    out_specs = [pl.BlockSpec((tm, D), lambda i: (i, 0))]
    if want_f32:
        out_shape.append(jax.ShapeDtypeStruct((t, D), F32))
        out_specs.append(pl.BlockSpec((tm, D), lambda i: (i, 0)))
    res = pl.pallas_call(
        _norm_mod_kernel,
        grid=(t // tm,),
        in_specs=[pl.BlockSpec((gb, SUB, D), lambda i: (i, 0, 0)),
                  pl.BlockSpec((1, 1, D), lambda i: (0, 0, 0)),
                  pl.BlockSpec((gb, 1, D), lambda i: (i, 0, k_sc)),
                  pl.BlockSpec((gb, 1, D), lambda i: (i, 0, k_sh))],
        out_specs=out_specs,
        out_shape=out_shape,
        compiler_params=_cparams(("parallel",)),
        name="norm_mod",
    )(x.reshape(t // SUB, SUB, D), gain.reshape(1, 1, D), mod, mod)
    return res if want_f32 else res[0]


def _mm_kernel(x_ref, w_ref, o_ref, wb_ref):
    @pl.when(pl.program_id(1) == 0)
    def _():
        wb_ref[...] = w_ref[...].astype(BF16)
    o_ref[...] = jnp.dot(x_ref[...], wb_ref[...], preferred_element_type=F32).astype(o_ref.dtype)


def _mm(x, w, tm=512, tn=512, out_dtype=F32):
    m, k = x.shape
    n = w.shape[1]
    return pl.pallas_call(
        _mm_kernel,
        grid=(n // tn, m // tm),
        in_specs=[pl.BlockSpec((tm, k), lambda j, i: (i, 0)),
                  pl.BlockSpec((k, tn), lambda j, i: (0, j))],
        out_specs=pl.BlockSpec((tm, tn), lambda j, i: (i, j)),
        out_shape=jax.ShapeDtypeStruct((m, n), out_dtype),
        scratch_shapes=[pltpu.VMEM((k, tn), BF16)],
        compiler_params=_cparams(("arbitrary", "arbitrary"), VMEM_BIG),
        name="mm",
    )(x, w)


def _mm_res_kernel(x_ref, w_ref, r_ref, g_ref, o_ref, wb_ref):
    @pl.when(pl.program_id(1) == 0)
    def _():
        wb_ref[...] = w_ref[...].astype(BF16)
    acc = jnp.dot(x_ref[...], wb_ref[...], preferred_element_type=F32)
    o_ref[...] = r_ref[...] + g_ref[...] * acc.reshape(o_ref.shape)


def _mm_res(x, w, res, mod, k_gate, tm=512, tn=512):
    m, k = x.shape
    n = w.shape[1]
    gb = tm // SUB
    nj = n // tn
    out = pl.pallas_call(
        _mm_res_kernel,
        grid=(nj, m // tm),
        in_specs=[pl.BlockSpec((tm, k), lambda j, i: (i, 0)),
                  pl.BlockSpec((k, tn), lambda j, i: (0, j)),
                  pl.BlockSpec((gb, SUB, tn), lambda j, i: (i, 0, j)),
                  pl.BlockSpec((gb, 1, tn), lambda j, i: (i, 0, k_gate * nj + j))],
        out_specs=pl.BlockSpec((gb, SUB, tn), lambda j, i: (i, 0, j)),
        out_shape=jax.ShapeDtypeStruct((m // SUB, SUB, n), F32),
        scratch_shapes=[pltpu.VMEM((k, tn), BF16)],
        compiler_params=_cparams(("arbitrary", "arbitrary"), VMEM_BIG),
        name="mm_res",
    )(x, w, res.reshape(m // SUB, SUB, n), mod)
    return out.reshape(m, n)


def _mm_swiglu_kernel(x_ref, wg_ref, wu_ref, o_ref, wgb_ref, wub_ref):
    @pl.when(pl.program_id(1) == 0)
    def _():
        wgb_ref[...] = wg_ref[...].astype(BF16)
        wub_ref[...] = wu_ref[...].astype(BF16)
    x = x_ref[...]
    a = jnp.dot(x, wgb_ref[...], preferred_element_type=F32)
    b = jnp.dot(x, wub_ref[...], preferred_element_type=F32)
    o_ref[...] = ((a * jax.nn.sigmoid(a)) * b).astype(BF16)


def _mm_swiglu(x, wg, wu, tm=512, tn=512):
    m, k = x.shape
    n = wg.shape[1]
    return pl.pallas_call(
        _mm_swiglu_kernel,
        grid=(n // tn, m // tm),
        in_specs=[pl.BlockSpec((tm, k), lambda j, i: (i, 0)),
                  pl.BlockSpec((k, tn), lambda j, i: (0, j)),
                  pl.BlockSpec((k, tn), lambda j, i: (0, j))],
        out_specs=pl.BlockSpec((tm, tn), lambda j, i: (i, j)),
        out_shape=jax.ShapeDtypeStruct((m, n), BF16),
        scratch_shapes=[pltpu.VMEM((k, tn), BF16), pltpu.VMEM((k, tn), BF16)],
        compiler_params=_cparams(("arbitrary", "arbitrary"), VMEM_BIG),
        name="mm_swiglu",
    )(x, wg, wu)


def _rope_tables(pos, half, offset, period):
    inv_freq = ROPE_THETA ** (-jnp.arange(half, dtype=F32) / half)
    ang = pos.astype(F32)[:, None] * inv_freq[None, :]
    cos, sin = jnp.cos(ang), jnp.sin(ang)
    n = pos.shape[0]
    seg_c = jnp.ones((n, period), F32)
    seg_c = seg_c.at[:, offset:offset + half].set(cos).at[:, offset + half:offset + 2 * half].set(cos)
    seg_m = jnp.zeros((n, period), F32).at[:, offset:offset + half].set(-sin)
    seg_p = jnp.zeros((n, period), F32).at[:, offset + half:offset + 2 * half].set(sin)
    rep = LANE // period
    return jnp.concatenate([jnp.tile(seg_c, (1, rep)), jnp.tile(seg_m, (1, rep)), jnp.tile(seg_p, (1, rep))], axis=1)


def _rope(x, tab, half):
    c = tab[:, 0:LANE]
    sm = tab[:, LANE:2 * LANE]
    sp = tab[:, 2 * LANE:3 * LANE]
    return x * c + pltpu.roll(x, LANE - half, 1) * sm + pltpu.roll(x, half, 1) * sp


def _rms(x, n_valid):
    return lax.rsqrt(jnp.sum(x * x, axis=-1, keepdims=True) * (1.0 / n_valid) + EPS)


def _lane_tiles(x):
    return [x[:, j * LANE:(j + 1) * LANE] for j in range(x.shape[1] // LANE)]


def _fold(parts, op):
    while len(parts) > 1:
        parts = [op(parts[i], parts[i + 1]) if i + 1 < len(parts) else parts[i] for i in range(0, len(parts), 2)]
    return parts[0]


def _softmax_step(s, m_prev, exp=jnp.exp2):
    tiles = _lane_tiles(s)
    m_new = jnp.maximum(m_prev, jnp.max(_fold(tiles, jnp.maximum), axis=1, keepdims=True))
    p_tiles = [exp(t - m_new) for t in tiles]
    row_sum = jnp.sum(_fold(p_tiles, jnp.add), axis=1, keepdims=True)
    return m_new, exp(m_prev - m_new), jnp.concatenate(p_tiles, axis=1), row_sum


def _proj0_post_kernel(z_ref, ta_ref, tb_ref, tk_ref, gq_ref, gk_ref, gqa_ref, gkv_ref, gqb_ref, gkb_ref,
                       wuq_ref, wuk_ref, wuv_ref, *out_refs, prompt):
    if prompt:
        q_ref, ka_ref, va_ref, lat_ref, kr_ref, ks_ref, vs_ref = out_refs
    else:
        q_ref, ka_ref, va_ref, lat_ref, kr_ref = out_refs
    ta = ta_ref[...]
    tb = tb_ref[...]
    tk = tk_ref[...]
    lane = lax.broadcasted_iota(I32, (1, LANE), 1)
    lo = lane < DH_A

    def put_q(s, val):
        if prompt:
            q_ref[0, s] = val.astype(BF16)
        else:
            q_ref[:, s * LANE:(s + 1) * LANE] = val

    for s in range(2 * H_A):
        x = z_ref[:, s * LANE:(s + 1) * LANE]
        y = (x * _rms(x, DH_A)) * gq_ref[...]
        put_q(s, _rope(y, ta, DH_A // 8) * (DH_A ** -0.5 * LOG2E))
    for kv in range(KV_A):
        x = z_ref[:, (16 + kv) * LANE:(17 + kv) * LANE]
        xx = x * x
        s_lo = jnp.sum(jnp.where(lo, xx, 0.0), axis=-1, keepdims=True)
        s_hi = jnp.sum(jnp.where(lo, 0.0, xx), axis=-1, keepdims=True)
        inv = jnp.where(lo, lax.rsqrt(s_lo * (1.0 / DH_A) + EPS), lax.rsqrt(s_hi * (1.0 / DH_A) + EPS))
        k = _rope((x * inv) * gk_ref[...], ta, DH_A // 8)
        v = z_ref[:, (18 + kv) * LANE:(19 + kv) * LANE]
        ka_ref[:, kv * LANE:(kv + 1) * LANE] = k
        va_ref[:, kv * LANE:(kv + 1) * LANE] = v
        if prompt:
            ks_ref[0, kv] = k.astype(BF16)
            vs_ref[0, kv] = v.astype(BF16)
    qc = z_ref[:, 20 * LANE:24 * LANE]
    qcn = (qc * _rms(qc, Q_LORA)) * gqa_ref[...]
    qb = jnp.dot(qcn.astype(BF16), wuq_ref[...], preferred_element_type=F32)
    for h in range(H_B):
        x = _rope(qb[:, h * LANE:(h + 1) * LANE], tb, ROPE_B // 2)
        y = (x * _rms(x, QK_B)) * gqb_ref[...]
        put_q(2 * H_A + h, y * (QK_B ** -0.5 * LOG2E))
    kvc = z_ref[:, 24 * LANE:26 * LANE]
    lat = (kvc * _rms(kvc, KV_LORA)) * gkv_ref[...]
    lat_ref[...] = lat
    kr = _rope(z_ref[:, 26 * LANE:27 * LANE], tk, ROPE_B // 2)
    kr_ref[...] = kr[:, 0:ROPE_B]
    if prompt:
        latb = lat.astype(BF16)
        kn = jnp.dot(latb, wuk_ref[...], preferred_element_type=F32)
        vv = jnp.dot(latb, wuv_ref[...], preferred_element_type=F32)
        kr_hi = pltpu.roll(kr, NOPE_B, 1)
        for h in range(H_B):
            x = kn[:, h * LANE:(h + 1) * LANE] + kr_hi
            ks_ref[0, KV_A + h] = ((x * _rms(x, QK_B)) * gkb_ref[...]).astype(BF16)
            vs_ref[0, KV_A + h] = vv[:, h * LANE:(h + 1) * LANE].astype(BF16)


def _proj0_post(z0, tabs, gains, wts, *, prompt, nb, seq, row0):
    ta, tb, tk = tabs
    tm = 256
    n_rows = nb * seq
    nt = n_rows // tm
    r0 = row0 // tm
    if prompt:
        per_b = seq // tm
        rows = lambda i: (i, 0)
        grid = (nt,)
        zmap = lambda i: (r0 + i, 0)
        q_spec = pl.BlockSpec((1, N_QS, tm, LANE), lambda i: (i // per_b, 0, i % per_b, 0))
        kv_spec = pl.BlockSpec((1, N_KS, tm, LANE), lambda i: (i // per_b, 0, i % per_b, 0))
        out_shape = [jax.ShapeDtypeStruct((nb, N_QS, seq, LANE), BF16)]
        out_specs = [q_spec]
    else:
        rows = lambda i: (i, 0)
        grid = (nt,)
        zmap = lambda i: (r0 + i, 0)
        out_shape = [jax.ShapeDtypeStruct((n_rows, N_QS * LANE), F32)]
        out_specs = [pl.BlockSpec((tm, N_QS * LANE), rows)]
    out_shape += [jax.ShapeDtypeStruct((n_rows, 2 * LANE), F32)] * 3 + [jax.ShapeDtypeStruct((n_rows, ROPE_B), F32)]
    out_specs += [pl.BlockSpec((tm, 2 * LANE), rows)] * 3 + [pl.BlockSpec((tm, ROPE_B), rows)]
    if prompt:
        out_shape += [jax.ShapeDtypeStruct((nb, N_KS, seq, LANE), BF16)] * 2
        out_specs += [kv_spec, kv_spec]
    tab_spec = pl.BlockSpec((tm, 3 * LANE), zmap)
    const = lambda a: pl.BlockSpec(a.shape, lambda i: (0,) * a.ndim)
    return pl.pallas_call(
        functools.partial(_proj0_post_kernel, prompt=prompt),
        grid=grid,
        in_specs=[pl.BlockSpec((tm, Z0_W), zmap), tab_spec, tab_spec, tab_spec]
                 + [const(a) for a in gains] + [const(a) for a in wts],
        out_specs=out_specs,
        out_shape=out_shape,
        compiler_params=_cparams(("parallel",)),
        name="proj0_post_p" if prompt else "proj0_post_s",
    )(z0, ta, tb, tk, *gains, *wts)


def _flash_kernel(q_ref, k0_ref, v0_ref, k1_ref, v1_ref, o_ref, m_ref, l_ref, acc_ref, *, tq):
    qi = pl.program_id(2)
    kv_refs = ((k0_ref, v0_ref), (k1_ref, v1_ref))
    m_ref[...] = jnp.full(m_ref.shape, NEG_INF, F32)
    l_ref[...] = jnp.zeros(l_ref.shape, F32)
    acc_ref[...] = jnp.zeros(acc_ref.shape, F32)

    def step(ki, masked):
        start = pl.multiple_of(ki * tq, tq)
        s = jnp.concatenate(
            [lax.dot_general(q_ref[0, j], k_ref[0, 0, pl.ds(start, tq), :], (((1,), (1,)), ((), ())),
                             preferred_element_type=F32) for j, (k_ref, _) in enumerate(kv_refs)], axis=0)
        if masked:
            r = lax.broadcasted_iota(I32, (2 * tq, tq), 0) & (tq - 1)
            c = lax.broadcasted_iota(I32, (2 * tq, tq), 1)
            s = jnp.where(c <= r, s, NEG_INF)
        m_new, alpha, p, row_sum = _softmax_step(s, m_ref[...])
        l_ref[...] = alpha * l_ref[...] + row_sum
        m_ref[...] = m_new
        p = p.astype(BF16)
        pv = [jnp.dot(p[j * tq:(j + 1) * tq], v_ref[0, 0, pl.ds(start, tq), :], preferred_element_type=F32)
              for j, (_, v_ref) in enumerate(kv_refs)]
        acc_ref[...] = alpha * acc_ref[...] + jnp.concatenate(pv, axis=0)

    def body(ki, carry):
        step(ki, False)
        return carry

    lax.fori_loop(0, qi, body, 0)
    step(qi, True)
    o = acc_ref[...] / l_ref[...]
    for j in range(2):
        o_ref[0, j] = o[j * tq:(j + 1) * tq]


def _flash(q, k, v, tq=512):
    nb, _, seq, _ = q.shape

    def kv_map(j):
        def m(b, p, i):
            s = 2 * p + j
            return (b, jnp.where(s < 2 * H_A, s // (2 * H_A // KV_A), s - (2 * H_A - KV_A)), 0, 0)
        return m

    kv_spec = lambda j: pl.BlockSpec((1, 1, seq, LANE), kv_map(j))
    return pl.pallas_call(
        functools.partial(_flash_kernel, tq=tq),
        grid=(nb, N_QS // 2, seq // tq),
        in_specs=[pl.BlockSpec((1, 2, tq, LANE), lambda b, p, i: (b, p, i, 0)),
                  kv_spec(0), kv_spec(0), kv_spec(1), kv_spec(1)],
        out_specs=pl.BlockSpec((1, 2, tq, LANE), lambda b, p, i: (b, p, i, 0)),
        out_shape=jax.ShapeDtypeStruct((nb, N_QS, seq, LANE), F32),
        scratch_shapes=[pltpu.VMEM((2 * tq, LANE), F32), pltpu.VMEM((2 * tq, LANE), F32),
                        pltpu.VMEM((2 * tq, LANE), F32)],
        compiler_params=_cparams(("parallel", "parallel", "arbitrary")),
        name="flash0",
    )(q, k, v, k, v)


def _diff_lambda(lam_ref):
    lf = lam_ref[...]
    a = jnp.sum(lf[0:1] * lf[1:2], axis=-1, keepdims=True)
    b = jnp.sum(lf[2:3] * lf[3:4], axis=-1, keepdims=True)
    return jnp.exp(a) - jnp.exp(b) + LAMBDA_INIT


def _attn0_post_kernel(o_ref, lam_ref, sub_ref, a_ref):
    lam = _diff_lambda(lam_ref)
    for h in range(H_A):
        d = o_ref[0, 2 * h] - lam * o_ref[0, 2 * h + 1]
        y = ((d * _rms(d, 2 * DH_A)) * sub_ref[...]) * (1.0 - LAMBDA_INIT)
        a_ref[:, h * LANE:(h + 1) * LANE] = y.astype(BF16)
    for h in range(H_B):
        a_ref[:, (H_A + h) * LANE:(H_A + h + 1) * LANE] = o_ref[0, 2 * H_A + h].astype(BF16)


def _attn0_post(o, lam_p, subln):
    nb, _, seq, _ = o.shape
    tm = 256
    per_b = seq // tm
    return pl.pallas_call(
        _attn0_post_kernel,
        grid=(nb * per_b,),
        in_specs=[pl.BlockSpec((1, N_QS, tm, LANE), lambda i: (i // per_b, 0, i % per_b, 0)),
                  pl.BlockSpec((4, DH_A), lambda i: (0, 0)),
                  pl.BlockSpec((1, LANE), lambda i: (0, 0))],
        out_specs=pl.BlockSpec((tm, D), lambda i: (i, 0)),
        out_shape=jax.ShapeDtypeStruct((nb * seq, D), BF16),
        compiler_params=_cparams(("parallel",)),
        name="attn0_post",
    )(o, lam_p, subln)


def _decode0_kernel(pt_ref, q_ref, kan_ref, van_ref, latn_ref, krn_ref, wukt_ref, wuk_ref, wuv_ref, gkb_ref,
                    lam_ref, sub_ref, *rest, n_chunks):
    pps = PAGES_PER_STEP
    ck = rest[0:pps]
    cv = rest[pps:2 * pps]
    cl = rest[2 * pps:3 * pps]
    cr = rest[3 * pps:4 * pps]
    o_ref = rest[4 * pps]
    ka_s, va_s, lat_s, krt_s, krq_s, qa_s, qx_s, m_s, l_s, acca_s, accb_s = rest[4 * pps + 1:]
    c = pl.program_id(1)
    n_rows = 2 * H_A * SUB
    n_rows_b = H_B * SUB

    @pl.when(c == 0)
    def _():
        m_s[...] = jnp.full(m_s.shape, NEG_INF, F32)
        l_s[...] = jnp.zeros(l_s.shape, F32)
        acca_s[...] = jnp.zeros(acca_s.shape, F32)
        accb_s[...] = jnp.zeros(accb_s.shape, F32)
        lane = lax.broadcasted_iota(I32, (SUB, LANE), 1)
        for s in range(2 * H_A):
            qa_s[s * SUB:(s + 1) * SUB, :] = q_ref[:, s * LANE:(s + 1) * LANE]
        for h in range(H_B):
            qk = q_ref[:, (2 * H_A + h) * LANE:(2 * H_A + h + 1) * LANE] * gkb_ref[...]
            qx_s[h * SUB:(h + 1) * SUB, 0:2 * LANE] = jnp.dot(qk, wukt_ref[h], preferred_element_type=F32)
            qx_s[h * SUB:(h + 1) * SUB, 2 * LANE:3 * LANE] = jnp.where(lane < ROPE_B, pltpu.roll(qk, NOPE_B, 1), 0.0)

    def process(n, causal):
        nt = (((1,), (1,)), ((), ()))
        n_all = n_rows + n_rows_b
        qa = qa_s[...].astype(BF16)
        s_a = [lax.dot_general(qa[kv * n_rows_b:(kv + 1) * n_rows_b], ka_s[0:n, kv * LANE:(kv + 1) * LANE], nt,
                               preferred_element_type=F32) for kv in range(KV_A)]
        latb = lat_s[0:n, :]
        qx = qx_s[...].astype(BF16)
        s_raw = (lax.dot_general(qx[:, 0:2 * LANE], latb, nt, preferred_element_type=F32)
                 + jnp.dot(qx[:, 2 * LANE:2 * LANE + ROPE_B], krt_s[:, 0:n], preferred_element_type=F32))
        kn = jnp.dot(latb, wuk_ref[...], preferred_element_type=F32)
        seg = (lax.broadcasted_iota(I32, (H_B, H_B * NOPE_B), 1) // NOPE_B
               == lax.broadcasted_iota(I32, (H_B, H_B * NOPE_B), 0)).astype(BF16)
        n2 = lax.dot_general(seg, (kn * kn).astype(BF16), nt, preferred_element_type=F32)
        n2 = n2 + jnp.dot(jnp.ones((H_B, ROPE_B), BF16), krq_s[:, 0:n], preferred_element_type=F32)
        rinv = lax.rsqrt(n2 * (1.0 / QK_B) + EPS)
        s_b = (s_raw.reshape(H_B, SUB, n) * rinv[:, None, :]).reshape(n_rows_b, n)
        s = jnp.concatenate(s_a + [s_b], axis=0)
        if causal:
            row = lax.broadcasted_iota(I32, (n_all, n), 0)
            col = lax.broadcasted_iota(I32, (n_all, n), 1)
            s = jnp.where(col <= (row & (SUB - 1)), s, NEG_INF)
        m_new, alpha, p, row_sum = _softmax_step(s, m_s[...])
        l_s[...] = alpha * l_s[...] + row_sum
        m_s[...] = m_new
        p = p.astype(BF16)
        pv_a = [jnp.dot(p[kv * n_rows_b:(kv + 1) * n_rows_b], va_s[0:n, kv * LANE:(kv + 1) * LANE],
                        preferred_element_type=F32) for kv in range(KV_A)]
        acca_s[...] = alpha[0:n_rows] * acca_s[...] + jnp.concatenate(pv_a, axis=0)
        alpha_b = alpha[n_rows:n_all]
        accb_s[...] = (jnp.concatenate([alpha_b, alpha_b], axis=1) * accb_s[...]
                       + jnp.dot(p[n_rows:n_all], latb, preferred_element_type=F32))

    for i in range(pps):
        r = pl.ds(i * PAGE, PAGE)
        for kv in range(KV_A):
            ka_s[r, kv * LANE:(kv + 1) * LANE] = ck[i][0, pl.ds(kv, PAGE, stride=KV_A), :].astype(BF16)
            va_s[r, kv * LANE:(kv + 1) * LANE] = cv[i][0, pl.ds(kv, PAGE, stride=KV_A), :].astype(BF16)
        lat_s[r, :] = cl[i][0].astype(BF16)
        kr = cr[i][0]
        krt_s[:, i * PAGE:(i + 1) * PAGE] = kr.astype(BF16)
        krq_s[:, i * PAGE:(i + 1) * PAGE] = (kr * kr).astype(BF16)
    process(pps * PAGE, False)

    @pl.when(c == n_chunks - 1)
    def _():
        pad = lambda x: jnp.concatenate([x, jnp.zeros((PAGE - SUB, x.shape[1]), F32)], axis=0)
        r = pl.ds(0, PAGE)
        ka_s[r, :] = pad(kan_ref[...]).astype(BF16)
        va_s[r, :] = pad(van_ref[...]).astype(BF16)
        lat_s[r, :] = pad(latn_ref[...]).astype(BF16)
        kr = krn_ref[0]
        krt_s[:, 0:PAGE] = kr.astype(BF16)
        krq_s[:, 0:PAGE] = (kr * kr).astype(BF16)
        process(PAGE, True)
        lam = _diff_lambda(lam_ref)
        oa = acca_s[...] / l_s[0:n_rows, :]
        for h in range(H_A):
            d = oa[2 * h * SUB:(2 * h + 1) * SUB] - lam * oa[(2 * h + 1) * SUB:(2 * h + 2) * SUB]
            o_ref[:, h * LANE:(h + 1) * LANE] = ((d * _rms(d, 2 * DH_A)) * sub_ref[...]) * (1.0 - LAMBDA_INIT)
        l_b = l_s[n_rows:n_rows + n_rows_b, :]
        ob = accb_s[...] / jnp.concatenate([l_b, l_b], axis=1)
        for h in range(H_B):
            o_ref[:, (H_A + h) * LANE:(H_A + h + 1) * LANE] = jnp.dot(
                ob[h * SUB:(h + 1) * SUB], wuv_ref[:, h * LANE:(h + 1) * LANE], preferred_element_type=F32)


def _decode0(page_table, q_s, ka_s, va_s, lat_s, kr_s, caches, wts, gkb, lam_p, subln):
    nb, n_pages = page_table.shape
    pps = PAGES_PER_STEP
    n_chunks = n_pages // pps
    n = pps * PAGE
    ck, cv, cl, cr = caches
    wukt, wuk, wuv = wts
    rowmap = lambda b, c, pt: (b, 0)
    const = lambda a: pl.BlockSpec(a.shape, lambda b, c, pt: (0,) * a.ndim)

    def page_specs(rows, width):
        return [pl.BlockSpec((1, rows, width), functools.partial(lambda b, c, pt, i: (pt[b, c * pps + i], 0, 0), i=i))
                for i in range(pps)]

    in_specs = ([pl.BlockSpec((SUB, N_QS * LANE), rowmap), pl.BlockSpec((SUB, 2 * LANE), rowmap),
                 pl.BlockSpec((SUB, 2 * LANE), rowmap), pl.BlockSpec((SUB, 2 * LANE), rowmap),
                 pl.BlockSpec((1, ROPE_B, PAGE), lambda b, c, pt: (b, 0, 0)), const(wukt), const(wuk), const(wuv),
                 const(gkb), const(lam_p), const(subln)]
                + page_specs(KV_A * PAGE, LANE) + page_specs(KV_A * PAGE, LANE)
                + page_specs(PAGE, 2 * LANE) + page_specs(ROPE_B, PAGE))
    grid_spec = pltpu.PrefetchScalarGridSpec(
        num_scalar_prefetch=1,
        grid=(nb, n_chunks),
        in_specs=in_specs,
        out_specs=pl.BlockSpec((SUB, D), rowmap),
        scratch_shapes=[pltpu.VMEM((n, 2 * LANE), BF16), pltpu.VMEM((n, 2 * LANE), BF16),
                        pltpu.VMEM((n, 2 * LANE), BF16), pltpu.VMEM((ROPE_B, n), BF16), pltpu.VMEM((ROPE_B, n), BF16),
                        pltpu.VMEM((2 * H_A * SUB, LANE), F32), pltpu.VMEM((H_B * SUB, 3 * LANE), F32),
                        pltpu.VMEM(((2 * H_A + H_B) * SUB, LANE), F32), pltpu.VMEM(((2 * H_A + H_B) * SUB, LANE), F32),
                        pltpu.VMEM((2 * H_A * SUB, LANE), F32), pltpu.VMEM((H_B * SUB, 2 * LANE), F32)])
    return pl.pallas_call(
        functools.partial(_decode0_kernel, n_chunks=n_chunks),
        grid_spec=grid_spec,
        out_shape=jax.ShapeDtypeStruct((nb * SUB, D), F32),
        compiler_params=_cparams(("parallel", "arbitrary"), VMEM_BIG),
        name="decode0",
    )(page_table, q_s, ka_s, va_s, lat_s, kr_s, wukt, wuk, wuv, gkb, lam_p, subln,
      *([ck] * pps), *([cv] * pps), *([cl] * pps), *([cr] * pps))


def _proj1_post_kernel(z_ref, tk_ref, gq_ref, gk_ref, q_ref, k_ref, kf_ref, v_ref):
    j = pl.program_id(1)
    tk = tk_ref[...]
    nh = N_DIL * H_C

    @pl.when(j == 0)
    def _():
        for h in range(nh):
            x = z_ref[:, h * LANE:(h + 1) * LANE]
            y = _rope((x * _rms(x, DH_C)) * gq_ref[...], tk, DH_C // 8)
            q_ref[:, h * LANE:(h + 1) * LANE] = (y * (DH_C ** -0.5)).astype(BF16)

    @pl.when(j == 1)
    def _():
        for h in range(nh):
            x = z_ref[:, h * LANE:(h + 1) * LANE]
            y = _rope((x * _rms(x, DH_C)) * gk_ref[...], tk, DH_C // 8)
            kf_ref[:, h * LANE:(h + 1) * LANE] = y
            k_ref[:, h * LANE:(h + 1) * LANE] = y.astype(BF16)

    @pl.when(j == 2)
    def _():
        v_ref[...] = z_ref[...].astype(BF16)


def _proj1_views_kernel(z_ref, tk_ref, gq_ref, gk_ref, kf_ref, *rest):
    outs, scr = rest[:3 * N_DIL], rest[3 * N_DIL]
    j = pl.program_id(1)
    tk = tk_ref[...]
    tm = z_ref.shape[0]
    w = H_C * DH_C
    nh = N_DIL * H_C

    def emit(part, h, y):
        g, hh = divmod(h, H_C)
        d = DIL_GROUPS[g][1]
        o_ref = outs[part * N_DIL + g]
        if d == 1:
            o_ref[0, :, hh * LANE:(hh + 1) * LANE] = y.astype(BF16)
        else:
            scr[...] = y
            for r in range(d):
                o_ref[0, :, r * w + hh * LANE:r * w + (hh + 1) * LANE] = (
                    scr[pl.ds(r, tm // d, stride=d), :].astype(BF16))

    @pl.when(j == 0)
    def _():
        for h in range(nh):
            x = z_ref[:, h * LANE:(h + 1) * LANE]
            y = _rope((x * _rms(x, DH_C)) * gq_ref[...], tk, DH_C // 8)
            emit(0, h, y * (DH_C ** -0.5))

    @pl.when(j == 1)
    def _():
        for h in range(nh):
            x = z_ref[:, h * LANE:(h + 1) * LANE]
            y = _rope((x * _rms(x, DH_C)) * gk_ref[...], tk, DH_C // 8)
            kf_ref[:, h * LANE:(h + 1) * LANE] = y
            emit(1, h, y)

    @pl.when(j == 2)
    def _():
        for h in range(nh):
            emit(2, h, z_ref[:, h * LANE:(h + 1) * LANE])


def _proj1_views(z1, tk, gq, gk, *, nb, seq):
    tm = 256
    w = H_C * DH_C
    per_b = seq // tm
    out_shape = [jax.ShapeDtypeStruct((nb * seq, N_DIL * w), F32)]
    out_specs = [pl.BlockSpec((tm, N_DIL * w), lambda i, j: (i, 0))]
    for _ in range(3):
        for _, d in DIL_GROUPS:
            out_shape.append(jax.ShapeDtypeStruct((nb, seq // d, d * w), BF16))
            out_specs.append(pl.BlockSpec((1, tm // d, d * w), lambda i, j: (i // per_b, i % per_b, 0)))
    res = pl.pallas_call(
        _proj1_views_kernel,
        grid=(nb * seq // tm, 3),
        in_specs=[pl.BlockSpec((tm, N_DIL * w), lambda i, j: (i, j)),
                  pl.BlockSpec((tm, 3 * LANE), lambda i, j: (i, 0)),
                  pl.BlockSpec((1, LANE), lambda i, j: (0, 0)),
                  pl.BlockSpec((1, LANE), lambda i, j: (0, 0))],
        out_specs=out_specs,
        out_shape=out_shape,
        scratch_shapes=[pltpu.VMEM((tm, LANE), F32)],
        compiler_params=_cparams(("parallel", "arbitrary")),
        name="proj1_views",
    )(z1, tk, gq, gk)
    return res[0], res[1:1 + N_DIL], res[1 + N_DIL:1 + 2 * N_DIL], res[1 + 2 * N_DIL:]


def _proj1_post(z1, tk, gq, gk, *, n_rows, row0):
    tm = 256
    w = N_DIL * H_C * DH_C
    r0 = row0 // tm
    rows = lambda i, j: (i, 0)
    return pl.pallas_call(
        _proj1_post_kernel,
        grid=(n_rows // tm, 3),
        in_specs=[pl.BlockSpec((tm, w), lambda i, j: (r0 + i, j)),
                  pl.BlockSpec((tm, 3 * LANE), lambda i, j: (r0 + i, 0)),
                  pl.BlockSpec((1, LANE), lambda i, j: (0, 0)),
                  pl.BlockSpec((1, LANE), lambda i, j: (0, 0))],
        out_specs=[pl.BlockSpec((tm, w), rows)] * 4,
        out_shape=[jax.ShapeDtypeStruct((n_rows, w), BF16), jax.ShapeDtypeStruct((n_rows, w), BF16),
                   jax.ShapeDtypeStruct((n_rows, w), F32), jax.ShapeDtypeStruct((n_rows, w), BF16)],
        compiler_params=_cparams(("parallel", "arbitrary")),
        name="proj1_post",
    )(z1, tk, gq, gk)


def _dil_prompt_kernel(q_ref, kc_ref, kp_ref, vc_ref, vp_ref, o_ref, lse_ref, *, tq):
    qi = pl.program_id(2)
    nt = (((1,), (1,)), ((), ()))
    heads = [slice(h * LANE, (h + 1) * LANE) for h in range(H_C)]
    n_rows = H_C * tq
    s = jnp.concatenate(
        [jnp.concatenate([lax.dot_general(q_ref[0, :, sl], kc_ref[0, :, sl], nt, preferred_element_type=F32),
                          lax.dot_general(q_ref[0, :, sl], kp_ref[0, :, sl], nt, preferred_element_type=F32)], axis=1)
         for sl in heads], axis=0)
    r = lax.broadcasted_iota(I32, (n_rows, 2 * tq), 0) & (tq - 1)
    c = lax.broadcasted_iota(I32, (n_rows, 2 * tq), 1)
    keep = jnp.logical_or(c <= r, jnp.logical_and(c - tq >= r, qi > 0))
    m, _, p, row_sum = _softmax_step(jnp.where(keep, s, NEG_INF), jnp.full((n_rows, LANE), NEG_INF, F32), exp=jnp.exp)
    l = jnp.zeros((n_rows, LANE), F32) + row_sum
    p = p.astype(BF16)
    lse = m + jnp.log(l)
    lane = lax.broadcasted_iota(I32, (tq, LANE), 1)
    lse_tile = jnp.zeros((tq, LANE), F32)
    for h, sl in enumerate(heads):
        rows = slice(h * tq, (h + 1) * tq)
        o = (jnp.dot(p[rows, 0:tq], vc_ref[0, :, sl], preferred_element_type=F32)
             + jnp.dot(p[rows, tq:2 * tq], vp_ref[0, :, sl], preferred_element_type=F32))
        o_ref[0, :, sl] = (o / l[rows]).astype(o_ref.dtype)
        lse_tile = jnp.where(lane == h, lse[rows], lse_tile)
    lse_ref[0] = lse_tile


def _dil_prompt(q, k, v, g, dil, nb, seq):
    tq = PAGE
    w = H_C * DH_C
    ns = seq // dil
    view = lambda a: a
    cur = lambda b, r, i: (b, i, r)
    prev = lambda b, r, i: (b, jnp.maximum(i - 1, 0), r)
    blk = lambda m: pl.BlockSpec((1, tq, w), m)
    o, lse = pl.pallas_call(
        functools.partial(_dil_prompt_kernel, tq=tq),
        grid=(nb, dil, ns // tq),
        in_specs=[blk(cur), blk(cur), blk(prev), blk(cur), blk(prev)],
        out_specs=[pl.BlockSpec((1, tq, w), lambda b, r, i: (b, i, r)),
                   pl.BlockSpec((1, tq, LANE), lambda b, r, i: (b, i, r))],
        out_shape=[jax.ShapeDtypeStruct((nb, ns, dil * w), BF16), jax.ShapeDtypeStruct((nb, ns, dil * LANE), F32)],
        compiler_params=_cparams(("parallel", "parallel", "parallel")),
        name=f"dil_prompt{g}",
    )(view(q), view(k), view(k), view(v), view(v))
    return o.reshape(nb * seq, w), lse.reshape(nb * seq, LANE)


def _dil_sample_kernel(q_ref, kn_ref, vn_ref, st_ref, tail_ref, new_ref, o_ref, lse_ref, ns_ref, m_s, l_s, acc_s,
                       *, g, dil, ch, n_chunks):
    c = pl.program_id(1)
    nt = (((1,), (1,)), ((), ()))
    w = H_C * DH_C
    rpw = 2 * H_C
    body = (ch - SUB) * rpw
    half = dil == 2 * SUB
    n_slab = ch // dil

    if half:
        ns_ref[0, :, 0:SUB * rpw, :] = st_ref[0, :, SUB * rpw:dil * rpw, :]
        if n_slab > 1:
            ns_ref[0, 0:n_slab - 1, SUB * rpw:dil * rpw, :] = st_ref[0, 1:n_slab, 0:SUB * rpw, :]
    else:
        ns_ref[0, 0:body] = st_ref[0, SUB * rpw:ch * rpw]

    @pl.when(c < n_chunks - 1)
    def _():
        if half:
            ns_ref[0, n_slab - 1, SUB * rpw:dil * rpw, :] = tail_ref[0, 0]
        else:
            ns_ref[0, body:ch * rpw] = tail_ref[0]

    @pl.when(c == n_chunks - 1)
    def _():
        if half:
            ns_ref[0, n_slab - 1, SUB * rpw:dil * rpw, :] = new_ref[0, 0]
        else:
            ns_ref[0, body:ch * rpw] = new_ref[0]

    @pl.when(c == 0)
    def _():
        m_s[...] = jnp.full(m_s.shape, NEG_INF, F32)
        l_s[...] = jnp.zeros(l_s.shape, F32)
        acc_s[...] = jnp.zeros(acc_s.shape, F32)

    n_rows = H_C * SUB
    heads = [slice(g * w + h * LANE, g * w + (h + 1) * LANE) for h in range(H_C)]

    def update(keys, vals, keep):
        s = jnp.concatenate([lax.dot_general(q_ref[:, heads[h]], keys[h], nt, preferred_element_type=F32)
                             for h in range(H_C)], axis=0)
        m_new, alpha, p, row_sum = _softmax_step(jnp.where(keep, s, NEG_INF), m_s[...], exp=jnp.exp)
        l_s[...] = alpha * l_s[...] + row_sum
        m_s[...] = m_new
        pv = [jnp.dot(p[h * SUB:(h + 1) * SUB], vals[h], preferred_element_type=F32) for h in range(H_C)]
        acc_s[...] = alpha * acc_s[...] + jnp.concatenate(pv, axis=0)

    if half:
        nk = n_slab * SUB
        t = lax.broadcasted_iota(I32, (n_rows, nk), 0) & (SUB - 1)
        j = lax.broadcasted_iota(I32, (n_rows, nk), 1)
        update([st_ref[0, :, pl.ds(h, SUB, stride=rpw), :].reshape(nk, LANE) for h in range(H_C)],
               [st_ref[0, :, pl.ds(H_C + h, SUB, stride=rpw), :].reshape(nk, LANE) for h in range(H_C)],
               (j & (SUB - 1)) == t)
    else:
        t = lax.broadcasted_iota(I32, (n_rows, ch), 0) & (SUB - 1)
        i = lax.broadcasted_iota(I32, (n_rows, ch), 1) + c * ch
        update([st_ref[0, pl.ds(h, ch, stride=rpw), :] for h in range(H_C)],
               [st_ref[0, pl.ds(H_C + h, ch, stride=rpw), :] for h in range(H_C)],
               jnp.logical_and(i >= t, ((i - t) & (dil - 1)) == 0))

    @pl.when(c == n_chunks - 1)
    def _():
        tt = lax.broadcasted_iota(I32, (n_rows, PAGE), 0) & (SUB - 1)
        tn = lax.broadcasted_iota(I32, (n_rows, PAGE), 1)
        pad = lambda x: jnp.concatenate([x, jnp.zeros((PAGE - SUB, LANE), F32)], axis=0)
        update([pad(kn_ref[:, heads[h]]) for h in range(H_C)], [pad(vn_ref[:, heads[h]]) for h in range(H_C)],
               jnp.logical_and(tn <= tt, ((tt - tn) & (dil - 1)) == 0))
        lane = lax.broadcasted_iota(I32, (SUB, LANE), 1)
        lse_tile = jnp.zeros((SUB, LANE), F32)
        for h in range(H_C):
            rows = pl.ds(h * SUB, SUB)
            l = l_s[rows, :]
            o_ref[:, h * LANE:(h + 1) * LANE] = acc_s[rows, :] / l
            lse_tile = jnp.where(lane == h, m_s[rows, :] + jnp.log(l), lse_tile)
        lse_ref[...] = lse_tile


def _dil_sample(q_s, kf_s, z1, state, new_rows, g, dil, row0):
    nb, win = state.shape[0], state.shape[1]
    w = H_C * DH_C
    rpw = 2 * H_C
    ch = min(win, 512)
    n_chunks = win // ch
    wq = N_DIL * w
    r0 = row0 // SUB
    if dil == 2 * SUB:
        n_slab = ch // dil
        st_view = state.reshape(nb, win // dil, dil * rpw, LANE)
        st_spec = pl.BlockSpec((1, n_slab, dil * rpw, LANE), lambda b, c: (b, c, 0, 0))
        tail_spec = pl.BlockSpec((1, 1, SUB * rpw, LANE),
                                 lambda b, c: (b, jnp.minimum((c + 1) * n_slab, win // dil - 1), 0, 0))
        new_view = new_rows.reshape(nb, 1, SUB * rpw, LANE)
        new_spec = pl.BlockSpec((1, 1, SUB * rpw, LANE), lambda b, c: (b, 0, 0, 0))
    else:
        tail_blocks = ch // SUB
        last_tail = win // SUB - 1
        st_view = state.reshape(nb, win * rpw, LANE)
        st_spec = pl.BlockSpec((1, ch * rpw, LANE), lambda b, c: (b, c, 0))
        tail_spec = pl.BlockSpec((1, SUB * rpw, LANE),
                                 lambda b, c: (b, jnp.minimum((c + 1) * tail_blocks, last_tail), 0))
        new_view = new_rows.reshape(nb, SUB * rpw, LANE)
        new_spec = pl.BlockSpec((1, SUB * rpw, LANE), lambda b, c: (b, 0, 0))
    o, lse, ns = pl.pallas_call(
        functools.partial(_dil_sample_kernel, g=g, dil=dil, ch=ch, n_chunks=n_chunks),
        grid=(nb, n_chunks),
        in_specs=[pl.BlockSpec((SUB, wq), lambda b, c: (b, 0)),
                  pl.BlockSpec((SUB, wq), lambda b, c: (b, 0)),
                  pl.BlockSpec((SUB, wq), lambda b, c: (r0 + b, 2)),
                  st_spec, tail_spec, new_spec],
        out_specs=[pl.BlockSpec((SUB, w), lambda b, c: (b, 0)), pl.BlockSpec((SUB, LANE), lambda b, c: (b, 0)),
                   st_spec],
        out_shape=[jax.ShapeDtypeStruct((nb * SUB, w), F32), jax.ShapeDtypeStruct((nb * SUB, LANE), F32),
                   jax.ShapeDtypeStruct(st_view.shape, F32)],
        scratch_shapes=[pltpu.VMEM((H_C * SUB, LANE), F32), pltpu.VMEM((H_C * SUB, LANE), F32),
                        pltpu.VMEM((H_C * SUB, LANE), F32)],
        compiler_params=_cparams(("parallel", "arbitrary")),
        name=f"dil_sample{g}",
    )(q_s, kf_s, z1, st_view, st_view, new_view)
    return o, lse, ns.reshape(state.shape)


def _dil_combine_kernel(o0_ref, o1_ref, o2_ref, l0_ref, l1_ref, l2_ref, a_ref):
    l0, l1, l2 = l0_ref[...], l1_ref[...], l2_ref[...]
    m = jnp.maximum(jnp.maximum(l0, l1), l2)
    w0, w1, w2 = jnp.exp(l0 - m), jnp.exp(l1 - m), jnp.exp(l2 - m)
    den = w0 + w1 + w2
    w0, w1, w2 = w0 / den, w1 / den, w2 / den
    for h in range(H_C):
        sl = slice(h * LANE, (h + 1) * LANE)
        a_ref[:, sl] = (w0[:, h:h + 1] * o0_ref[:, sl].astype(F32) + w1[:, h:h + 1] * o1_ref[:, sl].astype(F32)
                        + w2[:, h:h + 1] * o2_ref[:, sl].astype(F32)).astype(a_ref.dtype)


def _dil_combine(outs, lses, out_dtype, tm):
    n = outs[0].shape[0]
    w = H_C * DH_C
    rows = lambda i: (i, 0)
    return pl.pallas_call(
        _dil_combine_kernel,
        grid=(n // tm,),
        in_specs=[pl.BlockSpec((tm, w), rows)] * 3 + [pl.BlockSpec((tm, LANE), rows)] * 3,
        out_specs=pl.BlockSpec((tm, w), rows),
        out_shape=jax.ShapeDtypeStruct((n, w), out_dtype),
        compiler_params=_cparams(("parallel",)),
        name="dil_combine",
    )(*outs, *lses)


def _router_kernel(h_ref, r_ref, idx_ref, gate_ref):
    logits = jnp.dot(h_ref[...], r_ref[...].astype(BF16), preferred_element_type=F32)
    lane = lax.broadcasted_iota(I32, logits.shape, 1)
    lanef = lane.astype(F32)
    lg = jnp.where(lane < N_EXPERTS, logits, -jnp.inf)
    m1 = jnp.max(lg, axis=1, keepdims=True)
    i1 = jnp.min(jnp.where(lg == m1, lanef, float(LANE)), axis=1, keepdims=True)
    lg2 = jnp.where(lanef == i1, -jnp.inf, lg)
    m2 = jnp.max(lg2, axis=1, keepdims=True)
    i2 = jnp.min(jnp.where(lg2 == m2, lanef, float(LANE)), axis=1, keepdims=True)
    e = jnp.exp(m2 - m1)
    g1 = 1.0 / (1.0 + e)
    g2 = e / (1.0 + e)
    idx_ref[...] = jnp.where(lane == 0, i1, jnp.where(lane == 1, i2, 0.0)).astype(I32)
    gate_ref[...] = jnp.where(lane == 0, g1, jnp.where(lane == 1, g2, 0.0))


def _router(h, router):
    t = h.shape[0]
    tm = 512
    rp = jnp.pad(router, ((0, 0), (0, LANE - N_EXPERTS)))
    return pl.pallas_call(
        _router_kernel,
        grid=(t // tm,),
        in_specs=[pl.BlockSpec((tm, D), lambda i: (i, 0)), pl.BlockSpec((D, LANE), lambda i: (0, 0))],
        out_specs=[pl.BlockSpec((tm, LANE), lambda i: (i, 0))] * 2,
        out_shape=[jax.ShapeDtypeStruct((t, LANE), I32), jax.ShapeDtypeStruct((t, LANE), F32)],
        compiler_params=_cparams(("parallel",)),
        name="router",
    )(h, rp)


def _moe_gather_kernel(tok_ref, h_hbm, o_ref, buf, sem, *, tm):
    i = pl.program_id(0)
    n = pl.num_programs(0)

    def row_copy(tile, slot, r):
        return pltpu.make_async_copy(h_hbm.at[pl.ds(tok_ref[tile * tm + r], 1)], buf.at[slot, pl.ds(r, 1)],
                                     sem.at[slot])

    def issue(tile, slot):
        def body(r, carry):
            row_copy(tile, slot, 2 * r).start(priority=0)
            row_copy(tile, slot, 2 * r + 1).start(priority=1)
            return carry
        lax.fori_loop(0, tm // 2, body, 0)

    @pl.when(i == 0)
    def _():
        issue(0, 0)

    @pl.when(i + 1 < n)
    def _():
        issue(i + 1, (i + 1) % 2)

    slot = i % 2

    def wait(r, carry):
        row_copy(i, slot, r).wait()
        return carry

    lax.fori_loop(0, tm, wait, 0)
    o_ref[...] = buf[slot].astype(BF16)


def _moe_gather(tok_of_slot, hf):
    p = tok_of_slot.shape[0]
    tm = MOE_TM
    grid_spec = pltpu.PrefetchScalarGridSpec(
        num_scalar_prefetch=1, grid=(p // tm,),
        in_specs=[pl.BlockSpec(memory_space=pl.ANY)],
        out_specs=pl.BlockSpec((tm, D), lambda i, tok: (i, 0)),
        scratch_shapes=[pltpu.VMEM((2, tm, D), F32), pltpu.SemaphoreType.DMA((2,))])
    return pl.pallas_call(
        functools.partial(_moe_gather_kernel, tm=tm),
        grid_spec=grid_spec,
        out_shape=jax.ShapeDtypeStruct((p, D), BF16),
        compiler_params=_cparams(("arbitrary",)),
        name="moe_gather",
    )(tok_of_slot, hf)


def _moe_up_kernel(te_ref, nu_ref, x_ref, wg_ref, wu_ref, o_ref, wgb_ref, wub_ref):
    i = pl.program_id(1)
    changed = jnp.logical_or(i == 0, te_ref[i] != te_ref[jnp.maximum(i - 1, 0)])

    @pl.when(changed)
    def _():
        wgb_ref[...] = wg_ref[0].astype(BF16)
        wub_ref[...] = wu_ref[0].astype(BF16)

    @pl.when(i < nu_ref[0])
    def _():
        x = x_ref[...]
        a = jnp.dot(x, wgb_ref[...], preferred_element_type=F32)
        b = jnp.dot(x, wub_ref[...], preferred_element_type=F32)
        o_ref[...] = ((a * jax.nn.sigmoid(a)) * b).astype(BF16)

    @pl.when(i >= nu_ref[0])
    def _():
        o_ref[...] = jnp.zeros(o_ref.shape, BF16)


def _moe_up(te, nu, xs, wg, wu, tn=512):
    p = xs.shape[0]
    tm = MOE_TM
    n = wg.shape[2]
    grid_spec = pltpu.PrefetchScalarGridSpec(
        num_scalar_prefetch=2, grid=(n // tn, p // tm),
        in_specs=[pl.BlockSpec((tm, D), lambda j, i, te, nu: (i, 0)),
                  pl.BlockSpec((1, D, tn), lambda j, i, te, nu: (te[i], 0, j)),
                  pl.BlockSpec((1, D, tn), lambda j, i, te, nu: (te[i], 0, j))],
        out_specs=pl.BlockSpec((tm, tn), lambda j, i, te, nu: (i, j)),
        scratch_shapes=[pltpu.VMEM((D, tn), BF16), pltpu.VMEM((D, tn), BF16)])
    return pl.pallas_call(
        _moe_up_kernel, grid_spec=grid_spec,
        out_shape=jax.ShapeDtypeStruct((p, n), BF16),
        compiler_params=_cparams(("arbitrary", "arbitrary"), VMEM_BIG),
        name="moe_up",
    )(te, nu, xs, wg, wu)


def _moe_down_kernel(te_ref, nu_ref, x_ref, w_ref, o_ref, wb_ref):
    i = pl.program_id(1)
    changed = jnp.logical_or(i == 0, te_ref[i] != te_ref[jnp.maximum(i - 1, 0)])

    @pl.when(changed)
    def _():
        wb_ref[...] = w_ref[0].astype(BF16)

    @pl.when(i < nu_ref[0])
    def _():
        o_ref[...] = jnp.dot(x_ref[...], wb_ref[...], preferred_element_type=F32)

    @pl.when(i >= nu_ref[0])
    def _():
        o_ref[...] = jnp.zeros(o_ref.shape, F32)


def _moe_down(te, nu, act, wd, tn=512):
    p, k = act.shape
    tm = MOE_TM_DOWN
    n = wd.shape[2]
    grid_spec = pltpu.PrefetchScalarGridSpec(
        num_scalar_prefetch=2, grid=(n // tn, p // tm),
        in_specs=[pl.BlockSpec((tm, k), lambda j, i, te, nu: (i, 0)),
                  pl.BlockSpec((1, k, tn), lambda j, i, te, nu: (te[i], 0, j))],
        out_specs=pl.BlockSpec((tm, tn), lambda j, i, te, nu: (i, j)),
        scratch_shapes=[pltpu.VMEM((k, tn), BF16)])
    return pl.pallas_call(
        _moe_down_kernel, grid_spec=grid_spec,
        out_shape=jax.ShapeDtypeStruct((p, n), F32),
        compiler_params=_cparams(("arbitrary", "arbitrary"), VMEM_BIG),
        name="moe_down",
    )(te, nu, act, wd)


def _moe_combine_kernel(slot_ref, ys_hbm, x_ref, gt_ref, g_ref, o_ref, buf_a, buf_b, sem, *, tm):
    base = pl.program_id(0) * tm

    def issue(r, carry):
        pltpu.make_async_copy(ys_hbm.at[pl.ds(slot_ref[2 * (base + r)], 1)], buf_a.at[pl.ds(r, 1)], sem).start(priority=0)
        pltpu.make_async_copy(ys_hbm.at[pl.ds(slot_ref[2 * (base + r) + 1], 1)], buf_b.at[pl.ds(r, 1)], sem).start(priority=1)
        return carry

    def wait(r, carry):
        pltpu.make_async_copy(ys_hbm.at[pl.ds(0, 1)], buf_a.at[pl.ds(r, 1)], sem).wait()
        pltpu.make_async_copy(ys_hbm.at[pl.ds(0, 1)], buf_b.at[pl.ds(r, 1)], sem).wait()
        return carry

    lax.fori_loop(0, tm, issue, 0)
    lax.fori_loop(0, tm, wait, 0)
    gt = gt_ref[...]
    y = gt[:, 0:1] * buf_a[...] + gt[:, 1:2] * buf_b[...]
    o_ref[...] = x_ref[...] + g_ref[...] * y.reshape(o_ref.shape)


def _moe_combine(slot_of_assign, ys, x, gates, mod, k_gate):
    t = x.shape[0]
    tm = 256
    gb = tm // SUB
    grid_spec = pltpu.PrefetchScalarGridSpec(
        num_scalar_prefetch=1, grid=(t // tm,),
        in_specs=[pl.BlockSpec(memory_space=pl.ANY),
                  pl.BlockSpec((gb, SUB, D), lambda i, sl: (i, 0, 0)),
                  pl.BlockSpec((tm, LANE), lambda i, sl: (i, 0)),
                  pl.BlockSpec((gb, 1, D), lambda i, sl: (i, 0, k_gate))],
        out_specs=pl.BlockSpec((gb, SUB, D), lambda i, sl: (i, 0, 0)),
        scratch_shapes=[pltpu.VMEM((tm, D), F32), pltpu.VMEM((tm, D), F32), pltpu.SemaphoreType.DMA(())])
    out = pl.pallas_call(
        functools.partial(_moe_combine_kernel, tm=tm),
        grid_spec=grid_spec,
        out_shape=jax.ShapeDtypeStruct((t // SUB, SUB, D), F32),
        compiler_params=_cparams(("arbitrary",)),
        name="moe_combine",
    )(slot_of_assign, ys, x.reshape(t // SUB, SUB, D), gates, mod)
    return out.reshape(t, D)


def _moe_plan(idx, t):
    tm = MOE_TM
    n_assign = 2 * t
    p = n_assign + N_EXPERTS * tm
    e_flat = idx[:, 0:2].reshape(n_assign)
    onehot = (e_flat[:, None] == jnp.arange(N_EXPERTS, dtype=I32)[None, :]).astype(I32)
    csum = jnp.cumsum(onehot, axis=0)
    rank = jnp.take_along_axis(csum, e_flat[:, None], axis=1)[:, 0] - 1
    counts = csum[-1]
    padded = ((counts + tm - 1) // tm) * tm
    ends = jnp.cumsum(padded)
    slot = (ends - padded)[e_flat] + rank
    tok_of_slot = jnp.zeros((p,), I32).at[slot].set(jnp.arange(n_assign, dtype=I32) // 2)
    tile_start = jnp.arange(p // tm, dtype=I32) * tm
    te = jnp.minimum(jnp.searchsorted(ends, tile_start, side="right").astype(I32), N_EXPERTS - 1)
    n_used = (ends[-1] // tm).astype(I32).reshape(1)
    return slot.astype(I32), tok_of_slot, te, n_used


def kernel(x_prompt, x_sample, c_prompt, c_sample, page_table, cache_a_k, cache_a_v, cache_b_lat, cache_b_krope,
           state_c_win0, state_c_win1, state_c_win2, ada_w, ada_b, norm_mix, norm_ffn, l0_w_in, l0_a_qnorm,
           l0_a_knorm, l0_a_lambda, l0_a_subln, l0_b_qa_norm, l0_b_w_uq, l0_b_kv_norm, l0_b_w_ukv, l0_b_qnorm,
           l0_b_knorm, l0_w_out, l0_ffn_gate, l0_ffn_up, l0_ffn_down, l1_w_in, l1_c_qnorm, l1_c_knorm, l1_w_out,
           l1_router, l1_moe_gate, l1_moe_up, l1_moe_down):
    nbp, seq, _ = x_prompt.shape
    nbs, dseq, _ = x_sample.shape
    assert dseq == SUB
    n_pages = page_table.shape[1]
    past = n_pages * PAGE
    tp, ts = nbp * seq, nbs * dseq
    t = tp + ts
    n_pool = cache_a_k.shape[0]

    nc = nbp + nbs
    ncp = -(-nc // SUB) * SUB
    c_all = jnp.pad(jnp.concatenate([c_prompt, c_sample], axis=0), ((0, ncp - nc), (0, 0)))
    mods = _adaln_all(c_all, ada_w, ada_b)
    rg_idx = jnp.concatenate([jnp.repeat(jnp.arange(nbp, dtype=I32), seq // SUB), nbp + jnp.arange(nbs, dtype=I32)])
    mod = [mods[l][rg_idx][:, None, :] for l in range(DEPTH)]

    x = jnp.concatenate([x_prompt.reshape(tp, D), x_sample.reshape(ts, D)], axis=0)
    pos = jnp.concatenate([jnp.tile(jnp.arange(seq, dtype=I32), nbp), jnp.tile(past + jnp.arange(dseq, dtype=I32), nbs)])
    tab_a = _rope_tables(pos, DH_A // 8, 0, DH_A)
    tab_k = _rope_tables(pos, ROPE_B // 2, 0, LANE)
    tab_b = _rope_tables(pos, ROPE_B // 2, NOPE_B, LANE)

    eye2 = jnp.eye(2, dtype=F32)
    w_qa = l0_w_in[:, :1024].reshape(D, H_A, 2, 1, DH_A) * eye2[None, None, :, :, None]
    w0 = jnp.concatenate([w_qa.reshape(D, 2 * H_A * LANE), l0_w_in[:, 1024:2336],
                          jnp.zeros((D, LANE - ROPE_B), F32)], axis=1)
    padl = lambda a, n: jnp.pad(a, ((0, 0),) * (a.ndim - 1) + ((0, n),))
    wuq = padl(l0_b_w_uq.reshape(Q_LORA, H_B, QK_B), LANE - QK_B).reshape(Q_LORA, H_B * LANE).astype(BF16)
    wukv = l0_b_w_ukv.reshape(KV_LORA, H_B, NOPE_B + VH_B)
    wuk_pad = padl(wukv[:, :, :NOPE_B], LANE - NOPE_B).reshape(KV_LORA, H_B * LANE).astype(BF16)
    wuk = wukv[:, :, :NOPE_B].reshape(KV_LORA, H_B * NOPE_B).astype(BF16)
    wuv = wukv[:, :, NOPE_B:].reshape(KV_LORA, H_B * VH_B).astype(BF16)
    wukt = jnp.pad(jnp.transpose(wukv[:, :, :NOPE_B], (1, 2, 0)), ((0, 0), (0, LANE - NOPE_B), (0, 0)))
    g_qa = jnp.tile(l0_a_qnorm, 2).reshape(1, LANE)
    g_ka = jnp.tile(l0_a_knorm, 2).reshape(1, LANE)
    g_qb = padl(l0_b_qnorm, LANE - QK_B).reshape(1, LANE)
    g_kb = padl(l0_b_knorm, LANE - QK_B).reshape(1, LANE)
    gains0 = (g_qa, g_ka, l0_b_qa_norm.reshape(1, Q_LORA), l0_b_kv_norm.reshape(1, KV_LORA), g_qb, g_kb)
    subln = l0_a_subln.reshape(1, LANE)

    h = _norm_mod(x, norm_mix[0], mod[0], 1, 0)
    z0 = _mm(h, w0, tm=1024, tn=384)
    tabs = (tab_a, tab_b, tab_k)
    q_p, ka_p, va_p, lat_p, kr_p, ks_p, vs_p = _proj0_post(
        z0, tabs, gains0, (wuq, wuk_pad, wuv), prompt=True, nb=nbp, seq=seq, row0=0)
    q_s, ka_s, va_s, lat_s, kr_s = _proj0_post(
        z0, tabs, gains0, (wuq, wuk_pad, wuv), prompt=False, nb=nbs, seq=dseq, row0=tp)
    o_p = _flash(q_p, ks_p, vs_p)
    a_p = _attn0_post(o_p, l0_a_lambda, subln)
    caches = (cache_a_k.reshape(n_pool, PAGE * KV_A, LANE), cache_a_v.reshape(n_pool, PAGE * KV_A, LANE),
              cache_b_lat, jnp.swapaxes(cache_b_krope, 1, 2))
    krt_new = jnp.pad(jnp.swapaxes(kr_s.reshape(nbs, dseq, ROPE_B), 1, 2), ((0, 0), (0, 0), (0, PAGE - dseq)))
    wuv_f = wukv[:, :, NOPE_B:].reshape(KV_LORA, H_B * VH_B)
    a_s = _decode0(page_table, q_s, ka_s, va_s, lat_s, krt_new, caches, (wukt, wuk, wuv_f), g_kb, l0_a_lambda, subln)
    a0 = jnp.concatenate([a_p, a_s.astype(BF16)], axis=0)
    x = _mm_res(a0, l0_w_out, x, mod[0], 2, tm=1024)
    h = _norm_mod(x, norm_ffn[0], mod[0], 4, 3)
    act = _mm_swiglu(h, l0_ffn_gate, l0_ffn_up, tm=1024)
    x = _mm_res(act, l0_ffn_down, x, mod[0], 5)

    h = _norm_mod(x, norm_mix[1], mod[1], 1, 0)
    z1 = _mm(h, l1_w_in, tm=1024, tn=512)
    gq1 = l1_c_qnorm.reshape(1, LANE)
    gk1 = l1_c_knorm.reshape(1, LANE)
    kf1p, q1p, k1p, v1p = _proj1_views(z1, tab_k, gq1, gk1, nb=nbp, seq=seq)
    q1s, _, kf1s, _ = _proj1_post(z1, tab_k, gq1, gk1, n_rows=ts, row0=tp)
    q1s_f = q1s.astype(F32)
    states = (state_c_win0, state_c_win1, state_c_win2)
    w = H_C * DH_C
    v1f = z1[:, 2 * N_DIL * w:]
    new_k = kf1s.reshape(nbs, dseq, N_DIL, H_C, DH_C)
    new_v = v1f[tp:].reshape(nbs, dseq, N_DIL, H_C, DH_C)
    outs_p, lses_p, outs_s, lses_s, win_s = [], [], [], [], []
    for g, (_, dil) in enumerate(DIL_GROUPS):
        o, lse = _dil_prompt(q1p[g], k1p[g], v1p[g], g, dil, nbp, seq)
        outs_p.append(o)
        lses_p.append(lse)
        new_rows = jnp.stack([new_k[:, :, g], new_v[:, :, g]], axis=2)
        o, lse, ns = _dil_sample(q1s_f, kf1s, z1, states[g], new_rows, g, dil, tp)
        outs_s.append(o)
        lses_s.append(lse)
        win_s.append(ns)
    a1 = jnp.concatenate([_dil_combine(outs_p, lses_p, BF16, 256),
                          _dil_combine(outs_s, lses_s, F32, 256).astype(BF16)], axis=0)
    x = _mm_res(a1, l1_w_out, x, mod[1], 2, tm=1024)
    h, hf = _norm_mod(x, norm_ffn[1], mod[1], 4, 3, want_f32=True)
    idx, gates = _router(h, l1_router)
    slot, tok_of_slot, te, n_used = _moe_plan(idx, t)
    xs = _moe_gather(tok_of_slot, hf)
    act = _moe_up(te, n_used, xs, l1_moe_gate, l1_moe_up)
    sub_tiles = MOE_TM // MOE_TM_DOWN
    ys = _moe_down(jnp.repeat(te, sub_tiles), n_used * sub_tiles, act, l1_moe_down)
    x = _moe_combine(slot, ys, x, gates, mod[1], 5)

    y_p = x[:tp].reshape(nbp, seq, D)
    y_s = x[tp:].reshape(nbs, dseq, D)
    win_p = []
    for g, (win, _) in enumerate(DIL_GROUPS):
        wl = min(win, seq)
        kk = kf1p.reshape(nbp, seq, N_DIL, H_C, DH_C)[:, seq - wl:, g]
        vv = v1f[:tp].reshape(nbp, seq, N_DIL, H_C, DH_C)[:, seq - wl:, g]
        win_p.append(jnp.stack([kk, vv], axis=2))
    return (y_p, y_s,
            ka_p.reshape(nbp, seq, KV_A, 2 * DH_A), ka_s.reshape(nbs, dseq, KV_A, 2 * DH_A),
            va_p.reshape(nbp, seq, KV_A, 2 * DH_A), va_s.reshape(nbs, dseq, KV_A, 2 * DH_A),
            lat_p.reshape(nbp, seq, KV_LORA), lat_s.reshape(nbs, dseq, KV_LORA),
            kr_p.reshape(nbp, seq, ROPE_B), kr_s.reshape(nbs, dseq, ROPE_B),
            win_p[0], win_s[0], win_p[1], win_s[1], win_p[2], win_s[2])
```

```python
import functools

import jax
import jax.numpy as jnp
from jax import lax
from jax.experimental import pallas as pl
from jax.experimental.pallas import tpu as pltpu

F32 = jnp.float32
BF16 = jnp.bfloat16
I32 = jnp.int32

D = 2048
DEPTH = 2
PAGE = 128
ROPE_THETA = 500000.0
EPS = 1e-6
NEG_INF = -1e30

H_A, KV_A, DH_A = 8, 2, 64
LAMBDA_INIT = 0.2
H_B, Q_LORA, KV_LORA, NOPE_B, ROPE_B, VH_B = 8, 512, 256, 64, 32, 128
QK_B = NOPE_B + ROPE_B
H_C, DH_C = 8, 128
DIL_GROUPS = ((128, 1), (512, 4), (2048, 16))
N_DIL = 3
D_FF = 5632
N_EXPERTS = 8
D_FF_E = 7168
OUT1 = H_C * DH_C

LANE = 128
SUB = 8
VMEM_BIG = 56 * 1024 * 1024

N_QS = 24
N_KS = 10
Z0_W = 27 * LANE
PAGES_PER_STEP = 16
MOE_TM = 512
MOE_TM_DOWN = 256
LOG2E = 1.4426950408889634


def _cparams(sem, vmem=None):
    return pltpu.CompilerParams(dimension_semantics=sem, vmem_limit_bytes=vmem)


def _adaln_kernel(c_ref, w_ref, b_ref, o_ref):
    c = c_ref[...]
    a = (c * jax.nn.sigmoid(c)).astype(BF16)
    o_ref[0] = jnp.dot(a, w_ref[0].astype(BF16), preferred_element_type=F32) + b_ref[0]


def _adaln_all(c_all, ada_w, ada_b):
    nb = c_all.shape[0]
    tn = 1024
    return pl.pallas_call(
        _adaln_kernel,
        grid=(DEPTH, 6 * D // tn),
        in_specs=[pl.BlockSpec((nb, D), lambda l, j: (0, 0)),
                  pl.BlockSpec((1, D, tn), lambda l, j: (l, 0, j)),
                  pl.BlockSpec((1, 1, tn), lambda l, j: (l, 0, j))],
        out_specs=pl.BlockSpec((1, nb, tn), lambda l, j: (l, 0, j)),
        out_shape=jax.ShapeDtypeStruct((DEPTH, nb, 6 * D), F32),
        compiler_params=_cparams(("parallel", "parallel")),
        name="adaln",
    )(c_all, ada_w, ada_b.reshape(DEPTH, 1, 6 * D))


def _norm_mod_kernel(x_ref, g_ref, sc_ref, sh_ref, o_ref, *of_ref):
    x = x_ref[...]
    ms = jnp.mean(x * x, axis=-1, keepdims=True)
    y = (x * lax.rsqrt(ms + EPS)) * g_ref[...]
    y = y * (1.0 + sc_ref[...]) + sh_ref[...]
    y2 = y.reshape(o_ref.shape)
    o_ref[...] = y2.astype(BF16)
    if of_ref:
        of_ref[0][...] = y2


def _norm_mod(x, gain, mod, k_sc, k_sh, want_f32=False):
    t = x.shape[0]
    gb = 32
    tm = gb * SUB
    out_shape = [jax.ShapeDtypeStruct((t, D), BF16)]
    out_specs = [pl.BlockSpec((tm, D), lambda i: (i, 0))]
    if want_f32:
        out_shape.append(jax.ShapeDtypeStruct((t, D), F32))
        out_specs.append(pl.BlockSpec((tm, D), lambda i: (i, 0)))
    res = pl.pallas_call(
        _norm_mod_kernel,
        grid=(t // tm,),
        in_specs=[pl.BlockSpec((gb, SUB, D), lambda i: (i, 0, 0)),
                  pl.BlockSpec((1, 1, D), lambda i: (0, 0, 0)),
                  pl.BlockSpec((gb, 1, D), lambda i: (i, 0, k_sc)),
                  pl.BlockSpec((gb, 1, D), lambda i: (i, 0, k_sh))],
        out_specs=out_specs,
        out_shape=out_shape,
        compiler_params=_cparams(("parallel",)),
        name="norm_mod",
    )(x.reshape(t // SUB, SUB, D), gain.reshape(1, 1, D), mod, mod)
    return res if want_f32 else res[0]


def _mm_kernel(x_ref, w_ref, o_ref, wb_ref):
    @pl.when(pl.program_id(1) == 0)
    def _():
        wb_ref[...] = w_ref[...].astype(BF16)
    o_ref[...] = jnp.dot(x_ref[...], wb_ref[...], preferred_element_type=F32).astype(o_ref.dtype)


def _mm(x, w, tm=512, tn=512, out_dtype=F32):
    m, k = x.shape
    n = w.shape[1]
    return pl.pallas_call(
        _mm_kernel,
        grid=(n // tn, m // tm),
        in_specs=[pl.BlockSpec((tm, k), lambda j, i: (i, 0)),
                  pl.BlockSpec((k, tn), lambda j, i: (0, j))],
        out_specs=pl.BlockSpec((tm, tn), lambda j, i: (i, j)),
        out_shape=jax.ShapeDtypeStruct((m, n), out_dtype),
        scratch_shapes=[pltpu.VMEM((k, tn), BF16)],
        compiler_params=_cparams(("arbitrary", "arbitrary"), VMEM_BIG),
        name="mm",
    )(x, w)


def _mm_res_kernel(x_ref, w_ref, r_ref, g_ref, o_ref, wb_ref):
    @pl.when(pl.program_id(1) == 0)
    def _():
        wb_ref[...] = w_ref[...].astype(BF16)
    acc = jnp.dot(x_ref[...], wb_ref[...], preferred_element_type=F32)
    o_ref[...] = r_ref[...] + g_ref[...] * acc.reshape(o_ref.shape)


def _mm_res(x, w, res, mod, k_gate, tm=512, tn=512):
    m, k = x.shape
    n = w.shape[1]
    gb = tm // SUB
    nj = n // tn
    out = pl.pallas_call(
        _mm_res_kernel,
        grid=(nj, m // tm),
        in_specs=[pl.BlockSpec((tm, k), lambda j, i: (i, 0)),
                  pl.BlockSpec((k, tn), lambda j, i: (0, j)),
                  pl.BlockSpec((gb, SUB, tn), lambda j, i: (i, 0, j)),
                  pl.BlockSpec((gb, 1, tn), lambda j, i: (i, 0, k_gate * nj + j))],
        out_specs=pl.BlockSpec((gb, SUB, tn), lambda j, i: (i, 0, j)),
        out_shape=jax.ShapeDtypeStruct((m // SUB, SUB, n), F32),
        scratch_shapes=[pltpu.VMEM((k, tn), BF16)],
        compiler_params=_cparams(("arbitrary", "arbitrary"), VMEM_BIG),
        name="mm_res",
    )(x, w, res.reshape(m // SUB, SUB, n), mod)
    return out.reshape(m, n)


def _mm_swiglu_kernel(x_ref, wg_ref, wu_ref, o_ref, wgb_ref, wub_ref):
    @pl.when(pl.program_id(1) == 0)
    def _():
        wgb_ref[...] = wg_ref[...].astype(BF16)
        wub_ref[...] = wu_ref[...].astype(BF16)
    x = x_ref[...]
    a = jnp.dot(x, wgb_ref[...], preferred_element_type=F32)
    b = jnp.dot(x, wub_ref[...], preferred_element_type=F32)
    o_ref[...] = ((a * jax.nn.sigmoid(a)) * b).astype(BF16)


def _mm_swiglu(x, wg, wu, tm=512, tn=512):
    m, k = x.shape
    n = wg.shape[1]
    return pl.pallas_call(
        _mm_swiglu_kernel,
        grid=(n // tn, m // tm),
        in_specs=[pl.BlockSpec((tm, k), lambda j, i: (i, 0)),
                  pl.BlockSpec((k, tn), lambda j, i: (0, j)),
                  pl.BlockSpec((k, tn), lambda j, i: (0, j))],
        out_specs=pl.BlockSpec((tm, tn), lambda j, i: (i, j)),
        out_shape=jax.ShapeDtypeStruct((m, n), BF16),
        scratch_shapes=[pltpu.VMEM((k, tn), BF16), pltpu.VMEM((k, tn), BF16)],
        compiler_params=_cparams(("arbitrary", "arbitrary"), VMEM_BIG),
        name="mm_swiglu",
    )(x, wg, wu)


def _rope_tables(pos, half, offset, period):
    inv_freq = ROPE_THETA ** (-jnp.arange(half, dtype=F32) / half)
    ang = pos.astype(F32)[:, None] * inv_freq[None, :]
    cos, sin = jnp.cos(ang), jnp.sin(ang)
    n = pos.shape[0]
    seg_c = jnp.ones((n, period), F32)
    seg_c = seg_c.at[:, offset:offset + half].set(cos).at[:, offset + half:offset + 2 * half].set(cos)
    seg_m = jnp.zeros((n, period), F32).at[:, offset:offset + half].set(-sin)
    seg_p = jnp.zeros((n, period), F32).at[:, offset + half:offset + 2 * half].set(sin)
    rep = LANE // period
    return jnp.concatenate([jnp.tile(seg_c, (1, rep)), jnp.tile(seg_m, (1, rep)), jnp.tile(seg_p, (1, rep))], axis=1)


def _rope(x, tab, half):
    c = tab[:, 0:LANE]
    sm = tab[:, LANE:2 * LANE]
    sp = tab[:, 2 * LANE:3 * LANE]
    return x * c + pltpu.roll(x, LANE - half, 1) * sm + pltpu.roll(x, half, 1) * sp


def _rms(x, n_valid):
    return lax.rsqrt(jnp.sum(x * x, axis=-1, keepdims=True) * (1.0 / n_valid) + EPS)


def _lane_tiles(x):
    return [x[:, j * LANE:(j + 1) * LANE] for j in range(x.shape[1] // LANE)]


def _fold(parts, op):
    while len(parts) > 1:
        parts = [op(parts[i], parts[i + 1]) if i + 1 < len(parts) else parts[i] for i in range(0, len(parts), 2)]
    return parts[0]


def _softmax_step(s, m_prev, exp=jnp.exp2):
    tiles = _lane_tiles(s)
    m_new = jnp.maximum(m_prev, jnp.max(_fold(tiles, jnp.maximum), axis=1, keepdims=True))
    p_tiles = [exp(t - m_new) for t in tiles]
    row_sum = jnp.sum(_fold(p_tiles, jnp.add), axis=1, keepdims=True)
    return m_new, exp(m_prev - m_new), jnp.concatenate(p_tiles, axis=1), row_sum


def _proj0_post_kernel(z_ref, ta_ref, tb_ref, tk_ref, gq_ref, gk_ref, gqa_ref, gkv_ref, gqb_ref, gkb_ref,
                       wuq_ref, wuk_ref, wuv_ref, *out_refs, prompt):
    if prompt:
        q_ref, ka_ref, va_ref, lat_ref, kr_ref, ks_ref, vs_ref = out_refs
    else:
        q_ref, ka_ref, va_ref, lat_ref, kr_ref = out_refs
    ta = ta_ref[...]
    tb = tb_ref[...]
    tk = tk_ref[...]
    lane = lax.broadcasted_iota(I32, (1, LANE), 1)
    lo = lane < DH_A

    def put_q(s, val):
        if prompt:
            q_ref[0, s] = val.astype(BF16)
        else:
            q_ref[:, s * LANE:(s + 1) * LANE] = val

    for s in range(2 * H_A):
        x = z_ref[:, s * LANE:(s + 1) * LANE]
        y = (x * _rms(x, DH_A)) * gq_ref[...]
        put_q(s, _rope(y, ta, DH_A // 8) * (DH_A ** -0.5 * LOG2E))
    for kv in range(KV_A):
        x = z_ref[:, (16 + kv) * LANE:(17 + kv) * LANE]
        xx = x * x
        s_lo = jnp.sum(jnp.where(lo, xx, 0.0), axis=-1, keepdims=True)
        s_hi = jnp.sum(jnp.where(lo, 0.0, xx), axis=-1, keepdims=True)
        inv = jnp.where(lo, lax.rsqrt(s_lo * (1.0 / DH_A) + EPS), lax.rsqrt(s_hi * (1.0 / DH_A) + EPS))
        k = _rope((x * inv) * gk_ref[...], ta, DH_A // 8)
        v = z_ref[:, (18 + kv) * LANE:(19 + kv) * LANE]
        ka_ref[:, kv * LANE:(kv + 1) * LANE] = k
        va_ref[:, kv * LANE:(kv + 1) * LANE] = v
        if prompt:
            ks_ref[0, kv] = k.astype(BF16)
            vs_ref[0, kv] = v.astype(BF16)
    qc = z_ref[:, 20 * LANE:24 * LANE]
    qcn = (qc * _rms(qc, Q_LORA)) * gqa_ref[...]
    qb = jnp.dot(qcn.astype(BF16), wuq_ref[...], preferred_element_type=F32)
    for h in range(H_B):
        x = _rope(qb[:, h * LANE:(h + 1) * LANE], tb, ROPE_B // 2)
        y = (x * _rms(x, QK_B)) * gqb_ref[...]
        put_q(2 * H_A + h, y * (QK_B ** -0.5 * LOG2E))
    kvc = z_ref[:, 24 * LANE:26 * LANE]
    lat = (kvc * _rms(kvc, KV_LORA)) * gkv_ref[...]
    lat_ref[...] = lat
    kr = _rope(z_ref[:, 26 * LANE:27 * LANE], tk, ROPE_B // 2)
    kr_ref[...] = kr[:, 0:ROPE_B]
    if prompt:
        latb = lat.astype(BF16)
        kn = jnp.dot(latb, wuk_ref[...], preferred_element_type=F32)
        vv = jnp.dot(latb, wuv_ref[...], preferred_element_type=F32)
        kr_hi = pltpu.roll(kr, NOPE_B, 1)
        for h in range(H_B):
            x = kn[:, h * LANE:(h + 1) * LANE] + kr_hi
            ks_ref[0, KV_A + h] = ((x * _rms(x, QK_B)) * gkb_ref[...]).astype(BF16)
            vs_ref[0, KV_A + h] = vv[:, h * LANE:(h + 1) * LANE].astype(BF16)


def _proj0_post(z0, tabs, gains, wts, *, prompt, nb, seq, row0):
    ta, tb, tk = tabs
    tm = 256
    n_rows = nb * seq
    nt = n_rows // tm
    r0 = row0 // tm
    if prompt:
        per_b = seq // tm
        rows = lambda i: (i, 0)
        grid = (nt,)
        zmap = lambda i: (r0 + i, 0)
        q_spec = pl.BlockSpec((1, N_QS, tm, LANE), lambda i: (i // per_b, 0, i % per_b, 0))
        kv_spec = pl.BlockSpec((1, N_KS, tm, LANE), lambda i: (i // per_b, 0, i % per_b, 0))
        out_shape = [jax.ShapeDtypeStruct((nb, N_QS, seq, LANE), BF16)]
        out_specs = [q_spec]
    else:
        rows = lambda i: (i, 0)
        grid = (nt,)
        zmap = lambda i: (r0 + i, 0)
        out_shape = [jax.ShapeDtypeStruct((n_rows, N_QS * LANE), F32)]
        out_specs = [pl.BlockSpec((tm, N_QS * LANE), rows)]
    out_shape += [jax.ShapeDtypeStruct((n_rows, 2 * LANE), F32)] * 3 + [jax.ShapeDtypeStruct((n_rows, ROPE_B), F32)]
    out_specs += [pl.BlockSpec((tm, 2 * LANE), rows)] * 3 + [pl.BlockSpec((tm, ROPE_B), rows)]
    if prompt:
        out_shape += [jax.ShapeDtypeStruct((nb, N_KS, seq, LANE), BF16)] * 2
        out_specs += [kv_spec, kv_spec]
    tab_spec = pl.BlockSpec((tm, 3 * LANE), zmap)
    const = lambda a: pl.BlockSpec(a.shape, lambda i: (0,) * a.ndim)
    return pl.pallas_call(
        functools.partial(_proj0_post_kernel, prompt=prompt),
        grid=grid,
        in_specs=[pl.BlockSpec((tm, Z0_W), zmap), tab_spec, tab_spec, tab_spec]
                 + [const(a) for a in gains] + [const(a) for a in wts],
        out_specs=out_specs,
        out_shape=out_shape,
        compiler_params=_cparams(("parallel",)),
        name="proj0_post_p" if prompt else "proj0_post_s",
    )(z0, ta, tb, tk, *gains, *wts)


def _flash_kernel(q_ref, k0_ref, v0_ref, k1_ref, v1_ref, o_ref, m_ref, l_ref, acc_ref, *, tq):
    qi = pl.program_id(2)
    kv_refs = ((k0_ref, v0_ref), (k1_ref, v1_ref))
    m_ref[...] = jnp.full(m_ref.shape, NEG_INF, F32)
    l_ref[...] = jnp.zeros(l_ref.shape, F32)
    acc_ref[...] = jnp.zeros(acc_ref.shape, F32)

    def step(ki, masked):
        start = pl.multiple_of(ki * tq, tq)
        s = jnp.concatenate(
            [lax.dot_general(q_ref[0, j], k_ref[0, 0, pl.ds(start, tq), :], (((1,), (1,)), ((), ())),
                             preferred_element_type=F32) for j, (k_ref, _) in enumerate(kv_refs)], axis=0)
        if masked:
            r = lax.broadcasted_iota(I32, (2 * tq, tq), 0) & (tq - 1)
            c = lax.broadcasted_iota(I32, (2 * tq, tq), 1)
            s = jnp.where(c <= r, s, NEG_INF)
        m_new, alpha, p, row_sum = _softmax_step(s, m_ref[...])
        l_ref[...] = alpha * l_ref[...] + row_sum
        m_ref[...] = m_new
        p = p.astype(BF16)
        pv = [jnp.dot(p[j * tq:(j + 1) * tq], v_ref[0, 0, pl.ds(start, tq), :], preferred_element_type=F32)
              for j, (_, v_ref) in enumerate(kv_refs)]
        acc_ref[...] = alpha * acc_ref[...] + jnp.concatenate(pv, axis=0)

    def body(ki, carry):
        step(ki, False)
        return carry

    lax.fori_loop(0, qi, body, 0)
    step(qi, True)
    o = acc_ref[...] / l_ref[...]
    for j in range(2):
        o_ref[0, j] = o[j * tq:(j + 1) * tq]


def _flash(q, k, v, tq=512):
    nb, _, seq, _ = q.shape

    def kv_map(j):
        def m(b, p, i):
            s = 2 * p + j
            return (b, jnp.where(s < 2 * H_A, s // (2 * H_A // KV_A), s - (2 * H_A - KV_A)), 0, 0)
        return m

    kv_spec = lambda j: pl.BlockSpec((1, 1, seq, LANE), kv_map(j))
    return pl.pallas_call(
        functools.partial(_flash_kernel, tq=tq),
        grid=(nb, N_QS // 2, seq // tq),
        in_specs=[pl.BlockSpec((1, 2, tq, LANE), lambda b, p, i: (b, p, i, 0)),
                  kv_spec(0), kv_spec(0), kv_spec(1), kv_spec(1)],
        out_specs=pl.BlockSpec((1, 2, tq, LANE), lambda b, p, i: (b, p, i, 0)),
        out_shape=jax.ShapeDtypeStruct((nb, N_QS, seq, LANE), F32),
        scratch_shapes=[pltpu.VMEM((2 * tq, LANE), F32), pltpu.VMEM((2 * tq, LANE), F32),
                        pltpu.VMEM((2 * tq, LANE), F32)],
        compiler_params=_cparams(("parallel", "parallel", "arbitrary")),
        name="flash0",
    )(q, k, v, k, v)


def _diff_lambda(lam_ref):
    lf = lam_ref[...]
    a = jnp.sum(lf[0:1] * lf[1:2], axis=-1, keepdims=True)
    b = jnp.sum(lf[2:3] * lf[3:4], axis=-1, keepdims=True)
    return jnp.exp(a) - jnp.exp(b) + LAMBDA_INIT


def _attn0_post_kernel(o_ref, lam_ref, sub_ref, a_ref):
    lam = _diff_lambda(lam_ref)
    for h in range(H_A):
        d = o_ref[0, 2 * h] - lam * o_ref[0, 2 * h + 1]
        y = ((d * _rms(d, 2 * DH_A)) * sub_ref[...]) * (1.0 - LAMBDA_INIT)
        a_ref[:, h * LANE:(h + 1) * LANE] = y.astype(BF16)
    for h in range(H_B):
        a_ref[:, (H_A + h) * LANE:(H_A + h + 1) * LANE] = o_ref[0, 2 * H_A + h].astype(BF16)


def _attn0_post(o, lam_p, subln):
    nb, _, seq, _ = o.shape
    tm = 256
    per_b = seq // tm
    return pl.pallas_call(
        _attn0_post_kernel,
        grid=(nb * per_b,),
        in_specs=[pl.BlockSpec((1, N_QS, tm, LANE), lambda i: (i // per_b, 0, i % per_b, 0)),
                  pl.BlockSpec((4, DH_A), lambda i: (0, 0)),
                  pl.BlockSpec((1, LANE), lambda i: (0, 0))],
        out_specs=pl.BlockSpec((tm, D), lambda i: (i, 0)),
        out_shape=jax.ShapeDtypeStruct((nb * seq, D), BF16),
        compiler_params=_cparams(("parallel",)),
        name="attn0_post",
    )(o, lam_p, subln)


def _decode0_kernel(pt_ref, q_ref, kan_ref, van_ref, latn_ref, krn_ref, wukt_ref, wuk_ref, wuv_ref, gkb_ref,
                    lam_ref, sub_ref, *rest, n_chunks):
    pps = PAGES_PER_STEP
    ck = rest[0:pps]
    cv = rest[pps:2 * pps]
    cl = rest[2 * pps:3 * pps]
    cr = rest[3 * pps:4 * pps]
    o_ref = rest[4 * pps]
    ka_s, va_s, lat_s, krt_s, krq_s, qa_s, qx_s, m_s, l_s, acca_s, accb_s = rest[4 * pps + 1:]
    c = pl.program_id(1)
    n_rows = 2 * H_A * SUB
    n_rows_b = H_B * SUB

    @pl.when(c == 0)
    def _():
        m_s[...] = jnp.full(m_s.shape, NEG_INF, F32)
        l_s[...] = jnp.zeros(l_s.shape, F32)
        acca_s[...] = jnp.zeros(acca_s.shape, F32)
        accb_s[...] = jnp.zeros(accb_s.shape, F32)
        lane = lax.broadcasted_iota(I32, (SUB, LANE), 1)
        for s in range(2 * H_A):
            qa_s[s * SUB:(s + 1) * SUB, :] = q_ref[:, s * LANE:(s + 1) * LANE]
        for h in range(H_B):
            qk = q_ref[:, (2 * H_A + h) * LANE:(2 * H_A + h + 1) * LANE] * gkb_ref[...]
            qx_s[h * SUB:(h + 1) * SUB, 0:2 * LANE] = jnp.dot(qk, wukt_ref[h], preferred_element_type=F32)
            qx_s[h * SUB:(h + 1) * SUB, 2 * LANE:3 * LANE] = jnp.where(lane < ROPE_B, pltpu.roll(qk, NOPE_B, 1), 0.0)

    def process(n, causal):
        nt = (((1,), (1,)), ((), ()))
        n_all = n_rows + n_rows_b
        qa = qa_s[...].astype(BF16)
        s_a = [lax.dot_general(qa[kv * n_rows_b:(kv + 1) * n_rows_b], ka_s[0:n, kv * LANE:(kv + 1) * LANE], nt,
                               preferred_element_type=F32) for kv in range(KV_A)]
        latb = lat_s[0:n, :]
        qx = qx_s[...].astype(BF16)
        s_raw = (lax.dot_general(qx[:, 0:2 * LANE], latb, nt, preferred_element_type=F32)
                 + jnp.dot(qx[:, 2 * LANE:2 * LANE + ROPE_B], krt_s[:, 0:n], preferred_element_type=F32))
        kn = jnp.dot(latb, wuk_ref[...], preferred_element_type=F32)
        seg = (lax.broadcasted_iota(I32, (H_B, H_B * NOPE_B), 1) // NOPE_B
               == lax.broadcasted_iota(I32, (H_B, H_B * NOPE_B), 0)).astype(BF16)
        n2 = lax.dot_general(seg, (kn * kn).astype(BF16), nt, preferred_element_type=F32)
        n2 = n2 + jnp.dot(jnp.ones((H_B, ROPE_B), BF16), krq_s[:, 0:n], preferred_element_type=F32)
        rinv = lax.rsqrt(n2 * (1.0 / QK_B) + EPS)
        s_b = (s_raw.reshape(H_B, SUB, n) * rinv[:, None, :]).reshape(n_rows_b, n)
        s = jnp.concatenate(s_a + [s_b], axis=0)
        if causal:
            row = lax.broadcasted_iota(I32, (n_all, n), 0)
            col = lax.broadcasted_iota(I32, (n_all, n), 1)
            s = jnp.where(col <= (row & (SUB - 1)), s, NEG_INF)
        m_new, alpha, p, row_sum = _softmax_step(s, m_s[...])
        l_s[...] = alpha * l_s[...] + row_sum
        m_s[...] = m_new
        p = p.astype(BF16)
        pv_a = [jnp.dot(p[kv * n_rows_b:(kv + 1) * n_rows_b], va_s[0:n, kv * LANE:(kv + 1) * LANE],
                        preferred_element_type=F32) for kv in range(KV_A)]
        acca_s[...] = alpha[0:n_rows] * acca_s[...] + jnp.concatenate(pv_a, axis=0)
        alpha_b = alpha[n_rows:n_all]
        accb_s[...] = (jnp.concatenate([alpha_b, alpha_b], axis=1) * accb_s[...]
                       + jnp.dot(p[n_rows:n_all], latb, preferred_element_type=F32))

    for i in range(pps):
        r = pl.ds(i * PAGE, PAGE)
        for kv in range(KV_A):
            ka_s[r, kv * LANE:(kv + 1) * LANE] = ck[i][0, pl.ds(kv, PAGE, stride=KV_A), :].astype(BF16)
            va_s[r, kv * LANE:(kv + 1) * LANE] = cv[i][0, pl.ds(kv, PAGE, stride=KV_A), :].astype(BF16)
        lat_s[r, :] = cl[i][0].astype(BF16)
        kr = cr[i][0]
        krt_s[:, i * PAGE:(i + 1) * PAGE] = kr.astype(BF16)
        krq_s[:, i * PAGE:(i + 1) * PAGE] = (kr * kr).astype(BF16)
    process(pps * PAGE, False)

    @pl.when(c == n_chunks - 1)
    def _():
        pad = lambda x: jnp.concatenate([x, jnp.zeros((PAGE - SUB, x.shape[1]), F32)], axis=0)
        r = pl.ds(0, PAGE)
        ka_s[r, :] = pad(kan_ref[...]).astype(BF16)
        va_s[r, :] = pad(van_ref[...]).astype(BF16)
        lat_s[r, :] = pad(latn_ref[...]).astype(BF16)
        kr = krn_ref[0]
        krt_s[:, 0:PAGE] = kr.astype(BF16)
        krq_s[:, 0:PAGE] = (kr * kr).astype(BF16)
        process(PAGE, True)
        lam = _diff_lambda(lam_ref)
        oa = acca_s[...] / l_s[0:n_rows, :]
        for h in range(H_A):
            d = oa[2 * h * SUB:(2 * h + 1) * SUB] - lam * oa[(2 * h + 1) * SUB:(2 * h + 2) * SUB]
            o_ref[:, h * LANE:(h + 1) * LANE] = ((d * _rms(d, 2 * DH_A)) * sub_ref[...]) * (1.0 - LAMBDA_INIT)
        l_b = l_s[n_rows:n_rows + n_rows_b, :]
        ob = accb_s[...] / jnp.concatenate([l_b, l_b], axis=1)
        for h in range(H_B):
            o_ref[:, (H_A + h) * LANE:(H_A + h + 1) * LANE] = jnp.dot(
                ob[h * SUB:(h + 1) * SUB], wuv_ref[:, h * LANE:(h + 1) * LANE], preferred_element_type=F32)


def _decode0(page_table, q_s, ka_s, va_s, lat_s, kr_s, caches, wts, gkb, lam_p, subln):
    nb, n_pages = page_table.shape
    pps = PAGES_PER_STEP
    n_chunks = n_pages // pps
    n = pps * PAGE
    ck, cv, cl, cr = caches
    wukt, wuk, wuv = wts
    rowmap = lambda b, c, pt: (b, 0)
    const = lambda a: pl.BlockSpec(a.shape, lambda b, c, pt: (0,) * a.ndim)

    def page_specs(rows, width):
        return [pl.BlockSpec((1, rows, width), functools.partial(lambda b, c, pt, i: (pt[b, c * pps + i], 0, 0), i=i))
                for i in range(pps)]

    in_specs = ([pl.BlockSpec((SUB, N_QS * LANE), rowmap), pl.BlockSpec((SUB, 2 * LANE), rowmap),
                 pl.BlockSpec((SUB, 2 * LANE), rowmap), pl.BlockSpec((SUB, 2 * LANE), rowmap),
                 pl.BlockSpec((1, ROPE_B, PAGE), lambda b, c, pt: (b, 0, 0)), const(wukt), const(wuk), const(wuv),
                 const(gkb), const(lam_p), const(subln)]
                + page_specs(KV_A * PAGE, LANE) + page_specs(KV_A * PAGE, LANE)
                + page_specs(PAGE, 2 * LANE) + page_specs(ROPE_B, PAGE))
    grid_spec = pltpu.PrefetchScalarGridSpec(
        num_scalar_prefetch=1,
        grid=(nb, n_chunks),
        in_specs=in_specs,
        out_specs=pl.BlockSpec((SUB, D), rowmap),
        scratch_shapes=[pltpu.VMEM((n, 2 * LANE), BF16), pltpu.VMEM((n, 2 * LANE), BF16),
                        pltpu.VMEM((n, 2 * LANE), BF16), pltpu.VMEM((ROPE_B, n), BF16), pltpu.VMEM((ROPE_B, n), BF16),
                        pltpu.VMEM((2 * H_A * SUB, LANE), F32), pltpu.VMEM((H_B * SUB, 3 * LANE), F32),
                        pltpu.VMEM(((2 * H_A + H_B) * SUB, LANE), F32), pltpu.VMEM(((2 * H_A + H_B) * SUB, LANE), F32),
                        pltpu.VMEM((2 * H_A * SUB, LANE), F32), pltpu.VMEM((H_B * SUB, 2 * LANE), F32)])
    return pl.pallas_call(
        functools.partial(_decode0_kernel, n_chunks=n_chunks),
        grid_spec=grid_spec,
        out_shape=jax.ShapeDtypeStruct((nb * SUB, D), F32),
        compiler_params=_cparams(("parallel", "arbitrary"), VMEM_BIG),
        name="decode0",
    )(page_table, q_s, ka_s, va_s, lat_s, kr_s, wukt, wuk, wuv, gkb, lam_p, subln,
      *([ck] * pps), *([cv] * pps), *([cl] * pps), *([cr] * pps))


def _proj1_post_kernel(z_ref, tk_ref, gq_ref, gk_ref, q_ref, k_ref, kf_ref, v_ref):
    j = pl.program_id(1)
    tk = tk_ref[...]
    nh = N_DIL * H_C

    @pl.when(j == 0)
    def _():
        for h in range(nh):
            x = z_ref[:, h * LANE:(h + 1) * LANE]
            y = _rope((x * _rms(x, DH_C)) * gq_ref[...], tk, DH_C // 8)
            q_ref[:, h * LANE:(h + 1) * LANE] = (y * (DH_C ** -0.5)).astype(BF16)

    @pl.when(j == 1)
    def _():
        for h in range(nh):
            x = z_ref[:, h * LANE:(h + 1) * LANE]
            y = _rope((x * _rms(x, DH_C)) * gk_ref[...], tk, DH_C // 8)
            kf_ref[:, h * LANE:(h + 1) * LANE] = y
            k_ref[:, h * LANE:(h + 1) * LANE] = y.astype(BF16)

    @pl.when(j == 2)
    def _():
        v_ref[...] = z_ref[...].astype(BF16)


def _proj1_views_kernel(z_ref, tk_ref, gq_ref, gk_ref, kf_ref, *rest):
    outs, scr = rest[:3 * N_DIL], rest[3 * N_DIL]
    j = pl.program_id(1)
    tk = tk_ref[...]
    tm = z_ref.shape[0]
    w = H_C * DH_C
    nh = N_DIL * H_C

    def emit(part, h, y):
        g, hh = divmod(h, H_C)
        d = DIL_GROUPS[g][1]
        o_ref = outs[part * N_DIL + g]
        if d == 1:
            o_ref[0, :, hh * LANE:(hh + 1) * LANE] = y.astype(BF16)
        else:
            scr[...] = y
            for r in range(d):
                o_ref[0, :, r * w + hh * LANE:r * w + (hh + 1) * LANE] = (
                    scr[pl.ds(r, tm // d, stride=d), :].astype(BF16))

    @pl.when(j == 0)
    def _():
        for h in range(nh):
            x = z_ref[:, h * LANE:(h + 1) * LANE]
            y = _rope((x * _rms(x, DH_C)) * gq_ref[...], tk, DH_C // 8)
            emit(0, h, y * (DH_C ** -0.5))

    @pl.when(j == 1)
    def _():
        for h in range(nh):
            x = z_ref[:, h * LANE:(h + 1) * LANE]
            y = _rope((x * _rms(x, DH_C)) * gk_ref[...], tk, DH_C // 8)
            kf_ref[:, h * LANE:(h + 1) * LANE] = y
            emit(1, h, y)

    @pl.when(j == 2)
    def _():
        for h in range(nh):
            emit(2, h, z_ref[:, h * LANE:(h + 1) * LANE])


def _proj1_views(z1, tk, gq, gk, *, nb, seq):
    tm = 256
    w = H_C * DH_C
    per_b = seq // tm
    out_shape = [jax.ShapeDtypeStruct((nb * seq, N_DIL * w), F32)]
    out_specs = [pl.BlockSpec((tm, N_DIL * w), lambda i, j: (i, 0))]
    for _ in range(3):
        for _, d in DIL_GROUPS:
            out_shape.append(jax.ShapeDtypeStruct((nb, seq // d, d * w), BF16))
            out_specs.append(pl.BlockSpec((1, tm // d, d * w), lambda i, j: (i // per_b, i % per_b, 0)))
    res = pl.pallas_call(
        _proj1_views_kernel,
        grid=(nb * seq // tm, 3),
        in_specs=[pl.BlockSpec((tm, N_DIL * w), lambda i, j: (i, j)),
                  pl.BlockSpec((tm, 3 * LANE), lambda i, j: (i, 0)),
                  pl.BlockSpec((1, LANE), lambda i, j: (0, 0)),
                  pl.BlockSpec((1, LANE), lambda i, j: (0, 0))],
        out_specs=out_specs,
        out_shape=out_shape,
        scratch_shapes=[pltpu.VMEM((tm, LANE), F32)],
        compiler_params=_cparams(("parallel", "arbitrary")),
        name="proj1_views",
    )(z1, tk, gq, gk)
    return res[0], res[1:1 + N_DIL], res[1 + N_DIL:1 + 2 * N_DIL], res[1 + 2 * N_DIL:]


def _proj1_post(z1, tk, gq, gk, *, n_rows, row0):
    tm = 256
    w = N_DIL * H_C * DH_C
    r0 = row0 // tm
    rows = lambda i, j: (i, 0)
    return pl.pallas_call(
        _proj1_post_kernel,
        grid=(n_rows // tm, 3),
        in_specs=[pl.BlockSpec((tm, w), lambda i, j: (r0 + i, j)),
                  pl.BlockSpec((tm, 3 * LANE), lambda i, j: (r0 + i, 0)),
                  pl.BlockSpec((1, LANE), lambda i, j: (0, 0)),
                  pl.BlockSpec((1, LANE), lambda i, j: (0, 0))],
        out_specs=[pl.BlockSpec((tm, w), rows)] * 4,
        out_shape=[jax.ShapeDtypeStruct((n_rows, w), BF16), jax.ShapeDtypeStruct((n_rows, w), BF16),
                   jax.ShapeDtypeStruct((n_rows, w), F32), jax.ShapeDtypeStruct((n_rows, w), BF16)],
        compiler_params=_cparams(("parallel", "arbitrary")),
        name="proj1_post",
    )(z1, tk, gq, gk)


def _dil_prompt_kernel(q_ref, kc_ref, kp_ref, vc_ref, vp_ref, o_ref, lse_ref, *, tq):
    qi = pl.program_id(2)
    nt = (((1,), (1,)), ((), ()))
    heads = [slice(h * LANE, (h + 1) * LANE) for h in range(H_C)]
    n_rows = H_C * tq
    s = jnp.concatenate(
        [jnp.concatenate([lax.dot_general(q_ref[0, :, sl], kc_ref[0, :, sl], nt, preferred_element_type=F32),
                          lax.dot_general(q_ref[0, :, sl], kp_ref[0, :, sl], nt, preferred_element_type=F32)], axis=1)
         for sl in heads], axis=0)
    r = lax.broadcasted_iota(I32, (n_rows, 2 * tq), 0) & (tq - 1)
    c = lax.broadcasted_iota(I32, (n_rows, 2 * tq), 1)
    keep = jnp.logical_or(c <= r, jnp.logical_and(c - tq >= r, qi > 0))
    m, _, p, row_sum = _softmax_step(jnp.where(keep, s, NEG_INF), jnp.full((n_rows, LANE), NEG_INF, F32), exp=jnp.exp)
    l = jnp.zeros((n_rows, LANE), F32) + row_sum
    p = p.astype(BF16)
    lse = m + jnp.log(l)
    lane = lax.broadcasted_iota(I32, (tq, LANE), 1)
    lse_tile = jnp.zeros((tq, LANE), F32)
    for h, sl in enumerate(heads):
        rows = slice(h * tq, (h + 1) * tq)
        o = (jnp.dot(p[rows, 0:tq], vc_ref[0, :, sl], preferred_element_type=F32)
             + jnp.dot(p[rows, tq:2 * tq], vp_ref[0, :, sl], preferred_element_type=F32))
        o_ref[0, :, sl] = (o / l[rows]).astype(o_ref.dtype)
        lse_tile = jnp.where(lane == h, lse[rows], lse_tile)
    lse_ref[0] = lse_tile


def _dil_prompt(q, k, v, g, dil, nb, seq):
    tq = PAGE
    w = H_C * DH_C
    ns = seq // dil
    view = lambda a: a
    cur = lambda b, r, i: (b, i, r)
    prev = lambda b, r, i: (b, jnp.maximum(i - 1, 0), r)
    blk = lambda m: pl.BlockSpec((1, tq, w), m)
    o, lse = pl.pallas_call(
        functools.partial(_dil_prompt_kernel, tq=tq),
        grid=(nb, dil, ns // tq),
        in_specs=[blk(cur), blk(cur), blk(prev), blk(cur), blk(prev)],
        out_specs=[pl.BlockSpec((1, tq, w), lambda b, r, i: (b, i, r)),
                   pl.BlockSpec((1, tq, LANE), lambda b, r, i: (b, i, r))],
        out_shape=[jax.ShapeDtypeStruct((nb, ns, dil * w), BF16), jax.ShapeDtypeStruct((nb, ns, dil * LANE), F32)],
        compiler_params=_cparams(("parallel", "parallel", "parallel")),
        name=f"dil_prompt{g}",
    )(view(q), view(k), view(k), view(v), view(v))
    return o.reshape(nb * seq, w), lse.reshape(nb * seq, LANE)


def _dil_sample_kernel(q_ref, kn_ref, vn_ref, st_ref, tail_ref, new_ref, o_ref, lse_ref, ns_ref, m_s, l_s, acc_s,
                       *, g, dil, ch, n_chunks):
    c = pl.program_id(1)
    nt = (((1,), (1,)), ((), ()))
    w = H_C * DH_C
    rpw = 2 * H_C
    body = (ch - SUB) * rpw
    half = dil == 2 * SUB
    n_slab = ch // dil

    if half:
        ns_ref[0, :, 0:SUB * rpw, :] = st_ref[0, :, SUB * rpw:dil * rpw, :]
        if n_slab > 1:
            ns_ref[0, 0:n_slab - 1, SUB * rpw:dil * rpw, :] = st_ref[0, 1:n_slab, 0:SUB * rpw, :]
    else:
        ns_ref[0, 0:body] = st_ref[0, SUB * rpw:ch * rpw]

    @pl.when(c < n_chunks - 1)
    def _():
        if half:
            ns_ref[0, n_slab - 1, SUB * rpw:dil * rpw, :] = tail_ref[0, 0]
        else:
            ns_ref[0, body:ch * rpw] = tail_ref[0]

    @pl.when(c == n_chunks - 1)
    def _():
        if half:
            ns_ref[0, n_slab - 1, SUB * rpw:dil * rpw, :] = new_ref[0, 0]
        else:
            ns_ref[0, body:ch * rpw] = new_ref[0]

    @pl.when(c == 0)
    def _():
        m_s[...] = jnp.full(m_s.shape, NEG_INF, F32)
        l_s[...] = jnp.zeros(l_s.shape, F32)
        acc_s[...] = jnp.zeros(acc_s.shape, F32)

    n_rows = H_C * SUB
    heads = [slice(g * w + h * LANE, g * w + (h + 1) * LANE) for h in range(H_C)]

    def update(keys, vals, keep):
        s = jnp.concatenate([lax.dot_general(q_ref[:, heads[h]], keys[h], nt, preferred_element_type=F32)
                             for h in range(H_C)], axis=0)
        m_new, alpha, p, row_sum = _softmax_step(jnp.where(keep, s, NEG_INF), m_s[...], exp=jnp.exp)
        l_s[...] = alpha * l_s[...] + row_sum
        m_s[...] = m_new
        pv = [jnp.dot(p[h * SUB:(h + 1) * SUB], vals[h], preferred_element_type=F32) for h in range(H_C)]
        acc_s[...] = alpha * acc_s[...] + jnp.concatenate(pv, axis=0)

    if half:
        nk = n_slab * SUB
        t = lax.broadcasted_iota(I32, (n_rows, nk), 0) & (SUB - 1)
        j = lax.broadcasted_iota(I32, (n_rows, nk), 1)
        update([st_ref[0, :, pl.ds(h, SUB, stride=rpw), :].reshape(nk, LANE) for h in range(H_C)],
               [st_ref[0, :, pl.ds(H_C + h, SUB, stride=rpw), :].reshape(nk, LANE) for h in range(H_C)],
               (j & (SUB - 1)) == t)
    else:
        t = lax.broadcasted_iota(I32, (n_rows, ch), 0) & (SUB - 1)
        i = lax.broadcasted_iota(I32, (n_rows, ch), 1) + c * ch
        update([st_ref[0, pl.ds(h, ch, stride=rpw), :] for h in range(H_C)],
               [st_ref[0, pl.ds(H_C + h, ch, stride=rpw), :] for h in range(H_C)],
               jnp.logical_and(i >= t, ((i - t) & (dil - 1)) == 0))

    @pl.when(c == n_chunks - 1)
    def _():
        tt = lax.broadcasted_iota(I32, (n_rows, PAGE), 0) & (SUB - 1)
        tn = lax.broadcasted_iota(I32, (n_rows, PAGE), 1)
        pad = lambda x: jnp.concatenate([x, jnp.zeros((PAGE - SUB, LANE), F32)], axis=0)
        update([pad(kn_ref[:, heads[h]]) for h in range(H_C)], [pad(vn_ref[:, heads[h]]) for h in range(H_C)],
               jnp.logical_and(tn <= tt, ((tt - tn) & (dil - 1)) == 0))
        lane = lax.broadcasted_iota(I32, (SUB, LANE), 1)
        lse_tile = jnp.zeros((SUB, LANE), F32)
        for h in range(H_C):
            rows = pl.ds(h * SUB, SUB)
            l = l_s[rows, :]
            o_ref[:, h * LANE:(h + 1) * LANE] = acc_s[rows, :] / l
            lse_tile = jnp.where(lane == h, m_s[rows, :] + jnp.log(l), lse_tile)
        lse_ref[...] = lse_tile


def _dil_sample(q_s, kf_s, z1, state, new_rows, g, dil, row0):
    nb, win = state.shape[0], state.shape[1]
    w = H_C * DH_C
    rpw = 2 * H_C
    ch = min(win, 512)
    n_chunks = win // ch
    wq = N_DIL * w
    r0 = row0 // SUB
    if dil == 2 * SUB:
        n_slab = ch // dil
        st_view = state.reshape(nb, win // dil, dil * rpw, LANE)
        st_spec = pl.BlockSpec((1, n_slab, dil * rpw, LANE), lambda b, c: (b, c, 0, 0))
        tail_spec = pl.BlockSpec((1, 1, SUB * rpw, LANE),
                                 lambda b, c: (b, jnp.minimum((c + 1) * n_slab, win // dil - 1), 0, 0))
        new_view = new_rows.reshape(nb, 1, SUB * rpw, LANE)
        new_spec = pl.BlockSpec((1, 1, SUB * rpw, LANE), lambda b, c: (b, 0, 0, 0))
    else:
        tail_blocks = ch // SUB
        last_tail = win // SUB - 1
        st_view = state.reshape(nb, win * rpw, LANE)
        st_spec = pl.BlockSpec((1, ch * rpw, LANE), lambda b, c: (b, c, 0))
        tail_spec = pl.BlockSpec((1, SUB * rpw, LANE),
                                 lambda b, c: (b, jnp.minimum((c + 1) * tail_blocks, last_tail), 0))
        new_view = new_rows.reshape(nb, SUB * rpw, LANE)
        new_spec = pl.BlockSpec((1, SUB * rpw, LANE), lambda b, c: (b, 0, 0))
    o, lse, ns = pl.pallas_call(
        functools.partial(_dil_sample_kernel, g=g, dil=dil, ch=ch, n_chunks=n_chunks),
        grid=(nb, n_chunks),
        in_specs=[pl.BlockSpec((SUB, wq), lambda b, c: (b, 0)),
                  pl.BlockSpec((SUB, wq), lambda b, c: (b, 0)),
                  pl.BlockSpec((SUB, wq), lambda b, c: (r0 + b, 2)),
                  st_spec, tail_spec, new_spec],
        out_specs=[pl.BlockSpec((SUB, w), lambda b, c: (b, 0)), pl.BlockSpec((SUB, LANE), lambda b, c: (b, 0)),
                   st_spec],
        out_shape=[jax.ShapeDtypeStruct((nb * SUB, w), F32), jax.ShapeDtypeStruct((nb * SUB, LANE), F32),
                   jax.ShapeDtypeStruct(st_view.shape, F32)],
        scratch_shapes=[pltpu.VMEM((H_C * SUB, LANE), F32), pltpu.VMEM((H_C * SUB, LANE), F32),
                        pltpu.VMEM((H_C * SUB, LANE), F32)],
        compiler_params=_cparams(("parallel", "arbitrary")),
        name=f"dil_sample{g}",
    )(q_s, kf_s, z1, st_view, st_view, new_view)
    return o, lse, ns.reshape(state.shape)


def _dil_combine_kernel(o0_ref, o1_ref, o2_ref, l0_ref, l1_ref, l2_ref, a_ref):
    l0, l1, l2 = l0_ref[...], l1_ref[...], l2_ref[...]
    m = jnp.maximum(jnp.maximum(l0, l1), l2)
    w0, w1, w2 = jnp.exp(l0 - m), jnp.exp(l1 - m), jnp.exp(l2 - m)
    den = w0 + w1 + w2
    w0, w1, w2 = w0 / den, w1 / den, w2 / den
    for h in range(H_C):
        sl = slice(h * LANE, (h + 1) * LANE)
        a_ref[:, sl] = (w0[:, h:h + 1] * o0_ref[:, sl].astype(F32) + w1[:, h:h + 1] * o1_ref[:, sl].astype(F32)
                        + w2[:, h:h + 1] * o2_ref[:, sl].astype(F32)).astype(a_ref.dtype)


def _dil_combine(outs, lses, out_dtype, tm):
    n = outs[0].shape[0]
    w = H_C * DH_C
    rows = lambda i: (i, 0)
    return pl.pallas_call(
        _dil_combine_kernel,
        grid=(n // tm,),
        in_specs=[pl.BlockSpec((tm, w), rows)] * 3 + [pl.BlockSpec((tm, LANE), rows)] * 3,
        out_specs=pl.BlockSpec((tm, w), rows),
        out_shape=jax.ShapeDtypeStruct((n, w), out_dtype),
        compiler_params=_cparams(("parallel",)),
        name="dil_combine",
    )(*outs, *lses)


def _router_kernel(h_ref, r_ref, idx_ref, gate_ref):
    logits = jnp.dot(h_ref[...], r_ref[...].astype(BF16), preferred_element_type=F32)
    lane = lax.broadcasted_iota(I32, logits.shape, 1)
    lanef = lane.astype(F32)
    lg = jnp.where(lane < N_EXPERTS, logits, -jnp.inf)
    m1 = jnp.max(lg, axis=1, keepdims=True)
    i1 = jnp.min(jnp.where(lg == m1, lanef, float(LANE)), axis=1, keepdims=True)
    lg2 = jnp.where(lanef == i1, -jnp.inf, lg)
    m2 = jnp.max(lg2, axis=1, keepdims=True)
    i2 = jnp.min(jnp.where(lg2 == m2, lanef, float(LANE)), axis=1, keepdims=True)
    e = jnp.exp(m2 - m1)
    g1 = 1.0 / (1.0 + e)
    g2 = e / (1.0 + e)
    idx_ref[...] = jnp.where(lane == 0, i1, jnp.where(lane == 1, i2, 0.0)).astype(I32)
    gate_ref[...] = jnp.where(lane == 0, g1, jnp.where(lane == 1, g2, 0.0))


def _router(h, router):
    t = h.shape[0]
    tm = 512
    rp = jnp.pad(router, ((0, 0), (0, LANE - N_EXPERTS)))
    return pl.pallas_call(
        _router_kernel,
        grid=(t // tm,),
        in_specs=[pl.BlockSpec((tm, D), lambda i: (i, 0)), pl.BlockSpec((D, LANE), lambda i: (0, 0))],
        out_specs=[pl.BlockSpec((tm, LANE), lambda i: (i, 0))] * 2,
        out_shape=[jax.ShapeDtypeStruct((t, LANE), I32), jax.ShapeDtypeStruct((t, LANE), F32)],
        compiler_params=_cparams(("parallel",)),
        name="router",
    )(h, rp)


def _moe_gather_kernel(tok_ref, h_hbm, o_ref, buf, sem, *, tm):
    i = pl.program_id(0)
    n = pl.num_programs(0)

    def row_copy(tile, slot, r):
        return pltpu.make_async_copy(h_hbm.at[pl.ds(tok_ref[tile * tm + r], 1)], buf.at[slot, pl.ds(r, 1)],
                                     sem.at[slot])

    def issue(tile, slot):
        def body(r, carry):
            row_copy(tile, slot, 2 * r).start(priority=0)
            row_copy(tile, slot, 2 * r + 1).start(priority=1)
            return carry
        lax.fori_loop(0, tm // 2, body, 0)

    @pl.when(i == 0)
    def _():
        issue(0, 0)

    @pl.when(i + 1 < n)
    def _():
        issue(i + 1, (i + 1) % 2)

    slot = i % 2

    def wait(r, carry):
        row_copy(i, slot, r).wait()
        return carry

    lax.fori_loop(0, tm, wait, 0)
    o_ref[...] = buf[slot].astype(BF16)


def _moe_gather(tok_of_slot, hf):
    p = tok_of_slot.shape[0]
    tm = MOE_TM
    grid_spec = pltpu.PrefetchScalarGridSpec(
        num_scalar_prefetch=1, grid=(p // tm,),
        in_specs=[pl.BlockSpec(memory_space=pl.ANY)],
        out_specs=pl.BlockSpec((tm, D), lambda i, tok: (i, 0)),
        scratch_shapes=[pltpu.VMEM((2, tm, D), F32), pltpu.SemaphoreType.DMA((2,))])
    return pl.pallas_call(
        functools.partial(_moe_gather_kernel, tm=tm),
        grid_spec=grid_spec,
        out_shape=jax.ShapeDtypeStruct((p, D), BF16),
        compiler_params=_cparams(("arbitrary",)),
        name="moe_gather",
    )(tok_of_slot, hf)


def _moe_up_kernel(te_ref, nu_ref, x_ref, wg_ref, wu_ref, o_ref, wgb_ref, wub_ref):
    i = pl.program_id(1)
    changed = jnp.logical_or(i == 0, te_ref[i] != te_ref[jnp.maximum(i - 1, 0)])

    @pl.when(changed)
    def _():
        wgb_ref[...] = wg_ref[0].astype(BF16)
        wub_ref[...] = wu_ref[0].astype(BF16)

    @pl.when(i < nu_ref[0])
    def _():
        x = x_ref[...]
        a = jnp.dot(x, wgb_ref[...], preferred_element_type=F32)
        b = jnp.dot(x, wub_ref[...], preferred_element_type=F32)
        o_ref[...] = ((a * jax.nn.sigmoid(a)) * b).astype(BF16)

    @pl.when(i >= nu_ref[0])
    def _():
        o_ref[...] = jnp.zeros(o_ref.shape, BF16)


def _moe_up(te, nu, xs, wg, wu, tn=512):
    p = xs.shape[0]
    tm = MOE_TM
    n = wg.shape[2]
    grid_spec = pltpu.PrefetchScalarGridSpec(
        num_scalar_prefetch=2, grid=(n // tn, p // tm),
        in_specs=[pl.BlockSpec((tm, D), lambda j, i, te, nu: (i, 0)),
                  pl.BlockSpec((1, D, tn), lambda j, i, te, nu: (te[i], 0, j)),
                  pl.BlockSpec((1, D, tn), lambda j, i, te, nu: (te[i], 0, j))],
        out_specs=pl.BlockSpec((tm, tn), lambda j, i, te, nu: (i, j)),
        scratch_shapes=[pltpu.VMEM((D, tn), BF16), pltpu.VMEM((D, tn), BF16)])
    return pl.pallas_call(
        _moe_up_kernel, grid_spec=grid_spec,
        out_shape=jax.ShapeDtypeStruct((p, n), BF16),
        compiler_params=_cparams(("arbitrary", "arbitrary"), VMEM_BIG),
        name="moe_up",
    )(te, nu, xs, wg, wu)


def _moe_down_kernel(te_ref, nu_ref, x_ref, w_ref, o_ref, wb_ref):
    i = pl.program_id(1)
    changed = jnp.logical_or(i == 0, te_ref[i] != te_ref[jnp.maximum(i - 1, 0)])

    @pl.when(changed)
    def _():
        wb_ref[...] = w_ref[0].astype(BF16)

    @pl.when(i < nu_ref[0])
    def _():
        o_ref[...] = jnp.dot(x_ref[...], wb_ref[...], preferred_element_type=F32)

    @pl.when(i >= nu_ref[0])
    def _():
        o_ref[...] = jnp.zeros(o_ref.shape, F32)


def _moe_down(te, nu, act, wd, tn=512):
    p, k = act.shape
    tm = MOE_TM_DOWN
    n = wd.shape[2]
    grid_spec = pltpu.PrefetchScalarGridSpec(
        num_scalar_prefetch=2, grid=(n // tn, p // tm),
        in_specs=[pl.BlockSpec((tm, k), lambda j, i, te, nu: (i, 0)),
                  pl.BlockSpec((1, k, tn), lambda j, i, te, nu: (te[i], 0, j))],
        out_specs=pl.BlockSpec((tm, tn), lambda j, i, te, nu: (i, j)),
        scratch_shapes=[pltpu.VMEM((k, tn), BF16)])
    return pl.pallas_call(
        _moe_down_kernel, grid_spec=grid_spec,
        out_shape=jax.ShapeDtypeStruct((p, n), F32),
        compiler_params=_cparams(("arbitrary", "arbitrary"), VMEM_BIG),
        name="moe_down",
    )(te, nu, act, wd)


def _moe_combine_kernel(slot_ref, ys_hbm, x_ref, gt_ref, g_ref, o_ref, buf_a, buf_b, sem, *, tm):
    base = pl.program_id(0) * tm

    def issue(r, carry):
        pltpu.make_async_copy(ys_hbm.at[pl.ds(slot_ref[2 * (base + r)], 1)], buf_a.at[pl.ds(r, 1)], sem).start(priority=0)
        pltpu.make_async_copy(ys_hbm.at[pl.ds(slot_ref[2 * (base + r) + 1], 1)], buf_b.at[pl.ds(r, 1)], sem).start(priority=1)
        return carry

    def wait(r, carry):
        pltpu.make_async_copy(ys_hbm.at[pl.ds(0, 1)], buf_a.at[pl.ds(r, 1)], sem).wait()
        pltpu.make_async_copy(ys_hbm.at[pl.ds(0, 1)], buf_b.at[pl.ds(r, 1)], sem).wait()
        return carry

    lax.fori_loop(0, tm, issue, 0)
    lax.fori_loop(0, tm, wait, 0)
    gt = gt_ref[...]
    y = gt[:, 0:1] * buf_a[...] + gt[:, 1:2] * buf_b[...]
    o_ref[...] = x_ref[...] + g_ref[...] * y.reshape(o_ref.shape)


def _moe_combine(slot_of_assign, ys, x, gates, mod, k_gate):
    t = x.shape[0]
    tm = 256
    gb = tm // SUB
    grid_spec = pltpu.PrefetchScalarGridSpec(
        num_scalar_prefetch=1, grid=(t // tm,),
        in_specs=[pl.BlockSpec(memory_space=pl.ANY),
                  pl.BlockSpec((gb, SUB, D), lambda i, sl: (i, 0, 0)),
                  pl.BlockSpec((tm, LANE), lambda i, sl: (i, 0)),
                  pl.BlockSpec((gb, 1, D), lambda i, sl: (i, 0, k_gate))],
        out_specs=pl.BlockSpec((gb, SUB, D), lambda i, sl: (i, 0, 0)),
        scratch_shapes=[pltpu.VMEM((tm, D), F32), pltpu.VMEM((tm, D), F32), pltpu.SemaphoreType.DMA(())])
    out = pl.pallas_call(
        functools.partial(_moe_combine_kernel, tm=tm),
        grid_spec=grid_spec,
        out_shape=jax.ShapeDtypeStruct((t // SUB, SUB, D), F32),
        compiler_params=_cparams(("arbitrary",)),
        name="moe_combine",
    )(slot_of_assign, ys, x.reshape(t // SUB, SUB, D), gates, mod)
    return out.reshape(t, D)


def _moe_plan(idx, t):
    tm = MOE_TM
    n_assign = 2 * t
    p = n_assign + N_EXPERTS * tm
    e_flat = idx[:, 0:2].reshape(n_assign)
    onehot = (e_flat[:, None] == jnp.arange(N_EXPERTS, dtype=I32)[None, :]).astype(I32)
    csum = jnp.cumsum(onehot, axis=0)
    rank = jnp.take_along_axis(csum, e_flat[:, None], axis=1)[:, 0] - 1
    counts = csum[-1]
    padded = ((counts + tm - 1) // tm) * tm
    ends = jnp.cumsum(padded)
    slot = (ends - padded)[e_flat] + rank
    tok_of_slot = jnp.zeros((p,), I32).at[slot].set(jnp.arange(n_assign, dtype=I32) // 2)
    tile_start = jnp.arange(p // tm, dtype=I32) * tm
    te = jnp.minimum(jnp.searchsorted(ends, tile_start, side="right").astype(I32), N_EXPERTS - 1)
    n_used = (ends[-1] // tm).astype(I32).reshape(1)
    return slot.astype(I32), tok_of_slot, te, n_used


def kernel(x_prompt, x_sample, c_prompt, c_sample, page_table, cache_a_k, cache_a_v, cache_b_lat, cache_b_krope,
           state_c_win0, state_c_win1, state_c_win2, ada_w, ada_b, norm_mix, norm_ffn, l0_w_in, l0_a_qnorm,
           l0_a_knorm, l0_a_lambda, l0_a_subln, l0_b_qa_norm, l0_b_w_uq, l0_b_kv_norm, l0_b_w_ukv, l0_b_qnorm,
           l0_b_knorm, l0_w_out, l0_ffn_gate, l0_ffn_up, l0_ffn_down, l1_w_in, l1_c_qnorm, l1_c_knorm, l1_w_out,
           l1_router, l1_moe_gate, l1_moe_up, l1_moe_down):
    nbp, seq, _ = x_prompt.shape
    nbs, dseq, _ = x_sample.shape
    assert dseq == SUB
    n_pages = page_table.shape[1]
    past = n_pages * PAGE
    tp, ts = nbp * seq, nbs * dseq
    t = tp + ts
    n_pool = cache_a_k.shape[0]

    nc = nbp + nbs
    ncp = -(-nc // SUB) * SUB
    c_all = jnp.pad(jnp.concatenate([c_prompt, c_sample], axis=0), ((0, ncp - nc), (0, 0)))
    mods = _adaln_all(c_all, ada_w, ada_b)
    mod = [jnp.concatenate([jnp.broadcast_to(mods[l, b][None, :], (seq // SUB, 6 * D)) for b in range(nbp)]
                           + [mods[l, nbp:nbp + nbs]], axis=0)[:, None, :] for l in range(DEPTH)]

    x = jnp.concatenate([x_prompt.reshape(tp, D), x_sample.reshape(ts, D)], axis=0)
    pos = jnp.concatenate([jnp.tile(jnp.arange(seq, dtype=I32), nbp), jnp.tile(past + jnp.arange(dseq, dtype=I32), nbs)])
    tab_a = _rope_tables(pos, DH_A // 8, 0, DH_A)
    tab_k = _rope_tables(pos, ROPE_B // 2, 0, LANE)
    tab_b = _rope_tables(pos, ROPE_B // 2, NOPE_B, LANE)

    eye2 = jnp.eye(2, dtype=F32)
    w_qa = l0_w_in[:, :1024].reshape(D, H_A, 2, 1, DH_A) * eye2[None, None, :, :, None]
    w0 = jnp.concatenate([w_qa.reshape(D, 2 * H_A * LANE), l0_w_in[:, 1024:2336],
                          jnp.zeros((D, LANE - ROPE_B), F32)], axis=1)
    padl = lambda a, n: jnp.pad(a, ((0, 0),) * (a.ndim - 1) + ((0, n),))
    wuq = padl(l0_b_w_uq.reshape(Q_LORA, H_B, QK_B), LANE - QK_B).reshape(Q_LORA, H_B * LANE).astype(BF16)
    wukv = l0_b_w_ukv.reshape(KV_LORA, H_B, NOPE_B + VH_B)
    wuk_pad = padl(wukv[:, :, :NOPE_B], LANE - NOPE_B).reshape(KV_LORA, H_B * LANE).astype(BF16)
    wuk = wukv[:, :, :NOPE_B].reshape(KV_LORA, H_B * NOPE_B).astype(BF16)
    wuv = wukv[:, :, NOPE_B:].reshape(KV_LORA, H_B * VH_B).astype(BF16)
    wukt = jnp.pad(jnp.transpose(wukv[:, :, :NOPE_B], (1, 2, 0)), ((0, 0), (0, LANE - NOPE_B), (0, 0)))
    g_qa = jnp.tile(l0_a_qnorm, 2).reshape(1, LANE)
    g_ka = jnp.tile(l0_a_knorm, 2).reshape(1, LANE)
    g_qb = padl(l0_b_qnorm, LANE - QK_B).reshape(1, LANE)
    g_kb = padl(l0_b_knorm, LANE - QK_B).reshape(1, LANE)
    gains0 = (g_qa, g_ka, l0_b_qa_norm.reshape(1, Q_LORA), l0_b_kv_norm.reshape(1, KV_LORA), g_qb, g_kb)
    subln = l0_a_subln.reshape(1, LANE)

    h = _norm_mod(x, norm_mix[0], mod[0], 1, 0)
    z0 = _mm(h, w0, tm=1024, tn=384)
    tabs = (tab_a, tab_b, tab_k)
    q_p, ka_p, va_p, lat_p, kr_p, ks_p, vs_p = _proj0_post(
        z0, tabs, gains0, (wuq, wuk_pad, wuv), prompt=True, nb=nbp, seq=seq, row0=0)
    q_s, ka_s, va_s, lat_s, kr_s = _proj0_post(
        z0, tabs, gains0, (wuq, wuk_pad, wuv), prompt=False, nb=nbs, seq=dseq, row0=tp)
    o_p = _flash(q_p, ks_p, vs_p)
    a_p = _attn0_post(o_p, l0_a_lambda, subln)
    caches = (cache_a_k.reshape(n_pool, PAGE * KV_A, LANE), cache_a_v.reshape(n_pool, PAGE * KV_A, LANE),
              cache_b_lat, jnp.swapaxes(cache_b_krope, 1, 2))
    krt_new = jnp.pad(jnp.swapaxes(kr_s.reshape(nbs, dseq, ROPE_B), 1, 2), ((0, 0), (0, 0), (0, PAGE - dseq)))
    wuv_f = wukv[:, :, NOPE_B:].reshape(KV_LORA, H_B * VH_B)
    a_s = _decode0(page_table, q_s, ka_s, va_s, lat_s, krt_new, caches, (wukt, wuk, wuv_f), g_kb, l0_a_lambda, subln)
    a0 = jnp.concatenate([a_p, a_s.astype(BF16)], axis=0)
    x = _mm_res(a0, l0_w_out, x, mod[0], 2, tm=1024)
    h = _norm_mod(x, norm_ffn[0], mod[0], 4, 3)
    act = _mm_swiglu(h, l0_ffn_gate, l0_ffn_up, tm=1024)
    x = _mm_res(act, l0_ffn_down, x, mod[0], 5)

    h = _norm_mod(x, norm_mix[1], mod[1], 1, 0)
    z1 = _mm(h, l1_w_in, tm=1024, tn=512)
    gq1 = l1_c_qnorm.reshape(1, LANE)
    gk1 = l1_c_knorm.reshape(1, LANE)
    kf1p, q1p, k1p, v1p = _proj1_views(z1, tab_k, gq1, gk1, nb=nbp, seq=seq)
    q1s, _, kf1s, _ = _proj1_post(z1, tab_k, gq1, gk1, n_rows=ts, row0=tp)
    q1s_f = q1s.astype(F32)
    states = (state_c_win0, state_c_win1, state_c_win2)
    w = H_C * DH_C
    v1f = z1[:, 2 * N_DIL * w:]
    new_k = kf1s.reshape(nbs, dseq, N_DIL, H_C, DH_C)
    new_v = v1f[tp:].reshape(nbs, dseq, N_DIL, H_C, DH_C)
    outs_p, lses_p, outs_s, lses_s, win_s = [], [], [], [], []
    for g, (_, dil) in enumerate(DIL_GROUPS):
        o, lse = _dil_prompt(q1p[g], k1p[g], v1p[g], g, dil, nbp, seq)
        outs_p.append(o)
        lses_p.append(lse)
        new_rows = jnp.stack([new_k[:, :, g], new_v[:, :, g]], axis=2)
        o, lse, ns = _dil_sample(q1s_f, kf1s, z1, states[g], new_rows, g, dil, tp)
        outs_s.append(o)
        lses_s.append(lse)
        win_s.append(ns)
    a1 = jnp.concatenate([_dil_combine(outs_p, lses_p, BF16, 256),
                          _dil_combine(outs_s, lses_s, F32, 256).astype(BF16)], axis=0)
    x = _mm_res(a1, l1_w_out, x, mod[1], 2, tm=1024)
    h, hf = _norm_mod(x, norm_ffn[1], mod[1], 4, 3, want_f32=True)
    idx, gates = _router(h, l1_router)
    slot, tok_of_slot, te, n_used = _moe_plan(idx, t)
    xs = _moe_gather(tok_of_slot, hf)
    act = _moe_up(te, n_used, xs, l1_moe_gate, l1_moe_up)
    sub_tiles = MOE_TM // MOE_TM_DOWN
    ys = _moe_down(jnp.repeat(te, sub_tiles), n_used * sub_tiles, act, l1_moe_down)
    x = _moe_combine(slot, ys, x, gates, mod[1], 5)

    y_p = x[:tp].reshape(nbp, seq, D)
    y_s = x[tp:].reshape(nbs, dseq, D)
    win_p = []
    for g, (win, _) in enumerate(DIL_GROUPS):
        wl = min(win, seq)
        kk = kf1p.reshape(nbp, seq, N_DIL, H_C, DH_C)[:, seq - wl:, g]
        vv = v1f[:tp].reshape(nbp, seq, N_DIL, H_C, DH_C)[:, seq - wl:, g]
        win_p.append(jnp.stack([kk, vv], axis=2))
    return (y_p, y_s,
            ka_p.reshape(nbp, seq, KV_A, 2 * DH_A), ka_s.reshape(nbs, dseq, KV_A, 2 * DH_A),
            va_p.reshape(nbp, seq, KV_A, 2 * DH_A), va_s.reshape(nbs, dseq, KV_A, 2 * DH_A),
            lat_p.reshape(nbp, seq, KV_LORA), lat_s.reshape(nbs, dseq, KV_LORA),
            kr_p.reshape(nbp, seq, ROPE_B), kr_s.reshape(nbs, dseq, ROPE_B),
            win_p[0], win_s[0], win_p[1], win_s[1], win_p[2], win_s[2])
```
